```python
import math
import jax
import jax.numpy as jnp
from jax import lax
import numpy as np

D_MODEL = 2048
BATCH = 4
SEQ = 4096
DEPTH = 2

NORM_EPS = 1e-6
CONV_K = 4

ATTN_HEADS = 4
ATTN_QK_DIM = 64
ATTN_V_DIM = 2 * ATTN_QK_DIM
ATTN_WIDTH = ATTN_HEADS * ATTN_V_DIM
Q_BLOCK = 128
REL_BUCKETS = 32
REL_MAX_DIST = 128

GDN_HEADS = 6
GDN_DIM = 128
GDN_WIDTH = GDN_HEADS * GDN_DIM
GDN_CHUNK = 64

SSM_HEADS = 12
SSM_HEAD_DIM = 64
SSM_GROUPS = 2
SSM_STATE = 128
SSM_WIDTH = SSM_HEADS * SSM_HEAD_DIM
SSM_XBC = SSM_WIDTH + 2 * SSM_GROUPS * SSM_STATE
SSM_CHUNK = 128

MIX_WIDTH = ATTN_WIDTH + GDN_WIDTH + SSM_WIDTH
IN_SIZES = (ATTN_WIDTH, ATTN_WIDTH, ATTN_WIDTH,
            3 * GDN_WIDTH, GDN_WIDTH, GDN_HEADS, GDN_HEADS,
            SSM_WIDTH, SSM_XBC, SSM_HEADS)
D_IN = 3 * ATTN_WIDTH + 4 * GDN_WIDTH + 2 * GDN_HEADS + SSM_WIDTH + SSM_XBC + SSM_HEADS

D_FF = 5632
N_EXPERTS = 8
TOP_K = 2
D_FF_EXPERT = 5632

kernel_name = 'hybrid_diffattn_gdn_ssd_moe_trunk'


def rmsnorm(x, gain):
    xf = x.astype(jnp.float32)
    y = xf * lax.rsqrt(jnp.mean(xf * xf, axis=-1, keepdims=True) + NORM_EPS)
    return (y * gain.astype(jnp.float32)).astype(x.dtype)


def l2norm(x):
    return x * lax.rsqrt(jnp.sum(x * x, axis=-1, keepdims=True) + NORM_EPS)


def causal_dwconv(x, w):
    c = x.shape[-1]
    return lax.conv_general_dilated(x, w[:, None, :], window_strides=(1,), padding=[(CONV_K - 1, 0)],
                                    dimension_numbers=('NWC', 'WIO', 'NWC'), feature_group_count=c)


def swiglu(h, wg, wu, wd):
    return (jax.nn.silu(h @ wg) * (h @ wu)) @ wd


def split_in_proj(proj):
    pieces, start = [], 0
    for size in IN_SIZES:
        pieces.append(proj[..., start:start + size])
        start += size
    return pieces


def t5_bucket(n):
    max_exact = REL_BUCKETS // 2
    nf = jnp.maximum(n, max_exact).astype(jnp.float32)
    large = max_exact + (jnp.log(nf / max_exact) / math.log(REL_MAX_DIST / max_exact)
                         * (REL_BUCKETS - max_exact)).astype(jnp.int32)
    return jnp.where(n < max_exact, n, jnp.minimum(large, REL_BUCKETS - 1))


def diff_attention(q, k, v, q_gain, k_gain, lam, sub_gain, rel_bias, layer_idx):
    B, S, H, _, d = q.shape
    lam_init = 0.8 - 0.6 * math.exp(-0.3 * layer_idx)
    lam = lam.astype(jnp.float32)
    lam_full = jnp.exp(jnp.sum(lam[0] * lam[1])) - jnp.exp(jnp.sum(lam[2] * lam[3])) + lam_init
    q = rmsnorm(q, q_gain) * (d ** -0.5)
    k = rmsnorm(k, k_gain)
    n_blk = S // Q_BLOCK
    q_blocks = q.reshape(B, n_blk, Q_BLOCK, H, 2, d).swapaxes(0, 1)
    k_pos = jnp.arange(S, dtype=jnp.int32)
    table = rel_bias.astype(jnp.float32)

    def one_block(args):
        q_blk, blk = args
        q_pos = blk * Q_BLOCK + jnp.arange(Q_BLOCK, dtype=jnp.int32)
        rel = q_pos[:, None] - k_pos[None, :]
        bias = jnp.take(table, t5_bucket(jnp.maximum(rel, 0)), axis=0)
        bias = jnp.where((rel >= 0)[..., None], bias, -jnp.inf)
        logits = jnp.einsum('bqhmd,bkhmd->bhmqk', q_blk, k, preferred_element_type=jnp.float32)
        logits = logits + jnp.transpose(bias, (2, 0, 1))[None, :, None]
        p = jax.nn.softmax(logits, axis=-1)
        attn = p[:, :, 0] - lam_full * p[:, :, 1]
        return jnp.einsum('bhqk,bkhe->bqhe', attn.astype(v.dtype), v)

    o = lax.map(one_block, (q_blocks, jnp.arange(n_blk, dtype=jnp.int32)))
    o = o.swapaxes(0, 1).reshape(B, S, H, 2 * d)
    o = rmsnorm(o, sub_gain) * (1.0 - lam_init)
    return o.reshape(B, S, H * 2 * d)


def gated_deltanet(qkv, gate, beta_raw, a_raw, conv_w, A_log, dt_bias, o_gain):
    B, S, _ = qkv.shape
    H, D, C = GDN_HEADS, GDN_DIM, GDN_CHUNK
    N = S // C
    qkv = jax.nn.silu(causal_dwconv(qkv, conv_w)).astype(jnp.float32)
    q, k, v = [t.reshape(B, S, H, D) for t in jnp.split(qkv, 3, axis=-1)]
    q = l2norm(q) * (D ** -0.5)
    k = l2norm(k)
    beta = jax.nn.sigmoid(beta_raw.astype(jnp.float32))
    g = -jnp.exp(A_log.astype(jnp.float32)) * jax.nn.softplus(a_raw.astype(jnp.float32) + dt_bias.astype(jnp.float32))
    chv = lambda t: t.reshape(B, N, C, H, D).transpose(0, 3, 1, 2, 4)
    chs = lambda t: t.reshape(B, N, C, H).transpose(0, 3, 1, 2)
    q, k, v = chv(q), chv(k), chv(v)
    beta = chs(beta)
    gc = jnp.cumsum(chs(g), axis=-1)
    idx = jnp.arange(C)
    diff = gc[..., :, None] - gc[..., None, :]
    strict = idx[:, None] > idx[None, :]
    causal = idx[:, None] >= idx[None, :]
    kb = k * beta[..., None]
    A = jnp.einsum('bhncd,bhnsd->bhncs', kb, k) * jnp.exp(jnp.where(strict, diff, -jnp.inf))
    M = A + jnp.eye(C, dtype=jnp.float32)
    rhs = jnp.concatenate([v * beta[..., None], kb * jnp.exp(gc)[..., None]], axis=-1)
    sol = lax.linalg.triangular_solve(M, rhs, left_side=True, lower=True, unit_diagonal=True)
    u, w = sol[..., :D], sol[..., D:]
    qk = jnp.einsum('bhncd,bhnsd->bhncs', q, k) * jnp.exp(jnp.where(causal, diff, -jnp.inf))
    q_dec = q * jnp.exp(gc)[..., None]
    k_dec = k * jnp.exp(gc[..., -1:] - gc)[..., None]
    decay = jnp.exp(gc[..., -1])

    def step(state, xs):
        u_n, w_n, q_n, k_n, qk_n, d_n = xs
        v_new = u_n - jnp.einsum('bhcd,bhde->bhce', w_n, state)
        o_n = jnp.einsum('bhcd,bhde->bhce', q_n, state) + jnp.einsum('bhcs,bhse->bhce', qk_n, v_new)
        state = state * d_n[..., None, None] + jnp.einsum('bhcd,bhce->bhde', k_n, v_new)
        return state, o_n

    mv = lambda t: jnp.moveaxis(t, 2, 0)
    s0 = jnp.zeros((B, H, D, D), jnp.float32)
    _, o = lax.scan(step, s0, (mv(u), mv(w), mv(q_dec), mv(k_dec), mv(qk), mv(decay)))
    o = o.transpose(1, 0, 3, 2, 4).reshape(B, S, H, D)
    o = rmsnorm(o, o_gain) * jax.nn.silu(gate.astype(jnp.float32)).reshape(B, S, H, D)
    return o.reshape(B, S, GDN_WIDTH).astype(gate.dtype)


def mamba2_ssd(z, xbc, dt_raw, conv_w, conv_b, A_log, dt_bias, D_skip, norm_gain):
    B, S, _ = xbc.shape
    G, R, P, Ns, L = SSM_GROUPS, SSM_HEADS // SSM_GROUPS, SSM_HEAD_DIM, SSM_STATE, SSM_CHUNK
    Nc = S // L
    xbc = jax.nn.silu(causal_dwconv(xbc, conv_w) + conv_b).astype(jnp.float32)
    x, Bm, Cm = jnp.split(xbc, [SSM_WIDTH, SSM_WIDTH + G * Ns], axis=-1)
    x = x.reshape(B, Nc, L, G, R, P)
    Bm = Bm.reshape(B, Nc, L, G, Ns)
    Cm = Cm.reshape(B, Nc, L, G, Ns)
    dt = jax.nn.softplus(dt_raw.astype(jnp.float32) + dt_bias.astype(jnp.float32)).reshape(B, Nc, L, G, R)
    A = -jnp.exp(A_log.astype(jnp.float32)).reshape(G, R)
    X = x * dt[..., None]
    Acum = jnp.cumsum(A * dt, axis=2)
    At = Acum.transpose(0, 1, 3, 4, 2)
    tril = jnp.arange(L)[:, None] >= jnp.arange(L)[None, :]
    Lmat = jnp.exp(jnp.where(tril, At[..., :, None] - At[..., None, :], -jnp.inf))
    CB = jnp.einsum('bclgn,bcsgn->bcgls', Cm, Bm)
    y_diag = jnp.einsum('bcgls,bcgrls,bcsgrp->bclgrp', CB, Lmat, X)
    decay_states = jnp.exp(Acum[:, :, -1:] - Acum)
    states = jnp.einsum('bclgn,bclgr,bclgrp->bcgrpn', Bm, decay_states, X)
    chunk_decay = jnp.exp(Acum[:, :, -1])

    def step(h, xs):
        st, d = xs
        return h * d[..., None, None] + st, h

    h0 = jnp.zeros((B, G, R, P, Ns), jnp.float32)
    _, prev = lax.scan(step, h0, (jnp.moveaxis(states, 1, 0), jnp.moveaxis(chunk_decay, 1, 0)))
    prev = jnp.moveaxis(prev, 0, 1)
    y_off = jnp.einsum('bclgn,bcgrpn,bclgr->bclgrp', Cm, prev, jnp.exp(Acum))
    y = y_diag + y_off + D_skip.astype(jnp.float32).reshape(G, R)[..., None] * x
    y = y.reshape(B, S, SSM_WIDTH) * jax.nn.silu(z.astype(jnp.float32))
    y = rmsnorm(y.reshape(B, S, G, SSM_WIDTH // G), norm_gain.reshape(G, SSM_WIDTH // G))
    return y.reshape(B, S, SSM_WIDTH).astype(z.dtype)


def moe_swiglu(h, router, wg, wu, wd):
    logits = (h @ router).astype(jnp.float32)
    top_vals, top_idx = lax.top_k(logits, TOP_K)
    top_w = jax.nn.softmax(top_vals, axis=-1)
    gates = jnp.sum(jax.nn.one_hot(top_idx, N_EXPERTS, dtype=jnp.float32) * top_w[..., None], axis=-2)
    y = jnp.zeros_like(h)
    for e in range(N_EXPERTS):
        y = y + gates[..., e:e + 1].astype(h.dtype) * swiglu(h, wg[e], wu[e], wd[e])
    return y


def setup_inputs(seed: int = 0) -> dict:
    key = jax.random.key(seed)
    ks = iter(jax.random.split(key, 32))
    f32 = jnp.float32
    n_dense = (DEPTH + 1) // 2
    n_moe = DEPTH // 2

    def normal(shape, scale):
        return jax.random.normal(next(ks), shape, f32) * scale

    def gain(shape):
        return 1.0 + normal(shape, 0.02)

    def a_log(shape):
        return jnp.log(jax.random.uniform(next(ks), shape, f32, 1.0, 16.0))

    def dt_bias(shape):
        dt = jnp.exp(jax.random.uniform(next(ks), shape, f32, math.log(1e-3), math.log(1e-1)))
        return dt + jnp.log(-jnp.expm1(-dt))

    return {
        'x': normal((BATCH, SEQ, D_MODEL), 1.0),
        'rel_bias': normal((REL_BUCKETS, ATTN_HEADS), 0.5),
        'mix_norm': gain((DEPTH, D_MODEL)),
        'w_in': normal((DEPTH, D_MODEL, D_IN), D_MODEL ** -0.5),
        'attn_q_gain': gain((DEPTH, ATTN_QK_DIM)),
        'attn_k_gain': gain((DEPTH, ATTN_QK_DIM)),
        'attn_lambda': normal((DEPTH, 4, ATTN_QK_DIM), 0.1),
        'attn_sub_gain': gain((DEPTH, ATTN_V_DIM)),
        'gdn_conv_w': normal((DEPTH, CONV_K, 3 * GDN_WIDTH), CONV_K ** -0.5),
        'gdn_A_log': a_log((DEPTH, GDN_HEADS)),
        'gdn_dt_bias': dt_bias((DEPTH, GDN_HEADS)),
        'gdn_o_gain': gain((DEPTH, GDN_DIM)),
        'ssm_conv_w': normal((DEPTH, CONV_K, SSM_XBC), CONV_K ** -0.5),
        'ssm_conv_b': normal((DEPTH, SSM_XBC), 0.02),
        'ssm_A_log': a_log((DEPTH, SSM_HEADS)),
        'ssm_dt_bias': dt_bias((DEPTH, SSM_HEADS)),
        'ssm_D': gain((DEPTH, SSM_HEADS)),
        'ssm_norm_gain': gain((DEPTH, SSM_WIDTH)),
        'w_out': normal((DEPTH, MIX_WIDTH, D_MODEL), MIX_WIDTH ** -0.5),
        'ffn_norm': gain((DEPTH, D_MODEL)),
        'ffn_w_gate': normal((n_dense, D_MODEL, D_FF), D_MODEL ** -0.5),
        'ffn_w_up': normal((n_dense, D_MODEL, D_FF), D_MODEL ** -0.5),
        'ffn_w_down': normal((n_dense, D_FF, D_MODEL), D_FF ** -0.5),
        'moe_router': normal((n_moe, D_MODEL, N_EXPERTS), D_MODEL ** -0.5),
        'moe_w_gate': normal((n_moe, N_EXPERTS, D_MODEL, D_FF_EXPERT), D_MODEL ** -0.5),
        'moe_w_up': normal((n_moe, N_EXPERTS, D_MODEL, D_FF_EXPERT), D_MODEL ** -0.5),
        'moe_w_down': normal((n_moe, N_EXPERTS, D_FF_EXPERT, D_MODEL), D_FF_EXPERT ** -0.5),
    }


def reference(x, rel_bias, mix_norm, w_in, attn_q_gain, attn_k_gain, attn_lambda, attn_sub_gain,
              gdn_conv_w, gdn_A_log, gdn_dt_bias, gdn_o_gain, ssm_conv_w, ssm_conv_b, ssm_A_log,
              ssm_dt_bias, ssm_D, ssm_norm_gain, w_out, ffn_norm, ffn_w_gate, ffn_w_up, ffn_w_down,
              moe_router, moe_w_gate, moe_w_up, moe_w_down):
    B, S, _ = x.shape
    for li in range(DEPTH):
        h = rmsnorm(x, mix_norm[li])
        aq, ak, av, g_qkv, g_gate, g_beta, g_a, s_z, s_xbc, s_dt = split_in_proj(h @ w_in[li])
        attn_o = diff_attention(aq.reshape(B, S, ATTN_HEADS, 2, ATTN_QK_DIM),
                                ak.reshape(B, S, ATTN_HEADS, 2, ATTN_QK_DIM),
                                av.reshape(B, S, ATTN_HEADS, ATTN_V_DIM),
                                attn_q_gain[li], attn_k_gain[li], attn_lambda[li], attn_sub_gain[li],
                                rel_bias, li)
        gdn_o = gated_deltanet(g_qkv, g_gate, g_beta, g_a, gdn_conv_w[li], gdn_A_log[li],
                               gdn_dt_bias[li], gdn_o_gain[li])
        ssm_o = mamba2_ssd(s_z, s_xbc, s_dt, ssm_conv_w[li], ssm_conv_b[li], ssm_A_log[li],
                           ssm_dt_bias[li], ssm_D[li], ssm_norm_gain[li])
        x = x + jnp.concatenate([attn_o, gdn_o, ssm_o], axis=-1) @ w_out[li]
        h = rmsnorm(x, ffn_norm[li])
        j = li // 2
        if li % 2 == 0:
            x = x + swiglu(h, ffn_w_gate[j], ffn_w_up[j], ffn_w_down[j])
        else:
            x = x + moe_swiglu(h, moe_router[j], moe_w_gate[j], moe_w_up[j], moe_w_down[j])
    return x
```

```python
import functools
import math

import jax
import jax.numpy as jnp
from jax import lax
from jax.experimental import pallas as pl
from jax.experimental.pallas import tpu as pltpu

F32 = jnp.float32
BF16 = jnp.bfloat16

NORM_EPS = 1e-6
CONV_K = 4
LANES = 128
HALO = 8

ATTN_HEADS = 4
ATTN_QK_DIM = 64
ATTN_V_DIM = 128
ATTN_WIDTH = 512
REL_BUCKETS = 32
REL_MAX_DIST = 128

GDN_HEADS = 6
GDN_DIM = 128
GDN_WIDTH = 768

SSM_HEADS = 12
SSM_HEAD_DIM = 64
SSM_GROUPS = 2
SSM_STATE = 128
SSM_WIDTH = 768
SSM_XBC = 1280

N_EXPERTS = 8
TOP_K = 2

GDN_COLS = 4 * GDN_WIDTH
SZ_BLOCK = 4
XBC_BLOCK = 3
ATTN_BLOCK0 = (5 * GDN_WIDTH + SSM_XBC) // LANES
N_MAIN = 5 * GDN_WIDTH + SSM_XBC + 3 * ATTN_WIDTH
BETA_LANE0, GA_LANE0, DT_LANE0 = 0, GDN_HEADS, 2 * GDN_HEADS

VMEM_LIMIT = 56 * 1024 * 1024


def _cparams(sem):
    return pltpu.CompilerParams(dimension_semantics=sem, vmem_limit_bytes=VMEM_LIMIT)


def _dot(a, b):
    return jnp.dot(a.astype(BF16), b.astype(BF16), preferred_element_type=F32)


def _dot_nt(a, b):
    return lax.dot_general(a.astype(BF16), b.astype(BF16), (((1,), (1,)), ((), ())),
                           preferred_element_type=F32)


def _dot_tn(a, b):
    return lax.dot_general(a.astype(BF16), b.astype(BF16), (((0,), (0,)), ((), ())),
                           preferred_element_type=F32)


def _dot_f32(a, b):
    return jnp.dot(a, b, preferred_element_type=F32, precision=lax.Precision.HIGHEST)


def _silu(x):
    return x / (1.0 + jnp.exp(-x))


def _sigmoid(x):
    return 1.0 / (1.0 + jnp.exp(-x))


def _softplus(x):
    return jnp.maximum(x, 0.0) + jnp.log(1.0 + jnp.exp(-jnp.abs(x)))


def _rms(x):
    return x * lax.rsqrt(jnp.mean(x * x, axis=-1, keepdims=True) + NORM_EPS)


def _tril(n, strict=False):
    r = lax.broadcasted_iota(jnp.int32, (n, n), 0)
    c = lax.broadcasted_iota(jnp.int32, (n, n), 1)
    return (r > c) if strict else (r >= c)


def _norm_inproj_kernel(x_ref, g_ref, w_ref, ws_ref, o_ref, os_ref, h_ref):
    @pl.when(pl.program_id(1) == 0)
    def _():
        h = (_rms(x_ref[...]) * g_ref[...]).astype(BF16)
        h_ref[...] = h
        os_ref[...] = jnp.dot(h, ws_ref[...], preferred_element_type=F32)

    o_ref[...] = jnp.dot(h_ref[...], w_ref[...], preferred_element_type=F32).astype(o_ref.dtype)


def norm_inproj(x, gain, w_main, w_small, *, tm=1024, tn=512):
    t, d = x.shape
    n = w_main.shape[1]
    return pl.pallas_call(
        _norm_inproj_kernel,
        grid=(t // tm, n // tn),
        in_specs=[
            pl.BlockSpec((tm, d), lambda i, j: (i, 0)),
            pl.BlockSpec((1, d), lambda i, j: (0, 0)),
            pl.BlockSpec((d, tn), lambda i, j: (0, j)),
            pl.BlockSpec((d, LANES), lambda i, j: (0, 0)),
        ],
        out_specs=[
            pl.BlockSpec((tm, tn), lambda i, j: (i, j)),
            pl.BlockSpec((tm, LANES), lambda i, j: (i, 0)),
        ],
        out_shape=[jax.ShapeDtypeStruct((t, n), BF16), jax.ShapeDtypeStruct((t, LANES), F32)],
        scratch_shapes=[pltpu.VMEM((tm, d), BF16)],
        compiler_params=_cparams(("parallel", "arbitrary")),
        name="norm_inproj",
    )(x, gain.reshape(1, d), w_main, w_small)


def _pair_rms(x, gain):
    lo = lax.broadcasted_iota(jnp.int32, (1, LANES), 1) < ATTN_QK_DIM
    sq = x * x
    s_lo = jnp.sum(jnp.where(lo, sq, 0.0), axis=-1, keepdims=True)
    s_hi = jnp.sum(jnp.where(lo, 0.0, sq), axis=-1, keepdims=True)
    ms = jnp.where(lo, s_lo, s_hi) * (1.0 / ATTN_QK_DIM)
    return x * lax.rsqrt(ms + NORM_EPS) * gain


def _attn_kernel(lam_ref, qg_ref, kg_ref, sg_ref, bias_ref, q_ref, k_ref, v_ref, o_ref,
                 kn_ref, m_ref, l_ref, acc_ref, *, tq, lam_init, norm_rows):
    qi = pl.program_id(2)
    seq = k_ref.shape[0]
    lo = lax.broadcasted_iota(jnp.int32, (1, LANES), 1) < ATTN_QK_DIM

    @pl.when(qi == 0)
    def _():
        def body(c, carry):
            rows = pl.ds(pl.multiple_of(c * norm_rows, norm_rows), norm_rows)
            kn_ref[rows, :] = _pair_rms(k_ref[rows, :].astype(F32), kg_ref[...]).astype(BF16)
            return carry
        lax.fori_loop(0, seq // norm_rows, body, 0)

    q = _pair_rms(q_ref[...].astype(F32), qg_ref[...]) * (ATTN_QK_DIM ** -0.5)
    qz = (jnp.where(lo, q, 0.0).astype(BF16), jnp.where(lo, 0.0, q).astype(BF16))

    def block(kb, bias, first):
        rows = pl.ds(pl.multiple_of(kb * tq, tq), tq)
        k_blk = kn_ref[rows, :]
        v_blk = v_ref[rows, :]
        for mp in range(2):
            s = lax.dot_general(qz[mp], k_blk, (((1,), (1,)), ((), ())), preferred_element_type=F32)
            if bias is not None:
                s = s + bias
            s_max = jnp.max(s, axis=-1, keepdims=True)
            if first:
                m_new = s_max
            else:
                m_prev = m_ref[mp]
                m_new = jnp.maximum(m_prev, s_max)
                alpha = jnp.exp(m_prev - m_new)
            p = jnp.exp(s - m_new)
            p_sum = jnp.sum(p, axis=-1, keepdims=True)
            pv = jnp.dot(p.astype(BF16), v_blk, preferred_element_type=F32)
            if first:
                l_ref[mp] = p_sum
                acc_ref[mp] = pv
            else:
                l_ref[mp] = alpha * l_ref[mp] + p_sum
                acc_ref[mp] = alpha * acc_ref[mp] + pv
            m_ref[mp] = m_new

    block(qi, bias_ref[0], True)

    @pl.when(qi >= 1)
    def _():
        block(qi - 1, bias_ref[1], False)

    def far(kb, carry):
        block(kb, None, False)
        return carry
    lax.fori_loop(0, qi - 1, far, 0)

    lam = lam_ref[...]
    lam_full = (jnp.exp(jnp.sum(lam[0:1] * lam[1:2], axis=-1, keepdims=True))
                - jnp.exp(jnp.sum(lam[2:3] * lam[3:4], axis=-1, keepdims=True)) + lam_init)
    o = acc_ref[0] / l_ref[0] - lam_full * (acc_ref[1] / l_ref[1])
    o_ref[...] = (_rms(o) * sg_ref[...] * (1.0 - lam_init)).astype(o_ref.dtype)


def _t5_bucket(n):
    max_exact = REL_BUCKETS // 2
    nf = jnp.maximum(n, max_exact).astype(F32)
    large = max_exact + (jnp.log(nf / max_exact) / math.log(REL_MAX_DIST / max_exact)
                         * (REL_BUCKETS - max_exact)).astype(jnp.int32)
    return jnp.where(n < max_exact, n, jnp.minimum(large, REL_BUCKETS - 1))


def _bias_tiles(rel_bias, tq):
    assert tq >= REL_MAX_DIST
    table = rel_bias.astype(F32)
    i = jnp.arange(tq, dtype=jnp.int32)
    tiles = []
    for d in range(2):
        rel = d * tq + i[:, None] - i[None, :]
        b = jnp.take(table, _t5_bucket(jnp.maximum(rel, 0)), axis=0) - table[REL_BUCKETS - 1]
        b = jnp.where((rel >= 0)[..., None], b, -jnp.inf)
        tiles.append(jnp.transpose(b, (2, 0, 1)))
    return jnp.stack(tiles, axis=1)


def attention(proj3, q_gain, k_gain, lam, sub_gain, bias_tiles, layer_idx, *, tq=256):
    b, s, _ = proj3.shape
    h = ATTN_HEADS
    lam_init = 0.8 - 0.6 * math.exp(-0.3 * layer_idx)
    qg = jnp.tile(q_gain.astype(F32), 2).reshape(1, LANES)
    kg = jnp.tile(k_gain.astype(F32), 2).reshape(1, LANES)
    sg = sub_gain.astype(F32).reshape(1, LANES)
    kern = functools.partial(_attn_kernel, tq=tq, lam_init=lam_init, norm_rows=min(512, s))
    const = lambda shape: pl.BlockSpec(shape, lambda bi, hi, qi: (0,) * len(shape))
    return pl.pallas_call(
        kern,
        grid=(b, h, s // tq),
        in_specs=[
            const((4, ATTN_QK_DIM)), const((1, LANES)), const((1, LANES)), const((1, LANES)),
            pl.BlockSpec((None, 2, tq, tq), lambda bi, hi, qi: (hi, 0, 0, 0)),
            pl.BlockSpec((None, tq, LANES), lambda bi, hi, qi: (bi, qi, ATTN_BLOCK0 + hi)),
            pl.BlockSpec((None, s, LANES), lambda bi, hi, qi: (bi, 0, ATTN_BLOCK0 + h + hi)),
            pl.BlockSpec((None, s, LANES), lambda bi, hi, qi: (bi, 0, ATTN_BLOCK0 + 2 * h + hi)),
        ],
        out_specs=pl.BlockSpec((None, tq, LANES), lambda bi, hi, qi: (bi, qi, hi)),
        out_shape=jax.ShapeDtypeStruct((b, s, ATTN_WIDTH), BF16),
        scratch_shapes=[
            pltpu.VMEM((s, LANES), BF16),
            pltpu.VMEM((2, tq, 1), F32),
            pltpu.VMEM((2, tq, 1), F32),
            pltpu.VMEM((2, tq, LANES), F32),
        ],
        compiler_params=_cparams(("parallel", "parallel", "arbitrary")),
        name="diff_attention",
    )(lam.astype(F32), qg, kg, sg, bias_tiles, proj3, proj3, proj3)


def _causal_conv(x, w_ref, buf_ref, first):
    rows = x.shape[0]

    @pl.when(first)
    def _():
        buf_ref[0:HALO, :] = jnp.zeros((HALO, x.shape[1]), F32)

    buf_ref[HALO:HALO + rows, :] = x
    y = w_ref[CONV_K - 1:CONV_K, :] * x
    for j in range(CONV_K - 1):
        off = HALO - (CONV_K - 1) + j
        y = y + w_ref[j:j + 1, :] * buf_ref[off:off + rows, :]
    buf_ref[0:HALO, :] = x[rows - HALO:, :]
    return y


def _unit_lower_inverse(a):
    n = a.shape[0]
    eye = (lax.broadcasted_iota(jnp.int32, (n, n), 0)
           == lax.broadcasted_iota(jnp.int32, (n, n), 1)).astype(F32)
    p = eye - a
    b = _dot(a, a)
    steps = int(math.log2(n)) - 1
    for i in range(steps):
        if i + 1 < steps:
            both = _dot(jnp.concatenate([p, b], axis=0), b)
            p = p + both[:n]
            b = both[n:]
        else:
            p = p + _dot(p, b)
    return p


def _gdn_kernel(blk_ref, sm_ref, cw_ref, arow_ref, dtb_ref, og_ref, o_ref, buf_ref, st_ref, *, chunk):
    first = pl.program_id(1) == 0
    c = chunk
    d = GDN_DIM
    nq = 3 * GDN_WIDTH

    @pl.when(first)
    def _():
        st_ref[...] = jnp.zeros(st_ref.shape, F32)

    qkv = _silu(_causal_conv(blk_ref[:, 0:nq].astype(F32), cw_ref, buf_ref, first))
    sm = sm_ref[...]
    beta = _sigmoid(sm)
    g = arow_ref[...] * _softplus(sm + dtb_ref[...])
    strict = _tril(c, strict=True)
    causal = _tril(c)
    gc = _dot_f32(causal.astype(F32), g)
    gct = gc.T

    for h in range(GDN_HEADS):
        cols = slice(h * d, (h + 1) * d)
        q = qkv[:, h * d:(h + 1) * d]
        k = qkv[:, GDN_WIDTH + h * d:GDN_WIDTH + (h + 1) * d]
        v = qkv[:, 2 * GDN_WIDTH + h * d:2 * GDN_WIDTH + (h + 1) * d]
        q = q * lax.rsqrt(jnp.sum(q * q, axis=-1, keepdims=True) + NORM_EPS) * (d ** -0.5)
        k = k * lax.rsqrt(jnp.sum(k * k, axis=-1, keepdims=True) + NORM_EPS)
        bcol = beta[:, BETA_LANE0 + h:BETA_LANE0 + h + 1]
        col = gc[:, GA_LANE0 + h:GA_LANE0 + h + 1]
        row = gct[GA_LANE0 + h:GA_LANE0 + h + 1, :]
        last = gc[c - 1:c, GA_LANE0 + h:GA_LANE0 + h + 1]
        dec = jnp.exp(jnp.where(causal, col - row, -jnp.inf))
        kb = k * bcol
        a = _dot_nt(kb, k) * jnp.where(strict, dec, 0.0)
        qk = _dot_nt(q, k) * dec
        ecol = jnp.exp(col)
        t_inv = _unit_lower_inverse(a)
        sol = _dot(t_inv, jnp.concatenate([v * bcol, kb * ecol], axis=-1))
        u, w = sol[:, :d], sol[:, d:]
        state = st_ref[h]
        both = _dot(jnp.concatenate([w, q * ecol], axis=0), state)
        v_new = u - both[:c]
        o = both[c:] + _dot(qk, v_new)
        st_ref[h] = state * jnp.exp(last) + _dot_tn(k * jnp.exp(last - col), v_new)
        gate = blk_ref[:, nq + h * d:nq + (h + 1) * d].astype(F32)
        o_ref[:, cols] = (_rms(o) * og_ref[...] * _silu(gate)).astype(o_ref.dtype)


def gdn(proj3, small3, conv_w, a_log, dt_bias, o_gain, *, chunk=64):
    b, s, _ = proj3.shape
    pad = lambda v, lane0: jnp.zeros((1, LANES), F32).at[0, lane0:lane0 + v.shape[0]].set(v.astype(F32))
    arow = pad(-jnp.exp(a_log.astype(F32)), GA_LANE0)
    dtb = pad(dt_bias, GA_LANE0)
    const = lambda shape: pl.BlockSpec(shape, lambda bi, si: (0,) * len(shape))
    return pl.pallas_call(
        functools.partial(_gdn_kernel, chunk=chunk),
        grid=(b, s // chunk),
        in_specs=[
            pl.BlockSpec((None, chunk, GDN_COLS), lambda bi, si: (bi, si, 0)),
            pl.BlockSpec((None, chunk, LANES), lambda bi, si: (bi, si, 0)),
            const((CONV_K, 3 * GDN_WIDTH)), const((1, LANES)), const((1, LANES)), const((1, GDN_DIM)),
        ],
        out_specs=pl.BlockSpec((None, chunk, GDN_WIDTH), lambda bi, si: (bi, si, 0)),
        out_shape=jax.ShapeDtypeStruct((b, s, GDN_WIDTH), BF16),
        scratch_shapes=[
            pltpu.VMEM((HALO + chunk, 3 * GDN_WIDTH), F32),
            pltpu.VMEM((GDN_HEADS, GDN_DIM, GDN_DIM), F32),
        ],
        compiler_params=_cparams(("parallel", "arbitrary")),
        name="gated_deltanet",
    )(proj3, small3, conv_w.astype(F32), arow, dtb, o_gain.astype(F32).reshape(1, GDN_DIM))


def _ssd_kernel(xbc_ref, z_ref, sm_ref, cw_ref, cb_ref, arow_ref, dtb_ref, dsk_ref, ng_ref, o_ref,
                buf_ref, st_ref, y_ref, *, chunk):
    first = pl.program_id(1) == 0
    c = chunk
    p2 = 2 * SSM_HEAD_DIM
    heads_per_group = SSM_HEADS // SSM_GROUPS
    gw = SSM_WIDTH // SSM_GROUPS

    @pl.when(first)
    def _():
        st_ref[...] = jnp.zeros(st_ref.shape, F32)

    xbc = _silu(_causal_conv(xbc_ref[...].astype(F32), cw_ref, buf_ref, first) + cb_ref[...])
    x = xbc[:, :SSM_WIDTH]
    sm = sm_ref[...]
    dt = _softplus(sm + dtb_ref[...])
    causal = _tril(c)
    acum = _dot_f32(causal.astype(F32), dt * arow_ref[...])
    acum_t = acum.T
    lo = lax.broadcasted_iota(jnp.int32, (1, p2), 1) < SSM_HEAD_DIM
    halves = lambda lane, arr: (arr[:, lane:lane + 1], arr[:, lane + 1:lane + 2])
    sel = lambda pair: jnp.where(lo, pair[0], pair[1])

    for pr in range(SSM_HEADS // 2):
        grp = (2 * pr) // heads_per_group
        lane = DT_LANE0 + 2 * pr
        bm = xbc[:, SSM_WIDTH + grp * SSM_STATE:SSM_WIDTH + (grp + 1) * SSM_STATE]
        cm = xbc[:, SSM_WIDTH + (SSM_GROUPS + grp) * SSM_STATE:SSM_WIDTH + (SSM_GROUPS + grp + 1) * SSM_STATE]
        cb = _dot_nt(cm, bm)
        xp = x[:, pr * p2:(pr + 1) * p2]
        cols = halves(lane, acum)
        lasts = halves(lane, acum[c - 1:c, :])
        xdt = xp * sel(halves(lane, dt))
        y_diag = []
        for hh in range(2):
            row = acum_t[lane + hh:lane + hh + 1, :]
            lmat = jnp.exp(jnp.where(causal, cols[hh] - row, -jnp.inf))
            y_diag.append(_dot(cb * lmat, xdt))
        prev = st_ref[pr]
        y_off = _dot(cm, prev) * sel((jnp.exp(cols[0]), jnp.exp(cols[1])))
        decay_in = sel((jnp.exp(lasts[0] - cols[0]), jnp.exp(lasts[1] - cols[1])))
        st_ref[pr] = prev * sel((jnp.exp(lasts[0]), jnp.exp(lasts[1]))) + _dot_tn(bm, xdt * decay_in)
        y_ref[:, pr * p2:(pr + 1) * p2] = jnp.where(lo, y_diag[0], y_diag[1]) + y_off

    y = (y_ref[...] + dsk_ref[...] * x) * _silu(z_ref[...].astype(F32))
    for grp in range(SSM_GROUPS):
        cols = slice(grp * gw, (grp + 1) * gw)
        o_ref[:, cols] = (_rms(y[:, cols]) * ng_ref[:, cols]).astype(o_ref.dtype)


def ssd(proj3, small3, conv_w, conv_b, a_log, dt_bias, d_skip, norm_gain, *, chunk=128):
    b, s, _ = proj3.shape
    pad = lambda v: jnp.zeros((1, LANES), F32).at[0, DT_LANE0:DT_LANE0 + SSM_HEADS].set(v.astype(F32))
    arow = pad(-jnp.exp(a_log.astype(F32)))
    dtb = pad(dt_bias)
    dsk = jnp.repeat(d_skip.astype(F32), SSM_HEAD_DIM).reshape(1, SSM_WIDTH)
    const = lambda shape: pl.BlockSpec(shape, lambda bi, si: (0,) * len(shape))
    return pl.pallas_call(
        functools.partial(_ssd_kernel, chunk=chunk),
        grid=(b, s // chunk),
        in_specs=[
            pl.BlockSpec((None, chunk, SSM_XBC), lambda bi, si: (bi, si, XBC_BLOCK)),
            pl.BlockSpec((None, chunk, SSM_WIDTH), lambda bi, si: (bi, si, SZ_BLOCK)),
            pl.BlockSpec((None, chunk, LANES), lambda bi, si: (bi, si, 0)),
            const((CONV_K, SSM_XBC)), const((1, SSM_XBC)), const((1, LANES)), const((1, LANES)),
            const((1, SSM_WIDTH)), const((1, SSM_WIDTH)),
        ],
        out_specs=pl.BlockSpec((None, chunk, SSM_WIDTH), lambda bi, si: (bi, si, 0)),
        out_shape=jax.ShapeDtypeStruct((b, s, SSM_WIDTH), BF16),
        scratch_shapes=[
            pltpu.VMEM((HALO + chunk, SSM_XBC), F32),
            pltpu.VMEM((SSM_HEADS // 2, SSM_STATE, 2 * SSM_HEAD_DIM), F32),
            pltpu.VMEM((chunk, SSM_WIDTH), F32),
        ],
        compiler_params=_cparams(("parallel", "arbitrary")),
        name="mamba2_ssd",
    )(proj3, proj3, small3, conv_w.astype(F32), conv_b.astype(F32).reshape(1, SSM_XBC), arow, dtb, dsk,
      norm_gain.astype(F32).reshape(1, SSM_WIDTH))


def _outproj_kernel(*refs, with_router):
    if with_router:
        (a_ref, g_ref, s_ref, wa_ref, wg_ref, ws_ref, x_ref, ng_ref, wr_ref,
         xo_ref, h_ref, ri_ref, rw_ref) = refs
    else:
        a_ref, g_ref, s_ref, wa_ref, wg_ref, ws_ref, x_ref, ng_ref, xo_ref, h_ref = refs
    y = (jnp.dot(a_ref[...], wa_ref[...], preferred_element_type=F32)
         + jnp.dot(g_ref[...], wg_ref[...], preferred_element_type=F32)
         + jnp.dot(s_ref[...], ws_ref[...], preferred_element_type=F32))
    xn = x_ref[...] + y
    xo_ref[...] = xn
    h = (_rms(xn) * ng_ref[...]).astype(BF16)
    h_ref[...] = h
    if with_router:
        lane = lax.broadcasted_iota(jnp.int32, (1, LANES), 1)
        logits = jnp.where(lane < N_EXPERTS, jnp.dot(h, wr_ref[...], preferred_element_type=F32), -jnp.inf)
        v1 = jnp.max(logits, axis=-1, keepdims=True)
        i1 = jnp.min(jnp.where(logits == v1, lane, LANES), axis=-1, keepdims=True)
        rest = jnp.where(lane == i1, -jnp.inf, logits)
        v2 = jnp.max(rest, axis=-1, keepdims=True)
        i2 = jnp.min(jnp.where(rest == v2, lane, LANES), axis=-1, keepdims=True)
        e2 = jnp.exp(v2 - v1)
        ri_ref[...] = jnp.where(lane == 0, i1, i2)
        rw_ref[...] = jnp.where(lane == 0, 1.0 / (1.0 + e2), e2 / (1.0 + e2))


def outproj(attn_o, gdn_o, ssm_o, w_a, w_g, w_s, x, norm_gain, w_router=None, *, tm=512):
    t, d = x.shape
    with_router = w_router is not None
    row = lambda width: pl.BlockSpec((tm, width), lambda i: (i, 0))
    const = lambda shape: pl.BlockSpec(shape, lambda i: (0, 0))
    in_specs = [row(ATTN_WIDTH), row(GDN_WIDTH), row(SSM_WIDTH),
                const((ATTN_WIDTH, d)), const((GDN_WIDTH, d)), const((SSM_WIDTH, d)),
                row(d), const((1, d))]
    args = [attn_o, gdn_o, ssm_o, w_a, w_g, w_s, x, norm_gain.astype(F32).reshape(1, d)]
    out_specs = [row(d), row(d)]
    out_shape = [jax.ShapeDtypeStruct((t, d), F32), jax.ShapeDtypeStruct((t, d), BF16)]
    if with_router:
        in_specs.append(const((d, LANES)))
        args.append(w_router)
        out_specs += [row(LANES), row(LANES)]
        out_shape += [jax.ShapeDtypeStruct((t, LANES), jnp.int32), jax.ShapeDtypeStruct((t, LANES), F32)]
    return pl.pallas_call(
        functools.partial(_outproj_kernel, with_router=with_router),
        grid=(t // tm,),
        in_specs=in_specs, out_specs=out_specs, out_shape=out_shape,
        compiler_params=_cparams(("parallel",)),
        name="outproj_router" if with_router else "outproj",
    )(*args)


def _ffn_kernel(h_ref, x_ref, wg_ref, wu_ref, wd_ref, o_ref):
    @pl.when(pl.program_id(1) == 0)
    def _():
        o_ref[...] = x_ref[...]

    h = h_ref[...]
    g = jnp.dot(h, wg_ref[...], preferred_element_type=F32)
    u = jnp.dot(h, wu_ref[...], preferred_element_type=F32)
    o_ref[...] += jnp.dot((_silu(g) * u).astype(BF16), wd_ref[...], preferred_element_type=F32)


def ffn(h, x, wg, wu, wd, *, tm=512, tf=512):
    t, d = x.shape
    f = wg.shape[1]
    return pl.pallas_call(
        _ffn_kernel,
        grid=(t // tm, f // tf),
        in_specs=[
            pl.BlockSpec((tm, d), lambda i, j: (i, 0)),
            pl.BlockSpec((tm, d), lambda i, j: (i, 0)),
            pl.BlockSpec((d, tf), lambda i, j: (0, j)),
            pl.BlockSpec((d, tf), lambda i, j: (0, j)),
            pl.BlockSpec((tf, d), lambda i, j: (j, 0)),
        ],
        out_specs=pl.BlockSpec((tm, d), lambda i, j: (i, 0)),
        out_shape=jax.ShapeDtypeStruct((t, d), F32),
        compiler_params=_cparams(("parallel", "arbitrary")),
        name="ffn_swiglu",
    )(h, x, wg, wu, wd)


def _moe_ffn_kernel(te_ref, nt_ref, h_ref, rw_ref, wg_ref, wu_ref, wd_ref, o_ref, acc_ref):
    i = pl.program_id(0)
    j = pl.program_id(1)
    last = pl.num_programs(1) - 1

    @pl.when(i < nt_ref[0])
    def _():
        h = h_ref[...]
        g = jnp.dot(h, wg_ref[...], preferred_element_type=F32)
        u = jnp.dot(h, wu_ref[...], preferred_element_type=F32)
        y = jnp.dot((_silu(g) * u).astype(BF16), wd_ref[...], preferred_element_type=F32)

        @pl.when(j == 0)
        def _():
            acc_ref[...] = y

        @pl.when(j > 0)
        def _():
            acc_ref[...] += y

        @pl.when(j == last)
        def _():
            o_ref[...] = acc_ref[...] * rw_ref[...]

    @pl.when(jnp.logical_and(i >= nt_ref[0], j == last))
    def _():
        o_ref[...] = jnp.zeros(o_ref.shape, F32)


def moe_ffn(hs, row_w, tile_expert, n_tiles, wg, wu, wd, *, tm, tf=512):
    npad, d = hs.shape
    f = wg.shape[2]
    nj = f // tf
    col = lambda i, j, nt: jnp.where(i < nt[0], j, nj - 1)
    grid_spec = pltpu.PrefetchScalarGridSpec(
        num_scalar_prefetch=2,
        grid=(npad // tm, f // tf),
        in_specs=[
            pl.BlockSpec((tm, d), lambda i, j, te, nt: (i, 0)),
            pl.BlockSpec((tm, 1), lambda i, j, te, nt: (i, 0)),
            pl.BlockSpec((None, d, tf), lambda i, j, te, nt: (te[i], 0, col(i, j, nt))),
            pl.BlockSpec((None, d, tf), lambda i, j, te, nt: (te[i], 0, col(i, j, nt))),
            pl.BlockSpec((None, tf, d), lambda i, j, te, nt: (te[i], col(i, j, nt), 0)),
        ],
        out_specs=pl.BlockSpec((tm, d), lambda i, j, te, nt: (i, 0)),
        scratch_shapes=[pltpu.VMEM((tm, d), F32)],
    )
    return pl.pallas_call(
        _moe_ffn_kernel,
        grid_spec=grid_spec,
        out_shape=jax.ShapeDtypeStruct((npad, d), F32),
        compiler_params=_cparams(("parallel", "arbitrary")),
        name="moe_grouped_swiglu",
    )(tile_expert, n_tiles, hs, row_w, wg, wu, wd)


def _gather_kernel(idx_ref, src_ref, o_ref, sem, *, tg):
    base = pl.program_id(0) * tg

    def copy(r):
        return pltpu.make_async_copy(src_ref.at[pl.ds(idx_ref[base + r], 1), :],
                                     o_ref.at[pl.ds(r, 1), :], sem)

    def start(r, carry):
        copy(r).start()
        return carry
    lax.fori_loop(0, tg, start, 0)

    def wait(r, carry):
        copy(r).wait()
        return carry
    lax.fori_loop(0, tg, wait, 0)


def gather_rows(src, idx, *, tg=256):
    n = idx.shape[0]
    width = src.shape[1]
    grid_spec = pltpu.PrefetchScalarGridSpec(
        num_scalar_prefetch=1,
        grid=(n // tg,),
        in_specs=[pl.BlockSpec(memory_space=pl.ANY)],
        out_specs=pl.BlockSpec((tg, width), lambda i, idx: (i, 0)),
        scratch_shapes=[pltpu.SemaphoreType.DMA(())],
    )
    return pl.pallas_call(
        functools.partial(_gather_kernel, tg=tg),
        grid_spec=grid_spec,
        out_shape=jax.ShapeDtypeStruct((n, width), src.dtype),
        compiler_params=_cparams(("arbitrary",)),
        name="dispatch_gather",
    )(idx, src)


def _combine_kernel(pos_ref, x_ref, ys_ref, o_ref, buf_ref, sem, *, tc, n_tok):
    base = pl.program_id(0) * tc

    def copy(k, r):
        return pltpu.make_async_copy(ys_ref.at[pl.ds(pos_ref[k * n_tok + base + r], 1), :],
                                     buf_ref.at[k, pl.ds(r, 1), :], sem)

    def start(r, carry):
        copy(0, r).start()
        copy(1, r).start()
        return carry
    lax.fori_loop(0, tc, start, 0)

    def wait(r, carry):
        copy(0, r).wait()
        copy(1, r).wait()
        return carry
    lax.fori_loop(0, tc, wait, 0)
    o_ref[...] = x_ref[...] + buf_ref[0] + buf_ref[1]


def combine_rows(x, ys, pos, *, tc=256):
    t, d = x.shape
    grid_spec = pltpu.PrefetchScalarGridSpec(
        num_scalar_prefetch=1,
        grid=(t // tc,),
        in_specs=[pl.BlockSpec((tc, d), lambda i, pos: (i, 0)), pl.BlockSpec(memory_space=pl.ANY)],
        out_specs=pl.BlockSpec((tc, d), lambda i, pos: (i, 0)),
        scratch_shapes=[pltpu.VMEM((TOP_K, tc, d), F32), pltpu.SemaphoreType.DMA(())],
    )
    return pl.pallas_call(
        functools.partial(_combine_kernel, tc=tc, n_tok=t),
        grid_spec=grid_spec,
        out_shape=jax.ShapeDtypeStruct((t, d), F32),
        compiler_params=_cparams(("arbitrary",)),
        name="combine_gather",
    )(pos, x, ys)


def _routing_plan(idx, wts, tm):
    t = idx.shape[0]
    e = idx.reshape(-1)
    onehot = (e[:, None] == jnp.arange(N_EXPERTS, dtype=jnp.int32)[None, :]).astype(jnp.int32)
    csum = jnp.cumsum(onehot, axis=0)
    rank = jnp.sum((csum - onehot) * onehot, axis=1)
    counts = csum[-1]
    padded = ((counts + tm - 1) // tm) * tm
    ends = jnp.cumsum(padded)
    pos = (ends - padded)[e] + rank
    npad = TOP_K * t + N_EXPERTS * tm
    src = jnp.zeros((npad,), jnp.int32).at[pos].set(jnp.arange(TOP_K * t, dtype=jnp.int32) // TOP_K)
    row_w = jnp.zeros((npad,), F32).at[pos].set(wts.reshape(-1))
    n_tiles = (ends[-1] // tm).astype(jnp.int32).reshape(1)
    tile_start = jnp.minimum(jnp.arange(npad // tm, dtype=jnp.int32), n_tiles - 1) * tm
    tile_expert = jnp.searchsorted(ends, tile_start, side="right").astype(jnp.int32)
    return src, row_w.reshape(npad, 1), tile_expert, n_tiles, pos.reshape(t, TOP_K).T.reshape(-1)


def moe(h, x, idx, wts, wg, wu, wd, *, tm=512):
    t, d = x.shape
    src, row_w, tile_expert, n_tiles, pos = _routing_plan(idx, wts, tm)
    h32 = lax.bitcast_convert_type(h.reshape(t, d // 2, 2), jnp.uint32)
    hs32 = gather_rows(h32, src)
    hs = lax.bitcast_convert_type(hs32, BF16).reshape(src.shape[0], d)
    ys = moe_ffn(hs, row_w, tile_expert, n_tiles, wg, wu, wd, tm=tm)
    return combine_rows(x, ys, pos)


def _split_w_in(w):
    sizes = (ATTN_WIDTH, ATTN_WIDTH, ATTN_WIDTH, 3 * GDN_WIDTH, GDN_WIDTH, GDN_HEADS, GDN_HEADS,
             SSM_WIDTH, SSM_XBC, SSM_HEADS)
    pieces, start = [], 0
    for size in sizes:
        pieces.append(w[:, start:start + size])
        start += size
    aq, ak, av, gqkv, ggate, gbeta, ga, sz, sxbc, sdt = pieces
    main = jnp.concatenate([gqkv, ggate, sz, sxbc, aq, ak, av], axis=1).astype(BF16)
    small = jnp.concatenate([gbeta, ga, sdt], axis=1)
    small = jnp.pad(small, ((0, 0), (0, LANES - small.shape[1]))).astype(BF16)
    return main, small


def kernel(x, rel_bias, mix_norm, w_in, attn_q_gain, attn_k_gain, attn_lambda, attn_sub_gain, gdn_conv_w, gdn_A_log, gdn_dt_bias, gdn_o_gain, ssm_conv_w, ssm_conv_b, ssm_A_log, ssm_dt_bias, ssm_D, ssm_norm_gain, w_out, ffn_norm, ffn_w_gate, ffn_w_up, ffn_w_down, moe_router, moe_w_gate, moe_w_up, moe_w_down):
    b, s, d = x.shape
    t = b * s
    depth = w_in.shape[0]
    tq = min(256, s)
    bias_tiles = _bias_tiles(rel_bias, tq)
    xf = x.reshape(t, d).astype(F32)
    for li in range(depth):
        w_main, w_small = _split_w_in(w_in[li])
        proj, small = norm_inproj(xf, mix_norm[li].astype(F32), w_main, w_small, tm=min(1024, t))
        proj3 = proj.reshape(b, s, N_MAIN)
        small3 = small.reshape(b, s, LANES)
        attn_o = attention(proj3, attn_q_gain[li], attn_k_gain[li], attn_lambda[li], attn_sub_gain[li],
                           bias_tiles, li, tq=tq)
        gdn_o = gdn(proj3, small3, gdn_conv_w[li], gdn_A_log[li], gdn_dt_bias[li], gdn_o_gain[li])
        ssm_o = ssd(proj3, small3, ssm_conv_w[li], ssm_conv_b[li], ssm_A_log[li], ssm_dt_bias[li],
                    ssm_D[li], ssm_norm_gain[li])
        wo = w_out[li].astype(BF16)
        w_a, w_g, w_s = wo[:ATTN_WIDTH], wo[ATTN_WIDTH:ATTN_WIDTH + GDN_WIDTH], wo[ATTN_WIDTH + GDN_WIDTH:]
        mix = (attn_o.reshape(t, ATTN_WIDTH), gdn_o.reshape(t, GDN_WIDTH), ssm_o.reshape(t, SSM_WIDTH))
        j = li // 2
        if li % 2 == 0:
            xf, h = outproj(*mix, w_a, w_g, w_s, xf, ffn_norm[li], tm=min(512, t))
            xf = ffn(h, xf, ffn_w_gate[j].astype(BF16), ffn_w_up[j].astype(BF16), ffn_w_down[j].astype(BF16),
                     tm=min(512, t))
        else:
            w_r = jnp.pad(moe_router[j], ((0, 0), (0, LANES - N_EXPERTS))).astype(BF16)
            xf, h, ridx, rw = outproj(*mix, w_a, w_g, w_s, xf, ffn_norm[li], w_r, tm=min(512, t))
            xf = moe(h, xf, ridx[:, :TOP_K], rw[:, :TOP_K], moe_w_gate[j].astype(BF16), moe_w_up[j].astype(BF16),
                     moe_w_down[j].astype(BF16), tm=min(512, t))
    return xf.reshape(b, s, d).astype(x.dtype)
```

```python
import functools
import math

import jax
import jax.numpy as jnp
from jax import lax
from jax.experimental import pallas as pl
from jax.experimental.pallas import tpu as pltpu

F32 = jnp.float32
BF16 = jnp.bfloat16

NORM_EPS = 1e-6
CONV_K = 4
LANES = 128
HALO = 8

ATTN_HEADS = 4
ATTN_QK_DIM = 64
ATTN_V_DIM = 128
ATTN_WIDTH = 512
REL_BUCKETS = 32
REL_MAX_DIST = 128

GDN_HEADS = 6
GDN_DIM = 128
GDN_WIDTH = 768

SSM_HEADS = 12
SSM_HEAD_DIM = 64
SSM_GROUPS = 2
SSM_STATE = 128
SSM_WIDTH = 768
SSM_XBC = 1280

N_EXPERTS = 8
TOP_K = 2

GDN_COLS = 4 * GDN_WIDTH
SZ_BLOCK = 4
XBC_BLOCK = 3
ATTN_BLOCK0 = (5 * GDN_WIDTH + SSM_XBC) // LANES
N_MAIN = 5 * GDN_WIDTH + SSM_XBC + 3 * ATTN_WIDTH
BETA_LANE0, GA_LANE0, DT_LANE0 = 0, GDN_HEADS, 2 * GDN_HEADS

VMEM_LIMIT = 56 * 1024 * 1024


def _cparams(sem):
    return pltpu.CompilerParams(dimension_semantics=sem, vmem_limit_bytes=VMEM_LIMIT)


def _dot(a, b):
    return jnp.dot(a.astype(BF16), b.astype(BF16), preferred_element_type=F32)


def _dot_nt(a, b):
    return lax.dot_general(a.astype(BF16), b.astype(BF16), (((1,), (1,)), ((), ())),
                           preferred_element_type=F32)


def _dot_tn(a, b):
    return lax.dot_general(a.astype(BF16), b.astype(BF16), (((0,), (0,)), ((), ())),
                           preferred_element_type=F32)


def _dot_f32(a, b):
    return jnp.dot(a, b, preferred_element_type=F32, precision=lax.Precision.HIGHEST)


def _silu(x):
    return x / (1.0 + jnp.exp(-x))


def _sigmoid(x):
    return 1.0 / (1.0 + jnp.exp(-x))


def _softplus(x):
    return jnp.maximum(x, 0.0) + jnp.log(1.0 + jnp.exp(-jnp.abs(x)))


def _rms(x):
    return x * lax.rsqrt(jnp.mean(x * x, axis=-1, keepdims=True) + NORM_EPS)


def _tril(n, strict=False):
    r = lax.broadcasted_iota(jnp.int32, (n, n), 0)
    c = lax.broadcasted_iota(jnp.int32, (n, n), 1)
    return (r > c) if strict else (r >= c)


def _norm_inproj_kernel(x_ref, g_ref, w_ref, ws_ref, o_ref, os_ref, h_ref):
    @pl.when(pl.program_id(1) == 0)
    def _():
        h = (_rms(x_ref[...]) * g_ref[...]).astype(BF16)
        h_ref[...] = h
        os_ref[...] = jnp.dot(h, ws_ref[...], preferred_element_type=F32)

    o_ref[...] = jnp.dot(h_ref[...], w_ref[...], preferred_element_type=F32).astype(o_ref.dtype)


def norm_inproj(x, gain, w_main, w_small, *, tm=1024, tn=512):
    t, d = x.shape
    n = w_main.shape[1]
    return pl.pallas_call(
        _norm_inproj_kernel,
        grid=(t // tm, n // tn),
        in_specs=[
            pl.BlockSpec((tm, d), lambda i, j: (i, 0)),
            pl.BlockSpec((1, d), lambda i, j: (0, 0)),
            pl.BlockSpec((d, tn), lambda i, j: (0, j)),
            pl.BlockSpec((d, LANES), lambda i, j: (0, 0)),
        ],
        out_specs=[
            pl.BlockSpec((tm, tn), lambda i, j: (i, j)),
            pl.BlockSpec((tm, LANES), lambda i, j: (i, 0)),
        ],
        out_shape=[jax.ShapeDtypeStruct((t, n), BF16), jax.ShapeDtypeStruct((t, LANES), F32)],
        scratch_shapes=[pltpu.VMEM((tm, d), BF16)],
        compiler_params=_cparams(("parallel", "arbitrary")),
        name="norm_inproj",
    )(x, gain.reshape(1, d), w_main, w_small)


def _pair_rms(x, gain):
    lo = lax.broadcasted_iota(jnp.int32, (1, LANES), 1) < ATTN_QK_DIM
    sq = x * x
    s_lo = jnp.sum(jnp.where(lo, sq, 0.0), axis=-1, keepdims=True)
    s_hi = jnp.sum(jnp.where(lo, 0.0, sq), axis=-1, keepdims=True)
    ms = jnp.where(lo, s_lo, s_hi) * (1.0 / ATTN_QK_DIM)
    return x * lax.rsqrt(ms + NORM_EPS) * gain


ONES_ROWS = 16


def _attn_kernel(lam_ref, qg_ref, kg_ref, sg_ref, bias_ref, q_ref, k_ref, v_ref, o_ref,
                 kn_ref, vt_ref, m_ref, acc_ref, *, tq, lam_init, far_group):
    qi = pl.program_id(2)
    lo = lax.broadcasted_iota(jnp.int32, (1, LANES), 1) < ATTN_QK_DIM
    nv = ATTN_V_DIM

    @pl.when(qi == 0)
    def _():
        def body(c, carry):
            rows = pl.ds(pl.multiple_of(c * tq, tq), tq)
            kn_ref[rows, :] = _pair_rms(k_ref[rows, :].astype(F32), kg_ref[...]).astype(BF16)
            vt_ref[c, 0:nv, :] = v_ref[rows, :].astype(F32).T.astype(BF16)
            vt_ref[c, nv:nv + ONES_ROWS, :] = jnp.ones((ONES_ROWS, tq), BF16)
            return carry
        lax.fori_loop(0, k_ref.shape[0] // tq, body, 0)

    q = _pair_rms(q_ref[...].astype(F32), qg_ref[...]) * (ATTN_QK_DIM ** -0.5)
    qz = (jnp.where(lo, q, 0.0).astype(BF16), jnp.where(lo, 0.0, q).astype(BF16))

    def blocks(kb, n, bias, first):
        k_blk = kn_ref[pl.ds(pl.multiple_of(kb * tq, tq), n * tq), :]
        vt = vt_ref[kb] if n == 1 else jnp.concatenate([vt_ref[kb + i] for i in range(n)], axis=-1)
        maps = range(2)
        s = [lax.dot_general(k_blk, qz[mp], (((1,), (1,)), ((), ())), preferred_element_type=F32) for mp in maps]
        if bias is not None:
            s = [x + bias for x in s]
        s_max = [jnp.max(x, axis=0, keepdims=True) for x in s]
        if first:
            m_new = s_max
        else:
            m_prev = [m_ref[mp] for mp in maps]
            m_new = [jnp.maximum(m_prev[mp], s_max[mp]) for mp in maps]
            alpha = [jnp.exp(m_prev[mp] - m_new[mp]) for mp in maps]
        p = [jnp.exp(s[mp] - m_new[mp]).astype(BF16) for mp in maps]
        pv = [jnp.dot(vt, p[mp], preferred_element_type=F32) for mp in maps]
        for mp in maps:
            acc_ref[mp] = pv[mp] if first else alpha[mp] * acc_ref[mp] + pv[mp]
            m_ref[mp] = m_new[mp]

    @pl.when(qi == 0)
    def _():
        blocks(qi, 1, bias_ref[1], True)

    @pl.when(qi >= 1)
    def _():
        blocks(qi - 1, 2, bias_ref[...].reshape(2 * tq, tq), True)

    n_far = jnp.maximum(qi - 1, 0)
    n_groups = n_far // far_group

    def far(gi, carry):
        blocks(gi * far_group, far_group, None, False)
        return carry
    lax.fori_loop(0, n_groups, far, 0)
    rem = n_far - n_groups * far_group
    for r in range(1, far_group):
        @pl.when(rem == r)
        def _(r=r):
            blocks(n_groups * far_group, r, None, False)

    lam = lam_ref[...]
    lam_full = (jnp.exp(jnp.sum(lam[0:1] * lam[1:2], axis=-1, keepdims=True))
                - jnp.exp(jnp.sum(lam[2:3] * lam[3:4], axis=-1, keepdims=True)) + lam_init)
    a0, a1 = acc_ref[0], acc_ref[1]
    o = a0[0:nv] / a0[nv:nv + 1] - lam_full * (a1[0:nv] / a1[nv:nv + 1])
    o = o * lax.rsqrt(jnp.mean(o * o, axis=0, keepdims=True) + NORM_EPS)
    o_ref[...] = (o.T * sg_ref[...] * (1.0 - lam_init)).astype(o_ref.dtype)


def _t5_bucket(n):
    max_exact = REL_BUCKETS // 2
    nf = jnp.maximum(n, max_exact).astype(F32)
    large = max_exact + (jnp.log(nf / max_exact) / math.log(REL_MAX_DIST / max_exact)
                         * (REL_BUCKETS - max_exact)).astype(jnp.int32)
    return jnp.where(n < max_exact, n, jnp.minimum(large, REL_BUCKETS - 1))


def _bias_tiles(rel_bias, tq):
    assert tq >= REL_MAX_DIST
    table = rel_bias.astype(F32)
    i = jnp.arange(tq, dtype=jnp.int32)
    tiles = []
    for d in (1, 0):
        rel = d * tq + i[None, :] - i[:, None]
        b = jnp.take(table, _t5_bucket(jnp.maximum(rel, 0)), axis=0) - table[REL_BUCKETS - 1]
        b = jnp.where((rel >= 0)[..., None], b, -jnp.inf)
        tiles.append(jnp.transpose(b, (2, 0, 1)))
    return jnp.stack(tiles, axis=1)


def attention(proj3, q_gain, k_gain, lam, sub_gain, bias_tiles, layer_idx, *, tq=256, far_group=4):
    b, s, _ = proj3.shape
    h = ATTN_HEADS
    lam_init = 0.8 - 0.6 * math.exp(-0.3 * layer_idx)
    qg = jnp.tile(q_gain.astype(F32), 2).reshape(1, LANES)
    kg = jnp.tile(k_gain.astype(F32), 2).reshape(1, LANES)
    sg = sub_gain.astype(F32).reshape(1, LANES)
    kern = functools.partial(_attn_kernel, tq=tq, lam_init=lam_init, far_group=far_group)
    const = lambda shape: pl.BlockSpec(shape, lambda bi, hi, qi: (0,) * len(shape))
    return pl.pallas_call(
        kern,
        grid=(b, h, s // tq),
        in_specs=[
            const((4, ATTN_QK_DIM)), const((1, LANES)), const((1, LANES)), const((1, LANES)),
            pl.BlockSpec((None, 2, tq, tq), lambda bi, hi, qi: (hi, 0, 0, 0)),
            pl.BlockSpec((None, tq, LANES), lambda bi, hi, qi: (bi, qi, ATTN_BLOCK0 + hi)),
            pl.BlockSpec((None, s, LANES), lambda bi, hi, qi: (bi, 0, ATTN_BLOCK0 + h + hi)),
            pl.BlockSpec((None, s, LANES), lambda bi, hi, qi: (bi, 0, ATTN_BLOCK0 + 2 * h + hi)),
        ],
        out_specs=pl.BlockSpec((None, tq, LANES), lambda bi, hi, qi: (bi, qi, hi)),
        out_shape=jax.ShapeDtypeStruct((b, s, ATTN_WIDTH), BF16),
        scratch_shapes=[
            pltpu.VMEM((s, LANES), BF16),
            pltpu.VMEM((s // tq, ATTN_V_DIM + ONES_ROWS, tq), BF16),
            pltpu.VMEM((2, 1, tq), F32),
            pltpu.VMEM((2, ATTN_V_DIM + ONES_ROWS, tq), F32),
        ],
        compiler_params=_cparams(("parallel", "parallel", "arbitrary")),
        name="diff_attention",
    )(lam.astype(F32), qg, kg, sg, bias_tiles, proj3, proj3, proj3)


def _causal_conv(x, w_ref, buf_ref, first):
    rows = x.shape[0]

    @pl.when(first)
    def _():
        buf_ref[0:HALO, :] = jnp.zeros((HALO, x.shape[1]), F32)

    buf_ref[HALO:HALO + rows, :] = x
    y = w_ref[CONV_K - 1:CONV_K, :] * x
    for j in range(CONV_K - 1):
        off = HALO - (CONV_K - 1) + j
        y = y + w_ref[j:j + 1, :] * buf_ref[off:off + rows, :]
    buf_ref[0:HALO, :] = x[rows - HALO:, :]
    return y


def _unit_lower_inverses(mats):
    n = mats[0].shape[0]
    eye = (lax.broadcasted_iota(jnp.int32, (n, n), 0)
           == lax.broadcasted_iota(jnp.int32, (n, n), 1)).astype(F32)
    ps = [eye - a for a in mats]
    bs = [_dot(a, a) for a in mats]
    steps = int(math.log2(n)) - 1
    for i in range(steps):
        if i + 1 < steps:
            both = [_dot(jnp.concatenate([p, b], axis=0), b) for p, b in zip(ps, bs)]
            ps = [p + x[:n] for p, x in zip(ps, both)]
            bs = [x[n:] for x in both]
        else:
            ps = [p + _dot(p, b) for p, b in zip(ps, bs)]
    return ps


def _gdn_kernel(blk_ref, sm_ref, cw_ref, arow_ref, dtb_ref, og_ref, o_ref, buf_ref, st_ref, *, chunk):
    first = pl.program_id(1) == 0
    c = chunk
    d = GDN_DIM
    nq = 3 * GDN_WIDTH
    tile = blk_ref.shape[0]
    heads = range(GDN_HEADS)
    chunks = range(tile // c)

    @pl.when(first)
    def _():
        st_ref[...] = jnp.zeros(st_ref.shape, F32)

    qkv = _silu(_causal_conv(blk_ref[:, 0:nq].astype(F32), cw_ref, buf_ref, first))
    sm = sm_ref[...]
    beta = _sigmoid(sm)
    g = arow_ref[...] * _softplus(sm + dtb_ref[...])
    strict = _tril(c, strict=True)
    causal = _tril(c)
    r = lax.broadcasted_iota(jnp.int32, (tile, tile), 0)
    cc = lax.broadcasted_iota(jnp.int32, (tile, tile), 1)
    same_chunk_tril = jnp.logical_and(r >= cc, (r // c) == (cc // c))
    gc = _dot_f32(same_chunk_tril.astype(F32), g)
    gct = gc.T

    l2 = lambda x: x * lax.rsqrt(jnp.sum(x * x, axis=-1, keepdims=True) + NORM_EPS)
    qn = [l2(qkv[:, h * d:(h + 1) * d]) * (d ** -0.5) for h in heads]
    kn = [l2(qkv[:, GDN_WIDTH + h * d:GDN_WIDTH + (h + 1) * d]) for h in heads]

    pairs = [(ci, h) for ci in chunks for h in heads]
    prep = {}
    for ci, h in pairs:
        rows = slice(ci * c, (ci + 1) * c)
        q, k = qn[h][rows], kn[h][rows]
        v = qkv[rows, 2 * GDN_WIDTH + h * d:2 * GDN_WIDTH + (h + 1) * d]
        bcol = beta[rows, BETA_LANE0 + h:BETA_LANE0 + h + 1]
        col = gc[rows, GA_LANE0 + h:GA_LANE0 + h + 1]
        row = gct[GA_LANE0 + h:GA_LANE0 + h + 1, rows]
        last = gc[(ci + 1) * c - 1:(ci + 1) * c, GA_LANE0 + h:GA_LANE0 + h + 1]
        dec = jnp.exp(jnp.where(causal, col - row, -jnp.inf))
        ecol = jnp.exp(col)
        kb = k * bcol
        prep[ci, h] = dict(k=k, kb=kb, dec=dec, last=last,
                           lhs=jnp.concatenate([kb, q], axis=0),
                           rhs=jnp.concatenate([v * bcol, kb * ecol], axis=-1),
                           qd=q * ecol, kd=k * jnp.exp(last - col))
    for key in pairs:
        x = prep[key]
        both = _dot_nt(x["lhs"], x["k"])
        x["a"] = both[:c] * jnp.where(strict, x["dec"], 0.0)
        x["qk"] = both[c:] * x["dec"]
    t_inv = _unit_lower_inverses([prep[key]["a"] for key in pairs])
    for key, t in zip(pairs, t_inv):
        prep[key]["sol"] = _dot(t, prep[key]["rhs"])

    states = [st_ref[h] for h in heads]
    for ci in chunks:
        xs = [prep[ci, h] for h in heads]
        both = [_dot(jnp.concatenate([x["sol"][:, d:], x["qd"]], axis=0), st) for x, st in zip(xs, states)]
        v_new = [x["sol"][:, :d] - y[:c] for x, y in zip(xs, both)]
        inter = [_dot(x["qk"], vn) for x, vn in zip(xs, v_new)]
        upd = [_dot_tn(x["kd"], vn) for x, vn in zip(xs, v_new)]
        states = [st * jnp.exp(x["last"]) + u for st, x, u in zip(states, xs, upd)]
        rows = slice(ci * c, (ci + 1) * c)
        for h in heads:
            o = both[h][c:] + inter[h]
            gate = blk_ref[rows, nq + h * d:nq + (h + 1) * d].astype(F32)
            o_ref[rows, h * d:(h + 1) * d] = (_rms(o) * og_ref[...] * _silu(gate)).astype(o_ref.dtype)
    for h in heads:
        st_ref[h] = states[h]


def gdn(proj3, small3, conv_w, a_log, dt_bias, o_gain, *, chunk=64, tile=256):
    b, s, _ = proj3.shape
    tile = min(tile, s)
    pad = lambda v, lane0: jnp.zeros((1, LANES), F32).at[0, lane0:lane0 + v.shape[0]].set(v.astype(F32))
    arow = pad(-jnp.exp(a_log.astype(F32)), GA_LANE0)
    dtb = pad(dt_bias, GA_LANE0)
    const = lambda shape: pl.BlockSpec(shape, lambda bi, si: (0,) * len(shape))
    return pl.pallas_call(
        functools.partial(_gdn_kernel, chunk=chunk),
        grid=(b, s // tile),
        in_specs=[
            pl.BlockSpec((None, tile, GDN_COLS), lambda bi, si: (bi, si, 0)),
            pl.BlockSpec((None, tile, LANES), lambda bi, si: (bi, si, 0)),
            const((CONV_K, 3 * GDN_WIDTH)), const((1, LANES)), const((1, LANES)), const((1, GDN_DIM)),
        ],
        out_specs=pl.BlockSpec((None, tile, GDN_WIDTH), lambda bi, si: (bi, si, 0)),
        out_shape=jax.ShapeDtypeStruct((b, s, GDN_WIDTH), BF16),
        scratch_shapes=[
            pltpu.VMEM((HALO + tile, 3 * GDN_WIDTH), F32),
            pltpu.VMEM((GDN_HEADS, GDN_DIM, GDN_DIM), F32),
        ],
        compiler_params=_cparams(("parallel", "arbitrary")),
        name="gated_deltanet",
    )(proj3, small3, conv_w.astype(F32), arow, dtb, o_gain.astype(F32).reshape(1, GDN_DIM))


def _ssd_kernel(xbc_ref, z_ref, sm_ref, cw_ref, cb_ref, arow_ref, dtb_ref, dsk_ref, ng_ref, o_ref,
                buf_ref, st_ref, y_ref, *, chunk):
    first = pl.program_id(1) == 0
    c = chunk
    p2 = 2 * SSM_HEAD_DIM
    heads_per_group = SSM_HEADS // SSM_GROUPS
    gw = SSM_WIDTH // SSM_GROUPS

    @pl.when(first)
    def _():
        st_ref[...] = jnp.zeros(st_ref.shape, F32)

    xbc = _silu(_causal_conv(xbc_ref[...].astype(F32), cw_ref, buf_ref, first) + cb_ref[...])
    x = xbc[:, :SSM_WIDTH]
    sm = sm_ref[...]
    dt = _softplus(sm + dtb_ref[...])
    causal = _tril(c)
    acum = _dot_f32(causal.astype(F32), dt * arow_ref[...])
    acum_t = acum.T
    lo = lax.broadcasted_iota(jnp.int32, (1, p2), 1) < SSM_HEAD_DIM
    halves = lambda lane, arr: (arr[:, lane:lane + 1], arr[:, lane + 1:lane + 2])
    sel = lambda pair: jnp.where(lo, pair[0], pair[1])

    for pr in range(SSM_HEADS // 2):
        grp = (2 * pr) // heads_per_group
        lane = DT_LANE0 + 2 * pr
        bm = xbc[:, SSM_WIDTH + grp * SSM_STATE:SSM_WIDTH + (grp + 1) * SSM_STATE]
        cm = xbc[:, SSM_WIDTH + (SSM_GROUPS + grp) * SSM_STATE:SSM_WIDTH + (SSM_GROUPS + grp + 1) * SSM_STATE]
        cb = _dot_nt(cm, bm)
        xp = x[:, pr * p2:(pr + 1) * p2]
        cols = halves(lane, acum)
        lasts = halves(lane, acum[c - 1:c, :])
        xdt = xp * sel(halves(lane, dt))
        y_diag = []
        for hh in range(2):
            row = acum_t[lane + hh:lane + hh + 1, :]
            lmat = jnp.exp(jnp.where(causal, cols[hh] - row, -jnp.inf))
            y_diag.append(_dot(cb * lmat, xdt))
        prev = st_ref[pr]
        y_off = _dot(cm, prev) * sel((jnp.exp(cols[0]), jnp.exp(cols[1])))
        decay_in = sel((jnp.exp(lasts[0] - cols[0]), jnp.exp(lasts[1] - cols[1])))
        st_ref[pr] = prev * sel((jnp.exp(lasts[0]), jnp.exp(lasts[1]))) + _dot_tn(bm, xdt * decay_in)
        y_ref[:, pr * p2:(pr + 1) * p2] = jnp.where(lo, y_diag[0], y_diag[1]) + y_off

    y = (y_ref[...] + dsk_ref[...] * x) * _silu(z_ref[...].astype(F32))
    for grp in range(SSM_GROUPS):
        cols = slice(grp * gw, (grp + 1) * gw)
        o_ref[:, cols] = (_rms(y[:, cols]) * ng_ref[:, cols]).astype(o_ref.dtype)


def ssd(proj3, small3, conv_w, conv_b, a_log, dt_bias, d_skip, norm_gain, *, chunk=128):
    b, s, _ = proj3.shape
    pad = lambda v: jnp.zeros((1, LANES), F32).at[0, DT_LANE0:DT_LANE0 + SSM_HEADS].set(v.astype(F32))
    arow = pad(-jnp.exp(a_log.astype(F32)))
    dtb = pad(dt_bias)
    dsk = jnp.repeat(d_skip.astype(F32), SSM_HEAD_DIM).reshape(1, SSM_WIDTH)
    const = lambda shape: pl.BlockSpec(shape, lambda bi, si: (0,) * len(shape))
    return pl.pallas_call(
        functools.partial(_ssd_kernel, chunk=chunk),
        grid=(b, s // chunk),
        in_specs=[
            pl.BlockSpec((None, chunk, SSM_XBC), lambda bi, si: (bi, si, XBC_BLOCK)),
            pl.BlockSpec((None, chunk, SSM_WIDTH), lambda bi, si: (bi, si, SZ_BLOCK)),
            pl.BlockSpec((None, chunk, LANES), lambda bi, si: (bi, si, 0)),
            const((CONV_K, SSM_XBC)), const((1, SSM_XBC)), const((1, LANES)), const((1, LANES)),
            const((1, SSM_WIDTH)), const((1, SSM_WIDTH)),
        ],
        out_specs=pl.BlockSpec((None, chunk, SSM_WIDTH), lambda bi, si: (bi, si, 0)),
        out_shape=jax.ShapeDtypeStruct((b, s, SSM_WIDTH), BF16),
        scratch_shapes=[
            pltpu.VMEM((HALO + chunk, SSM_XBC), F32),
            pltpu.VMEM((SSM_HEADS // 2, SSM_STATE, 2 * SSM_HEAD_DIM), F32),
            pltpu.VMEM((chunk, SSM_WIDTH), F32),
        ],
        compiler_params=_cparams(("parallel", "arbitrary")),
        name="mamba2_ssd",
    )(proj3, proj3, small3, conv_w.astype(F32), conv_b.astype(F32).reshape(1, SSM_XBC), arow, dtb, dsk,
      norm_gain.astype(F32).reshape(1, SSM_WIDTH))


def _outproj_kernel(*refs, with_router):
    if with_router:
        (a_ref, g_ref, s_ref, wa_ref, wg_ref, ws_ref, x_ref, ng_ref, wr_ref,
         xo_ref, h_ref, ri_ref, rw_ref, rk_ref, cnt_ref, run_ref) = refs
    else:
        a_ref, g_ref, s_ref, wa_ref, wg_ref, ws_ref, x_ref, ng_ref, xo_ref, h_ref = refs
    y = (jnp.dot(a_ref[...], wa_ref[...], preferred_element_type=F32)
         + jnp.dot(g_ref[...], wg_ref[...], preferred_element_type=F32)
         + jnp.dot(s_ref[...], ws_ref[...], preferred_element_type=F32))
    xn = x_ref[...] + y
    xo_ref[...] = xn
    hf = _rms(xn) * ng_ref[...]
    h = hf.astype(BF16)
    h_ref[...] = hf.astype(h_ref.dtype)
    if with_router:
        tm = xn.shape[0]
        lane = lax.broadcasted_iota(jnp.int32, (1, LANES), 1)
        logits = jnp.where(lane < N_EXPERTS, jnp.dot(h, wr_ref[...], preferred_element_type=F32), -jnp.inf)
        v1 = jnp.max(logits, axis=-1, keepdims=True)
        i1 = jnp.min(jnp.where(logits == v1, lane, LANES), axis=-1, keepdims=True)
        rest = jnp.where(lane == i1, -jnp.inf, logits)
        v2 = jnp.max(rest, axis=-1, keepdims=True)
        i2 = jnp.min(jnp.where(rest == v2, lane, LANES), axis=-1, keepdims=True)
        e2 = jnp.exp(v2 - v1)
        ri_ref[...] = jnp.where(lane == 0, i1, i2)
        rw_ref[...] = jnp.where(lane == 0, 1.0 / (1.0 + e2), e2 / (1.0 + e2))

        @pl.when(pl.program_id(0) == 0)
        def _():
            run_ref[...] = jnp.zeros(run_ref.shape, F32)

        before = _tril(tm, strict=True).astype(BF16)
        run = run_ref[0:1, :]
        ranks = []
        for idx in (i1, i2):
            hit = lane == idx
            onehot = hit.astype(F32)
            earlier = jnp.dot(before, onehot.astype(BF16), preferred_element_type=F32) + run
            ranks.append(jnp.sum(jnp.where(hit, earlier, 0.0), axis=-1, keepdims=True))
            run = run + jnp.sum(onehot, axis=0, keepdims=True)
        rk_ref[...] = jnp.where(lane == 0, ranks[0], ranks[1]).astype(jnp.int32)
        run_ref[...] = jnp.broadcast_to(run, run_ref.shape)
        cnt_ref[...] = jnp.broadcast_to(run, cnt_ref.shape).astype(jnp.int32)


def outproj(attn_o, gdn_o, ssm_o, w_a, w_g, w_s, x, norm_gain, w_router=None, *, tm=512):
    t, d = x.shape
    with_router = w_router is not None
    row = lambda width: pl.BlockSpec((tm, width), lambda i: (i, 0))
    const = lambda shape: pl.BlockSpec(shape, lambda i: (0, 0))
    in_specs = [row(ATTN_WIDTH), row(GDN_WIDTH), row(SSM_WIDTH),
                const((ATTN_WIDTH, d)), const((GDN_WIDTH, d)), const((SSM_WIDTH, d)),
                row(d), const((1, d))]
    args = [attn_o, gdn_o, ssm_o, w_a, w_g, w_s, x, norm_gain.astype(F32).reshape(1, d)]
    out_specs = [row(d), row(d)]
    out_shape = [jax.ShapeDtypeStruct((t, d), F32), jax.ShapeDtypeStruct((t, d), F32 if with_router else BF16)]
    scratch = []
    if with_router:
        in_specs.append(const((d, LANES)))
        args.append(w_router)
        out_specs += [row(LANES), row(LANES), row(LANES), const((8, LANES))]
        out_shape += [jax.ShapeDtypeStruct((t, LANES), jnp.int32), jax.ShapeDtypeStruct((t, LANES), F32),
                      jax.ShapeDtypeStruct((t, LANES), jnp.int32), jax.ShapeDtypeStruct((8, LANES), jnp.int32)]
        scratch = [pltpu.VMEM((8, LANES), F32)]
    return pl.pallas_call(
        functools.partial(_outproj_kernel, with_router=with_router),
        grid=(t // tm,),
        in_specs=in_specs, out_specs=out_specs, out_shape=out_shape, scratch_shapes=scratch,
        compiler_params=_cparams(("arbitrary" if with_router else "parallel",)),
        name="outproj_router" if with_router else "outproj",
    )(*args)


def _ffn_kernel(h_ref, x_ref, wg_ref, wu_ref, wd_ref, o_ref):
    @pl.when(pl.program_id(1) == 0)
    def _():
        o_ref[...] = x_ref[...]

    h = h_ref[...]
    g = jnp.dot(h, wg_ref[...], preferred_element_type=F32)
    u = jnp.dot(h, wu_ref[...], preferred_element_type=F32)
    o_ref[...] += jnp.dot((_silu(g) * u).astype(BF16), wd_ref[...], preferred_element_type=F32)


def ffn(h, x, wg, wu, wd, *, tm=512, tf=512):
    t, d = x.shape
    f = wg.shape[1]
    return pl.pallas_call(
        _ffn_kernel,
        grid=(t // tm, f // tf),
        in_specs=[
            pl.BlockSpec((tm, d), lambda i, j: (i, 0)),
            pl.BlockSpec((tm, d), lambda i, j: (i, 0)),
            pl.BlockSpec((d, tf), lambda i, j: (0, j)),
            pl.BlockSpec((d, tf), lambda i, j: (0, j)),
            pl.BlockSpec((tf, d), lambda i, j: (j, 0)),
        ],
        out_specs=pl.BlockSpec((tm, d), lambda i, j: (i, 0)),
        out_shape=jax.ShapeDtypeStruct((t, d), F32),
        compiler_params=_cparams(("parallel", "arbitrary")),
        name="ffn_swiglu",
    )(h, x, wg, wu, wd)


def _moe_ffn_kernel(te_ref, nt_ref, h_ref, wg_ref, wu_ref, wd_ref, o_ref, hb_ref):
    i = pl.program_id(0)
    j = pl.program_id(1)
    active = i < nt_ref[0]

    @pl.when(jnp.logical_and(active, j == 0))
    def _():
        hb_ref[...] = h_ref[...].astype(BF16)

    @pl.when(active)
    def _():
        h = hb_ref[...]
        g = jnp.dot(h, wg_ref[...], preferred_element_type=F32)
        u = jnp.dot(h, wu_ref[...], preferred_element_type=F32)
        y = jnp.dot((_silu(g) * u).astype(BF16), wd_ref[...], preferred_element_type=F32)

        @pl.when(j == 0)
        def _():
            o_ref[...] = y

        @pl.when(j > 0)
        def _():
            o_ref[...] += y

    @pl.when(jnp.logical_and(jnp.logical_not(active), j == 0))
    def _():
        o_ref[...] = jnp.zeros(o_ref.shape, F32)


def moe_ffn(hs, tile_expert, n_tiles, wg, wu, wd, *, tm, tf=512):
    npad, d = hs.shape
    f = wg.shape[2]
    nj = f // tf
    col = lambda i, j, nt: jnp.where(i < nt[0], j, nj - 1)
    grid_spec = pltpu.PrefetchScalarGridSpec(
        num_scalar_prefetch=2,
        grid=(npad // tm, f // tf),
        in_specs=[
            pl.BlockSpec((tm, d), lambda i, j, te, nt: (jnp.minimum(i, nt[0] - 1), 0)),
            pl.BlockSpec((None, d, tf), lambda i, j, te, nt: (te[i], 0, col(i, j, nt))),
            pl.BlockSpec((None, d, tf), lambda i, j, te, nt: (te[i], 0, col(i, j, nt))),
            pl.BlockSpec((None, tf, d), lambda i, j, te, nt: (te[i], col(i, j, nt), 0)),
        ],
        out_specs=pl.BlockSpec((tm, d), lambda i, j, te, nt: (i, 0)),
        scratch_shapes=[pltpu.VMEM((tm, d), BF16)],
    )
    return pl.pallas_call(
        _moe_ffn_kernel,
        grid_spec=grid_spec,
        out_shape=jax.ShapeDtypeStruct((npad, d), F32),
        compiler_params=_cparams(("parallel", "arbitrary")),
        name="moe_grouped_swiglu",
    )(tile_expert, n_tiles, hs, wg, wu, wd)


DMA_UNROLL = 8


def _dispatch_kernel(pos_ref, pad_ref, h_ref, xs_ref, zero_ref, sem, *, tg, n_tok):
    i = pl.program_id(0)
    base = i * tg

    def copy(k, r):
        return pltpu.make_async_copy(h_ref.at[pl.ds(r, 1), :],
                                     xs_ref.at[pl.ds(pos_ref[k * n_tok + base + r], 1), :], sem)

    def start(r, carry):
        for k in range(TOP_K):
            copy(k, r).start()
        return carry
    lax.fori_loop(0, tg, start, 0, unroll=DMA_UNROLL)

    def wait(r, carry):
        for k in range(TOP_K):
            copy(k, r).wait()
        return carry
    lax.fori_loop(0, tg, wait, 0, unroll=DMA_UNROLL)

    @pl.when(i == pl.num_programs(0) - 1)
    def _():
        zero_ref[...] = jnp.zeros(zero_ref.shape, F32)

        def zero_row(r):
            return pltpu.make_async_copy(zero_ref.at[pl.ds(0, 1), :], xs_ref.at[pl.ds(r, 1), :], sem)

        for e in range(N_EXPERTS):
            lo, hi = pad_ref[e], pad_ref[N_EXPERTS + e]

            def zstart(r, carry):
                zero_row(r).start()
                return carry
            lax.fori_loop(lo, hi, zstart, 0)

            def zwait(r, carry):
                zero_row(r).wait()
                return carry
            lax.fori_loop(lo, hi, zwait, 0)

        tail = pad_ref[2 * N_EXPERTS]

        def zero_tile(c):
            rows = pl.ds(pl.multiple_of(tail + c * tg, tg), tg)
            return pltpu.make_async_copy(zero_ref, xs_ref.at[rows, :], sem)

        def tstart(c, carry):
            zero_tile(c).start()
            return carry
        n_tail = (xs_ref.shape[0] - tail) // tg
        lax.fori_loop(0, n_tail, tstart, 0)

        def twait(c, carry):
            zero_tile(c).wait()
            return carry
        lax.fori_loop(0, n_tail, twait, 0)


def dispatch_rows(h, pos, pad_ranges, npad, *, tg=256):
    t, d = h.shape
    grid_spec = pltpu.PrefetchScalarGridSpec(
        num_scalar_prefetch=2,
        grid=(t // tg,),
        in_specs=[pl.BlockSpec((tg, d), lambda i, pos, pad: (i, 0))],
        out_specs=pl.BlockSpec(memory_space=pl.ANY),
        scratch_shapes=[pltpu.VMEM((tg, d), F32), pltpu.SemaphoreType.DMA(())],
    )
    return pl.pallas_call(
        functools.partial(_dispatch_kernel, tg=tg, n_tok=t),
        grid_spec=grid_spec,
        out_shape=jax.ShapeDtypeStruct((npad, d), F32),
        compiler_params=_cparams(("arbitrary",)),
        name="dispatch_scatter",
    )(pos, pad_ranges, h)


def _combine_kernel(pos_ref, x_ref, rw_ref, ys_ref, o_ref, buf_ref, sem, *, tc, n_tok):
    base = pl.program_id(0) * tc

    def copy(k, r):
        return pltpu.make_async_copy(ys_ref.at[pl.ds(pos_ref[k * n_tok + base + r], 1), :],
                                     buf_ref.at[k, pl.ds(r, 1), :], sem)

    def start(r, carry):
        for k in range(TOP_K):
            copy(k, r).start()
        return carry
    lax.fori_loop(0, tc, start, 0, unroll=DMA_UNROLL)

    def wait(r, carry):
        for k in range(TOP_K):
            copy(k, r).wait()
        return carry
    lax.fori_loop(0, tc, wait, 0, unroll=DMA_UNROLL)
    rw = rw_ref[...]
    o_ref[...] = x_ref[...] + rw[:, 0:1] * buf_ref[0] + rw[:, 1:2] * buf_ref[1]


def combine_rows(x, rw, ys, pos, *, tc=256):
    t, d = x.shape
    grid_spec = pltpu.PrefetchScalarGridSpec(
        num_scalar_prefetch=1,
        grid=(t // tc,),
        in_specs=[pl.BlockSpec((tc, d), lambda i, pos: (i, 0)), pl.BlockSpec((tc, LANES), lambda i, pos: (i, 0)),
                  pl.BlockSpec(memory_space=pl.ANY)],
        out_specs=pl.BlockSpec((tc, d), lambda i, pos: (i, 0)),
        scratch_shapes=[pltpu.VMEM((TOP_K, tc, d), F32), pltpu.SemaphoreType.DMA(())],
    )
    return pl.pallas_call(
        functools.partial(_combine_kernel, tc=tc, n_tok=t),
        grid_spec=grid_spec,
        out_shape=jax.ShapeDtypeStruct((t, d), F32),
        compiler_params=_cparams(("arbitrary",)),
        name="combine_gather",
    )(pos, x, rw, ys)


def _routing_plan(ridx, rank, counts, tm, n_row_tiles):
    padded = ((counts + tm - 1) // tm) * tm
    ends = jnp.cumsum(padded)
    starts = ends - padded
    hit = ridx[:, :, None] == jnp.arange(N_EXPERTS, dtype=jnp.int32)[None, None, :]
    pos = rank + jnp.sum(jnp.where(hit, starts[None, None, :], 0), axis=-1)
    n_tiles = (ends[-1] // tm).astype(jnp.int32).reshape(1)
    tile_start = jnp.minimum(jnp.arange(n_row_tiles, dtype=jnp.int32), n_tiles - 1) * tm
    tile_expert = jnp.sum(tile_start[:, None] >= ends[None, :], axis=-1).astype(jnp.int32)
    pad_ranges = jnp.concatenate([starts + counts, ends, ends[-1:]]).astype(jnp.int32)
    return pos.T.reshape(-1).astype(jnp.int32), pad_ranges, tile_expert, n_tiles


def moe(h, x, ridx, rw, rank, counts, wg, wu, wd, *, tm=512):
    t, d = x.shape
    npad = TOP_K * t + N_EXPERTS * tm
    pos, pad_ranges, tile_expert, n_tiles = _routing_plan(
        ridx[:, :TOP_K], rank[:, :TOP_K], counts[0, :N_EXPERTS], tm, npad // tm)
    xs = dispatch_rows(h, pos, pad_ranges, npad)
    ys = moe_ffn(xs, tile_expert, n_tiles, wg, wu, wd, tm=tm)
    return combine_rows(x, rw, ys, pos)


def _split_w_in(w):
    sizes = (ATTN_WIDTH, ATTN_WIDTH, ATTN_WIDTH, 3 * GDN_WIDTH, GDN_WIDTH, GDN_HEADS, GDN_HEADS,
             SSM_WIDTH, SSM_XBC, SSM_HEADS)
    pieces, start = [], 0
    for size in sizes:
        pieces.append(w[:, start:start + size])
        start += size
    aq, ak, av, gqkv, ggate, gbeta, ga, sz, sxbc, sdt = pieces
    main = jnp.concatenate([gqkv, ggate, sz, sxbc, aq, ak, av], axis=1).astype(BF16)
    small = jnp.concatenate([gbeta, ga, sdt], axis=1)
    small = jnp.pad(small, ((0, 0), (0, LANES - small.shape[1]))).astype(BF16)
    return main, small


def kernel(x, rel_bias, mix_norm, w_in, attn_q_gain, attn_k_gain, attn_lambda, attn_sub_gain, gdn_conv_w, gdn_A_log, gdn_dt_bias, gdn_o_gain, ssm_conv_w, ssm_conv_b, ssm_A_log, ssm_dt_bias, ssm_D, ssm_norm_gain, w_out, ffn_norm, ffn_w_gate, ffn_w_up, ffn_w_down, moe_router, moe_w_gate, moe_w_up, moe_w_down):
    b, s, d = x.shape
    t = b * s
    depth = w_in.shape[0]
    tq = min(256, s)
    bias_tiles = _bias_tiles(rel_bias, tq)
    xf = x.reshape(t, d).astype(F32)
    for li in range(depth):
        w_main, w_small = _split_w_in(w_in[li])
        proj, small = norm_inproj(xf, mix_norm[li].astype(F32), w_main, w_small, tm=min(1024, t))
        proj3 = proj.reshape(b, s, N_MAIN)
        small3 = small.reshape(b, s, LANES)
        attn_o = attention(proj3, attn_q_gain[li], attn_k_gain[li], attn_lambda[li], attn_sub_gain[li],
                           bias_tiles, li, tq=tq)
        gdn_o = gdn(proj3, small3, gdn_conv_w[li], gdn_A_log[li], gdn_dt_bias[li], gdn_o_gain[li])
        ssm_o = ssd(proj3, small3, ssm_conv_w[li], ssm_conv_b[li], ssm_A_log[li], ssm_dt_bias[li],
                    ssm_D[li], ssm_norm_gain[li])
        wo = w_out[li].astype(BF16)
        w_a, w_g, w_s = wo[:ATTN_WIDTH], wo[ATTN_WIDTH:ATTN_WIDTH + GDN_WIDTH], wo[ATTN_WIDTH + GDN_WIDTH:]
        mix = (attn_o.reshape(t, ATTN_WIDTH), gdn_o.reshape(t, GDN_WIDTH), ssm_o.reshape(t, SSM_WIDTH))
        j = li // 2
        if li % 2 == 0:
            xf, h = outproj(*mix, w_a, w_g, w_s, xf, ffn_norm[li], tm=min(512, t))
            xf = ffn(h, xf, ffn_w_gate[j].astype(BF16), ffn_w_up[j].astype(BF16), ffn_w_down[j].astype(BF16),
                     tm=min(512, t))
        else:
            w_r = jnp.pad(moe_router[j], ((0, 0), (0, LANES - N_EXPERTS))).astype(BF16)
            xf, h, ridx, rw, rank, counts = outproj(*mix, w_a, w_g, w_s, xf, ffn_norm[li], w_r, tm=min(512, t))
            xf = moe(h, xf, ridx, rw, rank, counts, moe_w_gate[j].astype(BF16), moe_w_up[j].astype(BF16),
                     moe_w_down[j].astype(BF16), tm=min(512, t))
    return xf.reshape(b, s, d).astype(x.dtype)
```

```python
import functools
import math

import jax
import jax.numpy as jnp
from jax import lax
from jax.experimental import pallas as pl
from jax.experimental.pallas import tpu as pltpu

F32 = jnp.float32
BF16 = jnp.bfloat16

NORM_EPS = 1e-6
CONV_K = 4
LANES = 128
HALO = 8

ATTN_HEADS = 4
ATTN_QK_DIM = 64
ATTN_V_DIM = 128
ATTN_WIDTH = 512
REL_BUCKETS = 32
REL_MAX_DIST = 128

GDN_HEADS = 6
GDN_DIM = 128
GDN_WIDTH = 768

SSM_HEADS = 12
SSM_HEAD_DIM = 64
SSM_GROUPS = 2
SSM_STATE = 128
SSM_WIDTH = 768
SSM_XBC = 1280

N_EXPERTS = 8
TOP_K = 2

GDN_COLS = 4 * GDN_WIDTH
SZ_BLOCK = 4
XBC_BLOCK = 3
ATTN_BLOCK0 = (5 * GDN_WIDTH + SSM_XBC) // LANES
N_MAIN = 5 * GDN_WIDTH + SSM_XBC + 3 * ATTN_WIDTH
BETA_LANE0, GA_LANE0, DT_LANE0 = 0, GDN_HEADS, 2 * GDN_HEADS

VMEM_LIMIT = 56 * 1024 * 1024


def _cparams(sem):
    return pltpu.CompilerParams(dimension_semantics=sem, vmem_limit_bytes=VMEM_LIMIT)


def _dot(a, b):
    return jnp.dot(a.astype(BF16), b.astype(BF16), preferred_element_type=F32)


def _dot_nt(a, b):
    return lax.dot_general(a.astype(BF16), b.astype(BF16), (((1,), (1,)), ((), ())),
                           preferred_element_type=F32)


def _dot_tn(a, b):
    return lax.dot_general(a.astype(BF16), b.astype(BF16), (((0,), (0,)), ((), ())),
                           preferred_element_type=F32)


def _dot_f32(a, b):
    return jnp.dot(a, b, preferred_element_type=F32, precision=lax.Precision.HIGHEST)


def _silu(x):
    return x / (1.0 + jnp.exp(-x))


def _sigmoid(x):
    return 1.0 / (1.0 + jnp.exp(-x))


def _softplus(x):
    return jnp.maximum(x, 0.0) + jnp.log(1.0 + jnp.exp(-jnp.abs(x)))


def _rms(x):
    return x * lax.rsqrt(jnp.mean(x * x, axis=-1, keepdims=True) + NORM_EPS)


def _tril(n, strict=False):
    r = lax.broadcasted_iota(jnp.int32, (n, n), 0)
    c = lax.broadcasted_iota(jnp.int32, (n, n), 1)
    return (r > c) if strict else (r >= c)


def _norm_inproj_kernel(x_ref, g_ref, w_ref, ws_ref, o_ref, os_ref, h_ref):
    @pl.when(pl.program_id(1) == 0)
    def _():
        h = (_rms(x_ref[...]) * g_ref[...]).astype(BF16)
        h_ref[...] = h
        os_ref[...] = jnp.dot(h, ws_ref[...], preferred_element_type=F32)

    o_ref[...] = jnp.dot(h_ref[...], w_ref[...], preferred_element_type=F32).astype(o_ref.dtype)


def norm_inproj(x, gain, w_main, w_small, *, tm=512, tn=3328):
    t, d = x.shape
    n = w_main.shape[1]
    return pl.pallas_call(
        _norm_inproj_kernel,
        grid=(t // tm, n // tn),
        in_specs=[
            pl.BlockSpec((tm, d), lambda i, j: (i, 0)),
            pl.BlockSpec((1, d), lambda i, j: (0, 0)),
            pl.BlockSpec((d, tn), lambda i, j: (0, j)),
            pl.BlockSpec((d, LANES), lambda i, j: (0, 0)),
        ],
        out_specs=[
            pl.BlockSpec((tm, tn), lambda i, j: (i, j)),
            pl.BlockSpec((tm, LANES), lambda i, j: (i, 0)),
        ],
        out_shape=[jax.ShapeDtypeStruct((t, n), BF16), jax.ShapeDtypeStruct((t, LANES), F32)],
        scratch_shapes=[pltpu.VMEM((tm, d), BF16)],
        compiler_params=_cparams(("parallel", "arbitrary")),
        name="norm_inproj",
    )(x, gain.reshape(1, d), w_main, w_small)


def _pair_rms(x, gain):
    lo = lax.broadcasted_iota(jnp.int32, (1, LANES), 1) < ATTN_QK_DIM
    sq = x * x
    s_lo = jnp.sum(jnp.where(lo, sq, 0.0), axis=-1, keepdims=True)
    s_hi = jnp.sum(jnp.where(lo, 0.0, sq), axis=-1, keepdims=True)
    ms = jnp.where(lo, s_lo, s_hi) * (1.0 / ATTN_QK_DIM)
    return x * lax.rsqrt(ms + NORM_EPS) * gain


ONES_ROWS = 16


def _attn_kernel(lam_ref, qg_ref, kg_ref, sg_ref, bias_ref, q_ref, k_ref, v_ref, o_ref,
                 kn_ref, vt_ref, m_ref, acc_ref, *, tq, lam_init, far_group):
    qi = pl.program_id(2)
    lo = lax.broadcasted_iota(jnp.int32, (1, LANES), 1) < ATTN_QK_DIM
    nv = ATTN_V_DIM

    @pl.when(qi == 0)
    def _():
        def body(c, carry):
            rows = pl.ds(pl.multiple_of(c * tq, tq), tq)
            kn_ref[rows, :] = _pair_rms(k_ref[rows, :].astype(F32), kg_ref[...]).astype(BF16)
            vt_ref[c, 0:nv, :] = v_ref[rows, :].astype(F32).T.astype(BF16)
            vt_ref[c, nv:nv + ONES_ROWS, :] = jnp.ones((ONES_ROWS, tq), BF16)
            return carry
        lax.fori_loop(0, k_ref.shape[0] // tq, body, 0)

    q = _pair_rms(q_ref[...].astype(F32), qg_ref[...]) * (ATTN_QK_DIM ** -0.5)
    qz = (jnp.where(lo, q, 0.0).astype(BF16), jnp.where(lo, 0.0, q).astype(BF16))

    def blocks(kb, n, bias, first):
        k_blk = kn_ref[pl.ds(pl.multiple_of(kb * tq, tq), n * tq), :]
        vt = vt_ref[kb] if n == 1 else jnp.concatenate([vt_ref[kb + i] for i in range(n)], axis=-1)
        maps = range(2)
        s = [lax.dot_general(k_blk, qz[mp], (((1,), (1,)), ((), ())), preferred_element_type=F32) for mp in maps]
        if bias is not None:
            s = [x + bias for x in s]
        s_max = [jnp.max(x, axis=0, keepdims=True) for x in s]
        if first:
            m_new = s_max
        else:
            m_prev = [m_ref[mp] for mp in maps]
            m_new = [jnp.maximum(m_prev[mp], s_max[mp]) for mp in maps]
            alpha = [jnp.exp(m_prev[mp] - m_new[mp]) for mp in maps]
        p = [jnp.exp(s[mp] - m_new[mp]).astype(BF16) for mp in maps]
        pv = [jnp.dot(vt, p[mp], preferred_element_type=F32) for mp in maps]
        for mp in maps:
            acc_ref[mp] = pv[mp] if first else alpha[mp] * acc_ref[mp] + pv[mp]
            m_ref[mp] = m_new[mp]

    @pl.when(qi == 0)
    def _():
        blocks(qi, 1, bias_ref[1], True)

    @pl.when(qi >= 1)
    def _():
        blocks(qi - 1, 2, bias_ref[...].reshape(2 * tq, tq), True)

    n_far = jnp.maximum(qi - 1, 0)
    n_groups = n_far // far_group

    def far(gi, carry):
        blocks(gi * far_group, far_group, None, False)
        return carry
    lax.fori_loop(0, n_groups, far, 0)
    rem = n_far - n_groups * far_group
    for r in range(1, far_group):
        @pl.when(rem == r)
        def _(r=r):
            blocks(n_groups * far_group, r, None, False)

    lam = lam_ref[...]
    lam_full = (jnp.exp(jnp.sum(lam[0:1] * lam[1:2], axis=-1, keepdims=True))
                - jnp.exp(jnp.sum(lam[2:3] * lam[3:4], axis=-1, keepdims=True)) + lam_init)
    a0, a1 = acc_ref[0], acc_ref[1]
    o = a0[0:nv] / a0[nv:nv + 1] - lam_full * (a1[0:nv] / a1[nv:nv + 1])
    o = o * lax.rsqrt(jnp.mean(o * o, axis=0, keepdims=True) + NORM_EPS)
    o_ref[...] = (o.T * sg_ref[...] * (1.0 - lam_init)).astype(o_ref.dtype)


def _t5_bucket(n):
    max_exact = REL_BUCKETS // 2
    nf = jnp.maximum(n, max_exact).astype(F32)
    large = max_exact + (jnp.log(nf / max_exact) / math.log(REL_MAX_DIST / max_exact)
                         * (REL_BUCKETS - max_exact)).astype(jnp.int32)
    return jnp.where(n < max_exact, n, jnp.minimum(large, REL_BUCKETS - 1))


def _bias_tiles(rel_bias, tq):
    assert tq >= REL_MAX_DIST
    table = rel_bias.astype(F32)
    i = jnp.arange(tq, dtype=jnp.int32)
    tiles = []
    for d in (1, 0):
        rel = d * tq + i[None, :] - i[:, None]
        bucket = _t5_bucket(jnp.maximum(rel, 0))
        shifted = (table - table[REL_BUCKETS - 1]).T
        hit = bucket[None, :, :, None] == jnp.arange(REL_BUCKETS, dtype=jnp.int32)
        b = jnp.sum(jnp.where(hit, shifted[:, None, None, :], 0.0), axis=-1)
        tiles.append(jnp.where((rel >= 0)[None], b, -jnp.inf))
    return jnp.stack(tiles, axis=1)


def attention(proj3, q_gain, k_gain, lam, sub_gain, bias_tiles, layer_idx, *, tq=256, far_group=4):
    b, s, _ = proj3.shape
    h = ATTN_HEADS
    lam_init = 0.8 - 0.6 * math.exp(-0.3 * layer_idx)
    qg = jnp.tile(q_gain.astype(F32), 2).reshape(1, LANES)
    kg = jnp.tile(k_gain.astype(F32), 2).reshape(1, LANES)
    sg = sub_gain.astype(F32).reshape(1, LANES)
    kern = functools.partial(_attn_kernel, tq=tq, lam_init=lam_init, far_group=far_group)
    const = lambda shape: pl.BlockSpec(shape, lambda bi, hi, qi: (0,) * len(shape))
    return pl.pallas_call(
        kern,
        grid=(b, h, s // tq),
        in_specs=[
            const((4, ATTN_QK_DIM)), const((1, LANES)), const((1, LANES)), const((1, LANES)),
            pl.BlockSpec((None, 2, tq, tq), lambda bi, hi, qi: (hi, 0, 0, 0)),
            pl.BlockSpec((None, tq, LANES), lambda bi, hi, qi: (bi, qi, ATTN_BLOCK0 + hi)),
            pl.BlockSpec((None, s, LANES), lambda bi, hi, qi: (bi, 0, ATTN_BLOCK0 + h + hi)),
            pl.BlockSpec((None, s, LANES), lambda bi, hi, qi: (bi, 0, ATTN_BLOCK0 + 2 * h + hi)),
        ],
        out_specs=pl.BlockSpec((None, tq, LANES), lambda bi, hi, qi: (bi, qi, hi)),
        out_shape=jax.ShapeDtypeStruct((b, s, ATTN_WIDTH), BF16),
        scratch_shapes=[
            pltpu.VMEM((s, LANES), BF16),
            pltpu.VMEM((s // tq, ATTN_V_DIM + ONES_ROWS, tq), BF16),
            pltpu.VMEM((2, 1, tq), F32),
            pltpu.VMEM((2, ATTN_V_DIM + ONES_ROWS, tq), F32),
        ],
        compiler_params=_cparams(("parallel", "parallel", "arbitrary")),
        name="diff_attention",
    )(lam.astype(F32), qg, kg, sg, bias_tiles, proj3, proj3, proj3)


def _causal_conv(x, w_ref, buf_ref, first):
    rows = x.shape[0]

    @pl.when(first)
    def _():
        buf_ref[0:HALO, :] = jnp.zeros((HALO, x.shape[1]), F32)

    buf_ref[HALO:HALO + rows, :] = x
    y = w_ref[CONV_K - 1:CONV_K, :] * x
    for j in range(CONV_K - 1):
        off = HALO - (CONV_K - 1) + j
        y = y + w_ref[j:j + 1, :] * buf_ref[off:off + rows, :]
    buf_ref[0:HALO, :] = x[rows - HALO:, :]
    return y


def _unit_lower_inverses(mats):
    n = mats[0].shape[0]
    eye = (lax.broadcasted_iota(jnp.int32, (n, n), 0)
           == lax.broadcasted_iota(jnp.int32, (n, n), 1)).astype(F32)
    ps = [eye - a for a in mats]
    bs = [_dot(a, a) for a in mats]
    steps = int(math.log2(n)) - 1
    for i in range(steps):
        if i + 1 < steps:
            both = [_dot(jnp.concatenate([p, b], axis=0), b) for p, b in zip(ps, bs)]
            ps = [p + x[:n] for p, x in zip(ps, both)]
            bs = [x[n:] for x in both]
        else:
            ps = [p + _dot(p, b) for p, b in zip(ps, bs)]
    return ps


def _gdn_kernel(blk_ref, sm_ref, cw_ref, arow_ref, dtb_ref, og_ref, o_ref, buf_ref, st_ref, *, chunk):
    first = pl.program_id(1) == 0
    c = chunk
    d = GDN_DIM
    nq = 3 * GDN_WIDTH
    tile = blk_ref.shape[0]
    heads = range(GDN_HEADS)
    chunks = range(tile // c)

    @pl.when(first)
    def _():
        st_ref[...] = jnp.zeros(st_ref.shape, F32)

    qkv = _silu(_causal_conv(blk_ref[:, 0:nq].astype(F32), cw_ref, buf_ref, first))
    sm = sm_ref[...]
    beta = _sigmoid(sm)
    g = arow_ref[...] * _softplus(sm + dtb_ref[...])
    strict = _tril(c, strict=True)
    causal = _tril(c)
    r = lax.broadcasted_iota(jnp.int32, (tile, tile), 0)
    cc = lax.broadcasted_iota(jnp.int32, (tile, tile), 1)
    same_chunk_tril = jnp.logical_and(r >= cc, (r // c) == (cc // c))
    gc = _dot_f32(same_chunk_tril.astype(F32), g)
    gct = gc.T

    l2 = lambda x: x * lax.rsqrt(jnp.sum(x * x, axis=-1, keepdims=True) + NORM_EPS)
    qn = [l2(qkv[:, h * d:(h + 1) * d]) * (d ** -0.5) for h in heads]
    kn = [l2(qkv[:, GDN_WIDTH + h * d:GDN_WIDTH + (h + 1) * d]) for h in heads]

    pairs = [(ci, h) for ci in chunks for h in heads]
    prep = {}
    for ci, h in pairs:
        rows = slice(ci * c, (ci + 1) * c)
        q, k = qn[h][rows], kn[h][rows]
        v = qkv[rows, 2 * GDN_WIDTH + h * d:2 * GDN_WIDTH + (h + 1) * d]
        bcol = beta[rows, BETA_LANE0 + h:BETA_LANE0 + h + 1]
        col = gc[rows, GA_LANE0 + h:GA_LANE0 + h + 1]
        row = gct[GA_LANE0 + h:GA_LANE0 + h + 1, rows]
        last = gc[(ci + 1) * c - 1:(ci + 1) * c, GA_LANE0 + h:GA_LANE0 + h + 1]
        dec = jnp.exp(jnp.where(causal, col - row, -jnp.inf))
        ecol = jnp.exp(col)
        kb = k * bcol
        prep[ci, h] = dict(k=k, kb=kb, dec=dec, last=last,
                           lhs=jnp.concatenate([kb, q], axis=0),
                           rhs=jnp.concatenate([v * bcol, kb * ecol], axis=-1),
                           qd=q * ecol, kd=k * jnp.exp(last - col))
    for key in pairs:
        x = prep[key]
        both = _dot_nt(x["lhs"], x["k"])
        x["a"] = both[:c] * jnp.where(strict, x["dec"], 0.0)
        x["qk"] = both[c:] * x["dec"]
    t_inv = _unit_lower_inverses([prep[key]["a"] for key in pairs])
    for key, t in zip(pairs, t_inv):
        prep[key]["sol"] = _dot(t, prep[key]["rhs"])

    states = [st_ref[h] for h in heads]
    for ci in chunks:
        xs = [prep[ci, h] for h in heads]
        both = [_dot(jnp.concatenate([x["sol"][:, d:], x["qd"]], axis=0), st) for x, st in zip(xs, states)]
        v_new = [x["sol"][:, :d] - y[:c] for x, y in zip(xs, both)]
        inter = [_dot(x["qk"], vn) for x, vn in zip(xs, v_new)]
        upd = [_dot_tn(x["kd"], vn) for x, vn in zip(xs, v_new)]
        states = [st * jnp.exp(x["last"]) + u for st, x, u in zip(states, xs, upd)]
        rows = slice(ci * c, (ci + 1) * c)
        for h in heads:
            o = both[h][c:] + inter[h]
            gate = blk_ref[rows, nq + h * d:nq + (h + 1) * d].astype(F32)
            o_ref[rows, h * d:(h + 1) * d] = (_rms(o) * og_ref[...] * _silu(gate)).astype(o_ref.dtype)
    for h in heads:
        st_ref[h] = states[h]


def gdn(proj3, small3, conv_w, a_log, dt_bias, o_gain, *, chunk=64, tile=256):
    b, s, _ = proj3.shape
    tile = min(tile, s)
    pad = lambda v, lane0: jnp.zeros((1, LANES), F32).at[0, lane0:lane0 + v.shape[0]].set(v.astype(F32))
    arow = pad(-jnp.exp(a_log.astype(F32)), GA_LANE0)
    dtb = pad(dt_bias, GA_LANE0)
    const = lambda shape: pl.BlockSpec(shape, lambda bi, si: (0,) * len(shape))
    return pl.pallas_call(
        functools.partial(_gdn_kernel, chunk=chunk),
        grid=(b, s // tile),
        in_specs=[
            pl.BlockSpec((None, tile, GDN_COLS), lambda bi, si: (bi, si, 0)),
            pl.BlockSpec((None, tile, LANES), lambda bi, si: (bi, si, 0)),
            const((CONV_K, 3 * GDN_WIDTH)), const((1, LANES)), const((1, LANES)), const((1, GDN_DIM)),
        ],
        out_specs=pl.BlockSpec((None, tile, GDN_WIDTH), lambda bi, si: (bi, si, 0)),
        out_shape=jax.ShapeDtypeStruct((b, s, GDN_WIDTH), BF16),
        scratch_shapes=[
            pltpu.VMEM((HALO + tile, 3 * GDN_WIDTH), F32),
            pltpu.VMEM((GDN_HEADS, GDN_DIM, GDN_DIM), F32),
        ],
        compiler_params=_cparams(("parallel", "arbitrary")),
        name="gated_deltanet",
    )(proj3, small3, conv_w.astype(F32), arow, dtb, o_gain.astype(F32).reshape(1, GDN_DIM))


def _ssd_kernel(xbc_ref, z_ref, sm_ref, cw_ref, cb_ref, arow_ref, dtb_ref, dsk_ref, ng_ref, o_ref,
                buf_ref, st_ref, y_ref, *, chunk):
    first = pl.program_id(1) == 0
    c = chunk
    p2 = 2 * SSM_HEAD_DIM
    heads_per_group = SSM_HEADS // SSM_GROUPS
    gw = SSM_WIDTH // SSM_GROUPS

    @pl.when(first)
    def _():
        st_ref[...] = jnp.zeros(st_ref.shape, F32)

    xbc = _silu(_causal_conv(xbc_ref[...].astype(F32), cw_ref, buf_ref, first) + cb_ref[...])
    x = xbc[:, :SSM_WIDTH]
    sm = sm_ref[...]
    dt = _softplus(sm + dtb_ref[...])
    causal = _tril(c)
    acum = _dot_f32(causal.astype(F32), dt * arow_ref[...])
    acum_t = acum.T
    lo = lax.broadcasted_iota(jnp.int32, (1, p2), 1) < SSM_HEAD_DIM
    halves = lambda lane, arr: (arr[:, lane:lane + 1], arr[:, lane + 1:lane + 2])
    sel = lambda pair: jnp.where(lo, pair[0], pair[1])

    for pr in range(SSM_HEADS // 2):
        grp = (2 * pr) // heads_per_group
        lane = DT_LANE0 + 2 * pr
        bm = xbc[:, SSM_WIDTH + grp * SSM_STATE:SSM_WIDTH + (grp + 1) * SSM_STATE]
        cm = xbc[:, SSM_WIDTH + (SSM_GROUPS + grp) * SSM_STATE:SSM_WIDTH + (SSM_GROUPS + grp + 1) * SSM_STATE]
        cb = _dot_nt(cm, bm)
        xp = x[:, pr * p2:(pr + 1) * p2]
        cols = halves(lane, acum)
        lasts = halves(lane, acum[c - 1:c, :])
        xdt = xp * sel(halves(lane, dt))
        y_diag = []
        for hh in range(2):
            row = acum_t[lane + hh:lane + hh + 1, :]
            lmat = jnp.exp(jnp.where(causal, cols[hh] - row, -jnp.inf))
            y_diag.append(_dot(cb * lmat, xdt))
        prev = st_ref[pr]
        y_off = _dot(cm, prev) * sel((jnp.exp(cols[0]), jnp.exp(cols[1])))
        decay_in = sel((jnp.exp(lasts[0] - cols[0]), jnp.exp(lasts[1] - cols[1])))
        st_ref[pr] = prev * sel((jnp.exp(lasts[0]), jnp.exp(lasts[1]))) + _dot_tn(bm, xdt * decay_in)
        y_ref[:, pr * p2:(pr + 1) * p2] = jnp.where(lo, y_diag[0], y_diag[1]) + y_off

    y = (y_ref[...] + dsk_ref[...] * x) * _silu(z_ref[...].astype(F32))
    for grp in range(SSM_GROUPS):
        cols = slice(grp * gw, (grp + 1) * gw)
        o_ref[:, cols] = (_rms(y[:, cols]) * ng_ref[:, cols]).astype(o_ref.dtype)


def ssd(proj3, small3, conv_w, conv_b, a_log, dt_bias, d_skip, norm_gain, *, chunk=128):
    b, s, _ = proj3.shape
    pad = lambda v: jnp.zeros((1, LANES), F32).at[0, DT_LANE0:DT_LANE0 + SSM_HEADS].set(v.astype(F32))
    arow = pad(-jnp.exp(a_log.astype(F32)))
    dtb = pad(dt_bias)
    dsk = jnp.repeat(d_skip.astype(F32), SSM_HEAD_DIM).reshape(1, SSM_WIDTH)
    const = lambda shape: pl.BlockSpec(shape, lambda bi, si: (0,) * len(shape))
    return pl.pallas_call(
        functools.partial(_ssd_kernel, chunk=chunk),
        grid=(b, s // chunk),
        in_specs=[
            pl.BlockSpec((None, chunk, SSM_XBC), lambda bi, si: (bi, si, XBC_BLOCK)),
            pl.BlockSpec((None, chunk, SSM_WIDTH), lambda bi, si: (bi, si, SZ_BLOCK)),
            pl.BlockSpec((None, chunk, LANES), lambda bi, si: (bi, si, 0)),
            const((CONV_K, SSM_XBC)), const((1, SSM_XBC)), const((1, LANES)), const((1, LANES)),
            const((1, SSM_WIDTH)), const((1, SSM_WIDTH)),
        ],
        out_specs=pl.BlockSpec((None, chunk, SSM_WIDTH), lambda bi, si: (bi, si, 0)),
        out_shape=jax.ShapeDtypeStruct((b, s, SSM_WIDTH), BF16),
        scratch_shapes=[
            pltpu.VMEM((HALO + chunk, SSM_XBC), F32),
            pltpu.VMEM((SSM_HEADS // 2, SSM_STATE, 2 * SSM_HEAD_DIM), F32),
            pltpu.VMEM((chunk, SSM_WIDTH), F32),
        ],
        compiler_params=_cparams(("parallel", "arbitrary")),
        name="mamba2_ssd",
    )(proj3, proj3, small3, conv_w.astype(F32), conv_b.astype(F32).reshape(1, SSM_XBC), arow, dtb, dsk,
      norm_gain.astype(F32).reshape(1, SSM_WIDTH))


def _outproj_kernel(*refs, with_router):
    if with_router:
        (a_ref, g_ref, s_ref, wa_ref, wg_ref, ws_ref, x_ref, ng_ref, wr_ref,
         xo_ref, h_ref, ri_ref, rw_ref, rk_ref, cnt_ref, run_ref) = refs
    else:
        a_ref, g_ref, s_ref, wa_ref, wg_ref, ws_ref, x_ref, ng_ref, xo_ref, h_ref = refs
    y = (jnp.dot(a_ref[...], wa_ref[...], preferred_element_type=F32)
         + jnp.dot(g_ref[...], wg_ref[...], preferred_element_type=F32)
         + jnp.dot(s_ref[...], ws_ref[...], preferred_element_type=F32))
    xn = x_ref[...] + y
    xo_ref[...] = xn
    hf = _rms(xn) * ng_ref[...]
    h = hf.astype(BF16)
    h_ref[...] = hf.astype(h_ref.dtype)
    if with_router:
        tm = xn.shape[0]
        lane = lax.broadcasted_iota(jnp.int32, (1, LANES), 1)
        logits = jnp.where(lane < N_EXPERTS, jnp.dot(h, wr_ref[...], preferred_element_type=F32), -jnp.inf)
        v1 = jnp.max(logits, axis=-1, keepdims=True)
        i1 = jnp.min(jnp.where(logits == v1, lane, LANES), axis=-1, keepdims=True)
        rest = jnp.where(lane == i1, -jnp.inf, logits)
        v2 = jnp.max(rest, axis=-1, keepdims=True)
        i2 = jnp.min(jnp.where(rest == v2, lane, LANES), axis=-1, keepdims=True)
        e2 = jnp.exp(v2 - v1)
        ri_ref[...] = jnp.where(lane == 0, i1, i2)
        rw_ref[...] = jnp.where(lane == 0, 1.0 / (1.0 + e2), e2 / (1.0 + e2))

        @pl.when(pl.program_id(0) == 0)
        def _():
            run_ref[...] = jnp.zeros(run_ref.shape, F32)

        before = _tril(tm, strict=True).astype(BF16)
        run = run_ref[0:1, :]
        ranks = []
        for idx in (i1, i2):
            hit = lane == idx
            onehot = hit.astype(F32)
            earlier = jnp.dot(before, onehot.astype(BF16), preferred_element_type=F32) + run
            ranks.append(jnp.sum(jnp.where(hit, earlier, 0.0), axis=-1, keepdims=True))
            run = run + jnp.sum(onehot, axis=0, keepdims=True)
        rk_ref[...] = jnp.where(lane == 0, ranks[0], ranks[1]).astype(jnp.int32)
        run_ref[...] = jnp.broadcast_to(run, run_ref.shape)
        cnt_ref[...] = jnp.broadcast_to(run, cnt_ref.shape).astype(jnp.int32)


def outproj(attn_o, gdn_o, ssm_o, w_a, w_g, w_s, x, norm_gain, w_router=None, *, tm=512):
    t, d = x.shape
    with_router = w_router is not None
    row = lambda width: pl.BlockSpec((tm, width), lambda i: (i, 0))
    const = lambda shape: pl.BlockSpec(shape, lambda i: (0, 0))
    in_specs = [row(ATTN_WIDTH), row(GDN_WIDTH), row(SSM_WIDTH),
                const((ATTN_WIDTH, d)), const((GDN_WIDTH, d)), const((SSM_WIDTH, d)),
                row(d), const((1, d))]
    args = [attn_o, gdn_o, ssm_o, w_a, w_g, w_s, x, norm_gain.astype(F32).reshape(1, d)]
    out_specs = [row(d), row(d)]
    out_shape = [jax.ShapeDtypeStruct((t, d), F32), jax.ShapeDtypeStruct((t, d), F32 if with_router else BF16)]
    scratch = []
    if with_router:
        in_specs.append(const((d, LANES)))
        args.append(w_router)
        out_specs += [row(LANES), row(LANES), row(LANES), const((8, LANES))]
        out_shape += [jax.ShapeDtypeStruct((t, LANES), jnp.int32), jax.ShapeDtypeStruct((t, LANES), F32),
                      jax.ShapeDtypeStruct((t, LANES), jnp.int32), jax.ShapeDtypeStruct((8, LANES), jnp.int32)]
        scratch = [pltpu.VMEM((8, LANES), F32)]
    return pl.pallas_call(
        functools.partial(_outproj_kernel, with_router=with_router),
        grid=(t // tm,),
        in_specs=in_specs, out_specs=out_specs, out_shape=out_shape, scratch_shapes=scratch,
        compiler_params=_cparams(("arbitrary" if with_router else "parallel",)),
        name="outproj_router" if with_router else "outproj",
    )(*args)


def _ffn_kernel(h_ref, x_ref, wg_ref, wu_ref, wd_ref, o_ref):
    @pl.when(pl.program_id(1) == 0)
    def _():
        o_ref[...] = x_ref[...]

    h = h_ref[...]
    g = jnp.dot(h, wg_ref[...].astype(BF16), preferred_element_type=F32)
    u = jnp.dot(h, wu_ref[...].astype(BF16), preferred_element_type=F32)
    o_ref[...] += jnp.dot((_silu(g) * u).astype(BF16), wd_ref[...].astype(BF16), preferred_element_type=F32)


def ffn(h, x, wg, wu, wd, *, tm=1024, tf=256):
    t, d = x.shape
    f = wg.shape[1]
    return pl.pallas_call(
        _ffn_kernel,
        grid=(t // tm, f // tf),
        in_specs=[
            pl.BlockSpec((tm, d), lambda i, j: (i, 0)),
            pl.BlockSpec((tm, d), lambda i, j: (i, 0), pipeline_mode=pl.Buffered(1)),
            pl.BlockSpec((d, tf), lambda i, j: (0, j)),
            pl.BlockSpec((d, tf), lambda i, j: (0, j)),
            pl.BlockSpec((tf, d), lambda i, j: (j, 0)),
        ],
        out_specs=pl.BlockSpec((tm, d), lambda i, j: (i, 0)),
        out_shape=jax.ShapeDtypeStruct((t, d), F32),
        compiler_params=_cparams(("parallel", "arbitrary")),
        name="ffn_swiglu",
    )(h, x, wg, wu, wd)


def _moe_ffn_kernel(te_ref, nt_ref, h_ref, wg_ref, wu_ref, wd_ref, o_ref, hb_ref):
    i = pl.program_id(0)
    j = pl.program_id(1)
    active = i < nt_ref[0]

    @pl.when(j == 0)
    def _():
        o_ref[...] = jnp.zeros(o_ref.shape, F32)

    @pl.when(jnp.logical_and(active, j == 0))
    def _():
        hb_ref[...] = h_ref[...].astype(BF16)

    @pl.when(active)
    def _():
        h = hb_ref[...]
        g = jnp.dot(h, wg_ref[...].astype(BF16), preferred_element_type=F32)
        u = jnp.dot(h, wu_ref[...].astype(BF16), preferred_element_type=F32)
        o_ref[...] += jnp.dot((_silu(g) * u).astype(BF16), wd_ref[...].astype(BF16), preferred_element_type=F32)


def moe_ffn(hs, tile_expert, n_tiles, wg, wu, wd, *, tm, tf=512):
    npad, d = hs.shape
    f = wg.shape[2]
    nj = f // tf
    col = lambda i, j, nt: jnp.where(i < nt[0], j, nj - 1)
    grid_spec = pltpu.PrefetchScalarGridSpec(
        num_scalar_prefetch=2,
        grid=(npad // tm, f // tf),
        in_specs=[
            pl.BlockSpec((tm, d), lambda i, j, te, nt: (jnp.minimum(i, nt[0] - 1), 0)),
            pl.BlockSpec((None, d, tf), lambda i, j, te, nt: (te[i], 0, col(i, j, nt))),
            pl.BlockSpec((None, d, tf), lambda i, j, te, nt: (te[i], 0, col(i, j, nt))),
            pl.BlockSpec((None, tf, d), lambda i, j, te, nt: (te[i], col(i, j, nt), 0)),
        ],
        out_specs=pl.BlockSpec((tm, d), lambda i, j, te, nt: (i, 0)),
        scratch_shapes=[pltpu.VMEM((tm, d), BF16)],
    )
    return pl.pallas_call(
        _moe_ffn_kernel,
        grid_spec=grid_spec,
        out_shape=jax.ShapeDtypeStruct((npad, d), F32),
        compiler_params=_cparams(("parallel", "arbitrary")),
        name="moe_grouped_swiglu",
    )(tile_expert, n_tiles, hs, wg, wu, wd)


DMA_UNROLL = 8


def _dispatch_kernel(pos_ref, pad_ref, h_ref, xs_ref, zero_ref, sem, *, tg, n_tok):
    i = pl.program_id(0)
    base = i * tg

    def copy(k, r):
        return pltpu.make_async_copy(h_ref.at[pl.ds(r, 1), :],
                                     xs_ref.at[pl.ds(pos_ref[k * n_tok + base + r], 1), :], sem)

    def start(r, carry):
        for k in range(TOP_K):
            copy(k, r).start()
        return carry
    lax.fori_loop(0, tg, start, 0, unroll=DMA_UNROLL)

    def wait(r, carry):
        for k in range(TOP_K):
            copy(k, r).wait()
        return carry
    lax.fori_loop(0, tg, wait, 0, unroll=DMA_UNROLL)

    @pl.when(i == pl.num_programs(0) - 1)
    def _():
        zero_ref[...] = jnp.zeros(zero_ref.shape, F32)

        def zero_row(r):
            return pltpu.make_async_copy(zero_ref.at[pl.ds(0, 1), :], xs_ref.at[pl.ds(r, 1), :], sem)

        for e in range(N_EXPERTS):
            lo, hi = pad_ref[e], pad_ref[N_EXPERTS + e]

            def zstart(r, carry):
                zero_row(r).start()
                return carry
            lax.fori_loop(lo, hi, zstart, 0)

            def zwait(r, carry):
                zero_row(r).wait()
                return carry
            lax.fori_loop(lo, hi, zwait, 0)

        tail = pad_ref[2 * N_EXPERTS]

        def zero_tile(c):
            rows = pl.ds(pl.multiple_of(tail + c * tg, tg), tg)
            return pltpu.make_async_copy(zero_ref, xs_ref.at[rows, :], sem)

        def tstart(c, carry):
            zero_tile(c).start()
            return carry
        n_tail = (xs_ref.shape[0] - tail) // tg
        lax.fori_loop(0, n_tail, tstart, 0)

        def twait(c, carry):
            zero_tile(c).wait()
            return carry
        lax.fori_loop(0, n_tail, twait, 0)


def dispatch_rows(h, pos, pad_ranges, npad, *, tg=256):
    t, d = h.shape
    grid_spec = pltpu.PrefetchScalarGridSpec(
        num_scalar_prefetch=2,
        grid=(t // tg,),
        in_specs=[pl.BlockSpec((tg, d), lambda i, pos, pad: (i, 0))],
        out_specs=pl.BlockSpec(memory_space=pl.ANY),
        scratch_shapes=[pltpu.VMEM((tg, d), F32), pltpu.SemaphoreType.DMA(())],
    )
    return pl.pallas_call(
        functools.partial(_dispatch_kernel, tg=tg, n_tok=t),
        grid_spec=grid_spec,
        out_shape=jax.ShapeDtypeStruct((npad, d), F32),
        compiler_params=_cparams(("arbitrary",)),
        name="dispatch_scatter",
    )(pos, pad_ranges, h)


def _combine_kernel(pos_ref, x_ref, rw_ref, ys_ref, o_ref, buf_ref, sem, *, tc, n_tok):
    base = pl.program_id(0) * tc

    def copy(k, r):
        return pltpu.make_async_copy(ys_ref.at[pl.ds(pos_ref[k * n_tok + base + r], 1), :],
                                     buf_ref.at[k, pl.ds(r, 1), :], sem)

    def start(r, carry):
        for k in range(TOP_K):
            copy(k, r).start()
        return carry
    lax.fori_loop(0, tc, start, 0, unroll=DMA_UNROLL)

    def wait(r, carry):
        for k in range(TOP_K):
            copy(k, r).wait()
        return carry
    lax.fori_loop(0, tc, wait, 0, unroll=DMA_UNROLL)
    rw = rw_ref[...]
    o_ref[...] = x_ref[...] + rw[:, 0:1] * buf_ref[0] + rw[:, 1:2] * buf_ref[1]


def combine_rows(x, rw, ys, pos, *, tc=256):
    t, d = x.shape
    grid_spec = pltpu.PrefetchScalarGridSpec(
        num_scalar_prefetch=1,
        grid=(t // tc,),
        in_specs=[pl.BlockSpec((tc, d), lambda i, pos: (i, 0)), pl.BlockSpec((tc, LANES), lambda i, pos: (i, 0)),
                  pl.BlockSpec(memory_space=pl.ANY)],
        out_specs=pl.BlockSpec((tc, d), lambda i, pos: (i, 0)),
        scratch_shapes=[pltpu.VMEM((TOP_K, tc, d), F32), pltpu.SemaphoreType.DMA(())],
    )
    return pl.pallas_call(
        functools.partial(_combine_kernel, tc=tc, n_tok=t),
        grid_spec=grid_spec,
        out_shape=jax.ShapeDtypeStruct((t, d), F32),
        compiler_params=_cparams(("arbitrary",)),
        name="combine_gather",
    )(pos, x, rw, ys)


def _routing_plan(ridx, rank, counts, tm, n_row_tiles):
    padded = ((counts + tm - 1) // tm) * tm
    ends = jnp.cumsum(padded)
    starts = ends - padded
    hit = ridx[:, :, None] == jnp.arange(N_EXPERTS, dtype=jnp.int32)[None, None, :]
    pos = rank + jnp.sum(jnp.where(hit, starts[None, None, :], 0), axis=-1)
    n_tiles = (ends[-1] // tm).astype(jnp.int32).reshape(1)
    tile_start = jnp.minimum(jnp.arange(n_row_tiles, dtype=jnp.int32), n_tiles - 1) * tm
    tile_expert = jnp.sum(tile_start[:, None] >= ends[None, :], axis=-1).astype(jnp.int32)
    pad_ranges = jnp.concatenate([starts + counts, ends, ends[-1:]]).astype(jnp.int32)
    return pos.T.reshape(-1).astype(jnp.int32), pad_ranges, tile_expert, n_tiles


def moe(h, x, ridx, rw, rank, counts, wg, wu, wd, *, tm=768, tf=256, tg=256):
    t, d = x.shape
    assert tm % tg == 0 and t % tg == 0
    npad = pl.cdiv(TOP_K * t + N_EXPERTS * tm, tm) * tm
    pos, pad_ranges, tile_expert, n_tiles = _routing_plan(
        ridx[:, :TOP_K], rank[:, :TOP_K], counts[0, :N_EXPERTS], tm, npad // tm)
    xs = dispatch_rows(h, pos, pad_ranges, npad, tg=tg)
    ys = moe_ffn(xs, tile_expert, n_tiles, wg, wu, wd, tm=tm, tf=tf)
    return combine_rows(x, rw, ys, pos, tc=tg)


def _split_w_in(w):
    sizes = (ATTN_WIDTH, ATTN_WIDTH, ATTN_WIDTH, 3 * GDN_WIDTH, GDN_WIDTH, GDN_HEADS, GDN_HEADS,
             SSM_WIDTH, SSM_XBC, SSM_HEADS)
    pieces, start = [], 0
    for size in sizes:
        pieces.append(w[:, start:start + size])
        start += size
    aq, ak, av, gqkv, ggate, gbeta, ga, sz, sxbc, sdt = pieces
    main = jnp.concatenate([gqkv, ggate, sz, sxbc, aq, ak, av], axis=1).astype(BF16)
    small = jnp.concatenate([gbeta, ga, sdt], axis=1)
    small = jnp.pad(small, ((0, 0), (0, LANES - small.shape[1]))).astype(BF16)
    return main, small


def kernel(x, rel_bias, mix_norm, w_in, attn_q_gain, attn_k_gain, attn_lambda, attn_sub_gain, gdn_conv_w, gdn_A_log, gdn_dt_bias, gdn_o_gain, ssm_conv_w, ssm_conv_b, ssm_A_log, ssm_dt_bias, ssm_D, ssm_norm_gain, w_out, ffn_norm, ffn_w_gate, ffn_w_up, ffn_w_down, moe_router, moe_w_gate, moe_w_up, moe_w_down):
    b, s, d = x.shape
    t = b * s
    depth = w_in.shape[0]
    tq = min(512, s)
    bias_tiles = _bias_tiles(rel_bias, tq)
    xf = x.reshape(t, d).astype(F32)
    for li in range(depth):
        w_main, w_small = _split_w_in(w_in[li])
        proj, small = norm_inproj(xf, mix_norm[li].astype(F32), w_main, w_small, tm=min(512, t))
        proj3 = proj.reshape(b, s, N_MAIN)
        small3 = small.reshape(b, s, LANES)
        attn_o = attention(proj3, attn_q_gain[li], attn_k_gain[li], attn_lambda[li], attn_sub_gain[li],
                           bias_tiles, li, tq=tq, far_group=2)
        gdn_o = gdn(proj3, small3, gdn_conv_w[li], gdn_A_log[li], gdn_dt_bias[li], gdn_o_gain[li])
        ssm_o = ssd(proj3, small3, ssm_conv_w[li], ssm_conv_b[li], ssm_A_log[li], ssm_dt_bias[li],
                    ssm_D[li], ssm_norm_gain[li])
        wo = w_out[li].astype(BF16)
        w_a, w_g, w_s = wo[:ATTN_WIDTH], wo[ATTN_WIDTH:ATTN_WIDTH + GDN_WIDTH], wo[ATTN_WIDTH + GDN_WIDTH:]
        mix = (attn_o.reshape(t, ATTN_WIDTH), gdn_o.reshape(t, GDN_WIDTH), ssm_o.reshape(t, SSM_WIDTH))
        j = li // 2
        if li % 2 == 0:
            xf, h = outproj(*mix, w_a, w_g, w_s, xf, ffn_norm[li], tm=min(512, t))
            xf = ffn(h, xf, ffn_w_gate[j], ffn_w_up[j], ffn_w_down[j], tm=min(1024, t))
        else:
            w_r = jnp.pad(moe_router[j], ((0, 0), (0, LANES - N_EXPERTS))).astype(BF16)
            xf, h, ridx, rw, rank, counts = outproj(*mix, w_a, w_g, w_s, xf, ffn_norm[li], w_r, tm=min(512, t))
            xf = moe(h, xf, ridx, rw, rank, counts, moe_w_gate[j], moe_w_up[j], moe_w_down[j],
                     tm=min(768, t // 2), tg=min(256, t // 2))
    return xf.reshape(b, s, d).astype(x.dtype)
```

```python
import functools
import math

import jax
import jax.numpy as jnp
from jax import lax
from jax.experimental import pallas as pl
from jax.experimental.pallas import tpu as pltpu

F32 = jnp.float32
BF16 = jnp.bfloat16

NORM_EPS = 1e-6
CONV_K = 4
LANES = 128
HALO = 8

ATTN_HEADS = 4
ATTN_QK_DIM = 64
ATTN_V_DIM = 128
ATTN_WIDTH = 512
REL_BUCKETS = 32
REL_MAX_DIST = 128

GDN_HEADS = 6
GDN_DIM = 128
GDN_WIDTH = 768

SSM_HEADS = 12
SSM_HEAD_DIM = 64
SSM_GROUPS = 2
SSM_STATE = 128
SSM_WIDTH = 768
SSM_XBC = 1280

N_EXPERTS = 8
TOP_K = 2

GDN_COLS = 4 * GDN_WIDTH
SZ_BLOCK = 4
XBC_BLOCK = 3
ATTN_BLOCK0 = (5 * GDN_WIDTH + SSM_XBC) // LANES
N_MAIN = 5 * GDN_WIDTH + SSM_XBC + 3 * ATTN_WIDTH
BETA_LANE0, GA_LANE0, DT_LANE0 = 0, GDN_HEADS, 2 * GDN_HEADS

VMEM_LIMIT = 56 * 1024 * 1024


def _cparams(sem):
    return pltpu.CompilerParams(dimension_semantics=sem, vmem_limit_bytes=VMEM_LIMIT)


def _dot(a, b):
    return jnp.dot(a.astype(BF16), b.astype(BF16), preferred_element_type=F32)


def _dot_nt(a, b):
    return lax.dot_general(a.astype(BF16), b.astype(BF16), (((1,), (1,)), ((), ())),
                           preferred_element_type=F32)


def _dot_tn(a, b):
    return lax.dot_general(a.astype(BF16), b.astype(BF16), (((0,), (0,)), ((), ())),
                           preferred_element_type=F32)


def _dot_f32(a, b):
    return jnp.dot(a, b, preferred_element_type=F32, precision=lax.Precision.HIGHEST)


def _silu(x):
    return x / (1.0 + jnp.exp(-x))


def _sigmoid(x):
    return 1.0 / (1.0 + jnp.exp(-x))


def _softplus(x):
    return jnp.maximum(x, 0.0) + jnp.log(1.0 + jnp.exp(-jnp.abs(x)))


def _rms(x):
    return x * lax.rsqrt(jnp.mean(x * x, axis=-1, keepdims=True) + NORM_EPS)


def _tril(n, strict=False):
    r = lax.broadcasted_iota(jnp.int32, (n, n), 0)
    c = lax.broadcasted_iota(jnp.int32, (n, n), 1)
    return (r > c) if strict else (r >= c)


def _norm_inproj_kernel(x_ref, g_ref, w_ref, ws_ref, o_ref, os_ref, h_ref):
    @pl.when(pl.program_id(1) == 0)
    def _():
        h = (_rms(x_ref[...]) * g_ref[...]).astype(BF16)
        h_ref[...] = h
        os_ref[...] = jnp.dot(h, ws_ref[...], preferred_element_type=F32)

    o_ref[...] = jnp.dot(h_ref[...], w_ref[...], preferred_element_type=F32).astype(o_ref.dtype)


def norm_inproj(x, gain, w_main, w_small, *, tm=512, tn=3328):
    t, d = x.shape
    n = w_main.shape[1]
    return pl.pallas_call(
        _norm_inproj_kernel,
        grid=(t // tm, n // tn),
        in_specs=[
            pl.BlockSpec((tm, d), lambda i, j: (i, 0)),
            pl.BlockSpec((1, d), lambda i, j: (0, 0)),
            pl.BlockSpec((d, tn), lambda i, j: (0, j)),
            pl.BlockSpec((d, LANES), lambda i, j: (0, 0)),
        ],
        out_specs=[
            pl.BlockSpec((tm, tn), lambda i, j: (i, j)),
            pl.BlockSpec((tm, LANES), lambda i, j: (i, 0)),
        ],
        out_shape=[jax.ShapeDtypeStruct((t, n), BF16), jax.ShapeDtypeStruct((t, LANES), F32)],
        scratch_shapes=[pltpu.VMEM((tm, d), BF16)],
        compiler_params=_cparams(("parallel", "arbitrary")),
        name="norm_inproj",
    )(x, gain.reshape(1, d), w_main, w_small)


def _pair_rms(x, gain):
    lo = lax.broadcasted_iota(jnp.int32, (1, LANES), 1) < ATTN_QK_DIM
    sq = x * x
    s_lo = jnp.sum(jnp.where(lo, sq, 0.0), axis=-1, keepdims=True)
    s_hi = jnp.sum(jnp.where(lo, 0.0, sq), axis=-1, keepdims=True)
    ms = jnp.where(lo, s_lo, s_hi) * (1.0 / ATTN_QK_DIM)
    return x * lax.rsqrt(ms + NORM_EPS) * gain


LOG2_E = math.log2(math.e)
ONES_ROWS = 16


def _attn_kernel(lam_ref, qg_ref, kg_ref, sg_ref, bias_ref, q_ref, k_ref, v_ref, o_ref,
                 kn_ref, vt_ref, m_ref, acc_ref, sa_ref, sb_ref, *, tq, lam_init):
    qi = pl.program_id(2)
    lo = lax.broadcasted_iota(jnp.int32, (1, LANES), 1) < ATTN_QK_DIM
    nv = ATTN_V_DIM

    @pl.when(qi == 0)
    def _():
        def body(c, carry):
            rows = pl.ds(pl.multiple_of(c * tq, tq), tq)
            kn_ref[rows, :] = _pair_rms(k_ref[rows, :].astype(F32), kg_ref[...]).astype(BF16)
            vt_ref[c, 0:nv, :] = v_ref[rows, :].astype(F32).T.astype(BF16)
            vt_ref[c, nv:nv + ONES_ROWS, :] = jnp.ones((ONES_ROWS, tq), BF16)
            return carry
        lax.fori_loop(0, k_ref.shape[0] // tq, body, 0)

    q = _pair_rms(q_ref[...].astype(F32), qg_ref[...]) * (ATTN_QK_DIM ** -0.5 * LOG2_E)
    qz = (jnp.where(lo, q, 0.0).astype(BF16), jnp.where(lo, 0.0, q).astype(BF16))

    maps = range(2)

    def scores(kb, n, bias):
        k_blk = kn_ref[pl.ds(pl.multiple_of(kb * tq, tq), n * tq), :]
        s = [lax.dot_general(k_blk, qz[mp], (((1,), (1,)), ((), ())), preferred_element_type=F32) for mp in maps]
        return s if bias is None else [x + bias for x in s]

    def update(s, kb, n, first):
        vt = vt_ref[kb] if n == 1 else jnp.concatenate([vt_ref[kb + i] for i in range(n)], axis=-1)
        s_max = [jnp.max(x, axis=0, keepdims=True) for x in s]
        if first:
            m_new = s_max
        else:
            m_prev = [m_ref[mp] for mp in maps]
            m_new = [jnp.maximum(m_prev[mp], s_max[mp]) for mp in maps]
            alpha = [jnp.exp2(m_prev[mp] - m_new[mp]) for mp in maps]
        p = [jnp.exp2(s[mp] - m_new[mp]).astype(BF16) for mp in maps]
        pv = [jnp.dot(vt, p[mp], preferred_element_type=F32) for mp in maps]
        for mp in maps:
            acc_ref[mp] = pv[mp] if first else alpha[mp] * acc_ref[mp] + pv[mp]
            m_ref[mp] = m_new[mp]

    @pl.when(qi == 0)
    def _():
        update(scores(qi, 1, bias_ref[1]), qi, 1, True)

    n_far = jnp.maximum(qi - 1, 0)
    n_groups = n_far // 2
    bufs = (sa_ref, sb_ref)

    def fill(buf, k, near):
        s = scores(qi - 1, 2, bias_ref[...].reshape(2 * tq, tq)) if near else scores((k - 1) * 2, 2, None)
        for mp in maps:
            buf[mp] = s[mp]

    def drain(buf, k, near):
        update([buf[mp] for mp in maps], (qi - 1) if near else (k - 1) * 2, 2, near)

    def stage(k, cur, nxt, near=False):
        @pl.when(k < n_groups)
        def _():
            fill(nxt, k + 1, False)
            drain(cur, k, near)

        @pl.when(k == n_groups)
        def _():
            drain(cur, k, near)

    @pl.when(qi >= 1)
    def _():
        fill(bufs[0], 0, True)
        stage(0, bufs[0], bufs[1], near=True)

        def far(kk, carry):
            stage(2 * kk + 1, bufs[1], bufs[0])
            stage(2 * kk + 2, bufs[0], bufs[1])
            return carry
        lax.fori_loop(0, (n_groups + 1) // 2, far, 0)

    @pl.when(n_far - 2 * n_groups == 1)
    def _():
        update(scores(n_far - 1, 1, None), n_far - 1, 1, False)

    lam = lam_ref[...]
    lam_full = (jnp.exp(jnp.sum(lam[0:1] * lam[1:2], axis=-1, keepdims=True))
                - jnp.exp(jnp.sum(lam[2:3] * lam[3:4], axis=-1, keepdims=True)) + lam_init)
    a0, a1 = acc_ref[0], acc_ref[1]
    o = a0[0:nv] / a0[nv:nv + 1] - lam_full * (a1[0:nv] / a1[nv:nv + 1])
    o = o * lax.rsqrt(jnp.mean(o * o, axis=0, keepdims=True) + NORM_EPS)
    o_ref[...] = (o.T * sg_ref[...] * (1.0 - lam_init)).astype(o_ref.dtype)


def _t5_bucket(n):
    max_exact = REL_BUCKETS // 2
    nf = jnp.maximum(n, max_exact).astype(F32)
    large = max_exact + (jnp.log(nf / max_exact) / math.log(REL_MAX_DIST / max_exact)
                         * (REL_BUCKETS - max_exact)).astype(jnp.int32)
    return jnp.where(n < max_exact, n, jnp.minimum(large, REL_BUCKETS - 1))


def _bias_tiles(rel_bias, tq):
    assert tq >= REL_MAX_DIST
    table = rel_bias.astype(F32)
    i = jnp.arange(tq, dtype=jnp.int32)
    tiles = []
    for d in (1, 0):
        rel = d * tq + i[None, :] - i[:, None]
        bucket = _t5_bucket(jnp.maximum(rel, 0))
        shifted = (table - table[REL_BUCKETS - 1]).T
        hit = bucket[None, :, :, None] == jnp.arange(REL_BUCKETS, dtype=jnp.int32)
        b = jnp.sum(jnp.where(hit, shifted[:, None, None, :], 0.0), axis=-1)
        tiles.append(jnp.where((rel >= 0)[None], b * LOG2_E, -jnp.inf))
    return jnp.stack(tiles, axis=1)


def attention(proj3, q_gain, k_gain, lam, sub_gain, bias_tiles, layer_idx, *, tq=512):
    b, s, _ = proj3.shape
    h = ATTN_HEADS
    lam_init = 0.8 - 0.6 * math.exp(-0.3 * layer_idx)
    qg = jnp.tile(q_gain.astype(F32), 2).reshape(1, LANES)
    kg = jnp.tile(k_gain.astype(F32), 2).reshape(1, LANES)
    sg = sub_gain.astype(F32).reshape(1, LANES)
    kern = functools.partial(_attn_kernel, tq=tq, lam_init=lam_init)
    const = lambda shape: pl.BlockSpec(shape, lambda bi, hi, qi: (0,) * len(shape))
    return pl.pallas_call(
        kern,
        grid=(b, h, s // tq),
        in_specs=[
            const((4, ATTN_QK_DIM)), const((1, LANES)), const((1, LANES)), const((1, LANES)),
            pl.BlockSpec((None, 2, tq, tq), lambda bi, hi, qi: (hi, 0, 0, 0)),
            pl.BlockSpec((None, tq, LANES), lambda bi, hi, qi: (bi, qi, ATTN_BLOCK0 + hi)),
            pl.BlockSpec((None, s, LANES), lambda bi, hi, qi: (bi, 0, ATTN_BLOCK0 + h + hi)),
            pl.BlockSpec((None, s, LANES), lambda bi, hi, qi: (bi, 0, ATTN_BLOCK0 + 2 * h + hi)),
        ],
        out_specs=pl.BlockSpec((None, tq, LANES), lambda bi, hi, qi: (bi, qi, hi)),
        out_shape=jax.ShapeDtypeStruct((b, s, ATTN_WIDTH), BF16),
        scratch_shapes=[
            pltpu.VMEM((s, LANES), BF16),
            pltpu.VMEM((s // tq, ATTN_V_DIM + ONES_ROWS, tq), BF16),
            pltpu.VMEM((2, 1, tq), F32),
            pltpu.VMEM((2, ATTN_V_DIM + ONES_ROWS, tq), F32),
            pltpu.VMEM((2, 2 * tq, tq), F32),
            pltpu.VMEM((2, 2 * tq, tq), F32),
        ],
        compiler_params=_cparams(("parallel", "parallel", "arbitrary")),
        name="diff_attention",
    )(lam.astype(F32), qg, kg, sg, bias_tiles, proj3, proj3, proj3)


def _causal_conv(x, w_ref, buf_ref, first):
    rows = x.shape[0]

    @pl.when(first)
    def _():
        buf_ref[0:HALO, :] = jnp.zeros((HALO, x.shape[1]), F32)

    buf_ref[HALO:HALO + rows, :] = x
    y = w_ref[CONV_K - 1:CONV_K, :] * x
    for j in range(CONV_K - 1):
        off = HALO - (CONV_K - 1) + j
        y = y + w_ref[j:j + 1, :] * buf_ref[off:off + rows, :]
    buf_ref[0:HALO, :] = x[rows - HALO:, :]
    return y


def _unit_lower_inverses(mats):
    n = mats[0].shape[0]
    eye = (lax.broadcasted_iota(jnp.int32, (n, n), 0)
           == lax.broadcasted_iota(jnp.int32, (n, n), 1)).astype(F32)
    ps = [eye - a for a in mats]
    bs = [_dot(a, a) for a in mats]
    steps = int(math.log2(n)) - 1
    for i in range(steps):
        if i + 1 < steps:
            both = [_dot(jnp.concatenate([p, b], axis=0), b) for p, b in zip(ps, bs)]
            ps = [p + x[:n] for p, x in zip(ps, both)]
            bs = [x[n:] for x in both]
        else:
            ps = [p + _dot(p, b) for p, b in zip(ps, bs)]
    return ps


def _gdn_kernel(blk_ref, sm_ref, cw_ref, arow_ref, dtb_ref, og_ref, o_ref, buf_ref, st_ref, *, chunk):
    first = pl.program_id(1) == 0
    c = chunk
    d = GDN_DIM
    nq = 3 * GDN_WIDTH
    tile = blk_ref.shape[0]
    heads = range(GDN_HEADS)
    chunks = range(tile // c)

    @pl.when(first)
    def _():
        st_ref[...] = jnp.zeros(st_ref.shape, F32)

    qkv = _silu(_causal_conv(blk_ref[:, 0:nq].astype(F32), cw_ref, buf_ref, first))
    sm = sm_ref[...]
    beta = _sigmoid(sm)
    g = arow_ref[...] * _softplus(sm + dtb_ref[...])
    strict = _tril(c, strict=True)
    causal = _tril(c)
    r = lax.broadcasted_iota(jnp.int32, (tile, tile), 0)
    cc = lax.broadcasted_iota(jnp.int32, (tile, tile), 1)
    same_chunk_tril = jnp.logical_and(r >= cc, (r // c) == (cc // c))
    gc = _dot_f32(same_chunk_tril.astype(F32), g)
    gct = gc.T

    l2 = lambda x: x * lax.rsqrt(jnp.sum(x * x, axis=-1, keepdims=True) + NORM_EPS)
    qn = [l2(qkv[:, h * d:(h + 1) * d]) * (d ** -0.5) for h in heads]
    kn = [l2(qkv[:, GDN_WIDTH + h * d:GDN_WIDTH + (h + 1) * d]) for h in heads]

    pairs = [(ci, h) for ci in chunks for h in heads]
    prep = {}
    for ci, h in pairs:
        rows = slice(ci * c, (ci + 1) * c)
        q, k = qn[h][rows], kn[h][rows]
        v = qkv[rows, 2 * GDN_WIDTH + h * d:2 * GDN_WIDTH + (h + 1) * d]
        bcol = beta[rows, BETA_LANE0 + h:BETA_LANE0 + h + 1]
        col = gc[rows, GA_LANE0 + h:GA_LANE0 + h + 1]
        row = gct[GA_LANE0 + h:GA_LANE0 + h + 1, rows]
        last = gc[(ci + 1) * c - 1:(ci + 1) * c, GA_LANE0 + h:GA_LANE0 + h + 1]
        dec = jnp.exp(jnp.where(causal, col - row, -jnp.inf))
        ecol = jnp.exp(col)
        kb = k * bcol
        prep[ci, h] = dict(k=k, kb=kb, dec=dec, last=last,
                           lhs=jnp.concatenate([kb, q], axis=0),
                           rhs=jnp.concatenate([v * bcol, kb * ecol], axis=-1),
                           qd=q * ecol, kd=k * jnp.exp(last - col))
    for key in pairs:
        x = prep[key]
        both = _dot_nt(x["lhs"], x["k"])
        x["a"] = both[:c] * jnp.where(strict, x["dec"], 0.0)
        x["qk"] = both[c:] * x["dec"]
    t_inv = _unit_lower_inverses([prep[key]["a"] for key in pairs])
    for key, t in zip(pairs, t_inv):
        prep[key]["sol"] = _dot(t, prep[key]["rhs"])

    states = [st_ref[h] for h in heads]
    for ci in chunks:
        xs = [prep[ci, h] for h in heads]
        both = [_dot(jnp.concatenate([x["sol"][:, d:], x["qd"]], axis=0), st) for x, st in zip(xs, states)]
        v_new = [x["sol"][:, :d] - y[:c] for x, y in zip(xs, both)]
        inter = [_dot(x["qk"], vn) for x, vn in zip(xs, v_new)]
        upd = [_dot_tn(x["kd"], vn) for x, vn in zip(xs, v_new)]
        states = [st * jnp.exp(x["last"]) + u for st, x, u in zip(states, xs, upd)]
        rows = slice(ci * c, (ci + 1) * c)
        for h in heads:
            o = both[h][c:] + inter[h]
            gate = blk_ref[rows, nq + h * d:nq + (h + 1) * d].astype(F32)
            o_ref[rows, h * d:(h + 1) * d] = (_rms(o) * og_ref[...] * _silu(gate)).astype(o_ref.dtype)
    for h in heads:
        st_ref[h] = states[h]


def gdn(proj3, small3, conv_w, a_log, dt_bias, o_gain, *, chunk=64, tile=256):
    b, s, _ = proj3.shape
    tile = min(tile, s)
    pad = lambda v, lane0: jnp.zeros((1, LANES), F32).at[0, lane0:lane0 + v.shape[0]].set(v.astype(F32))
    arow = pad(-jnp.exp(a_log.astype(F32)), GA_LANE0)
    dtb = pad(dt_bias, GA_LANE0)
    const = lambda shape: pl.BlockSpec(shape, lambda bi, si: (0,) * len(shape))
    return pl.pallas_call(
        functools.partial(_gdn_kernel, chunk=chunk),
        grid=(b, s // tile),
        in_specs=[
            pl.BlockSpec((None, tile, GDN_COLS), lambda bi, si: (bi, si, 0)),
            pl.BlockSpec((None, tile, LANES), lambda bi, si: (bi, si, 0)),
            const((CONV_K, 3 * GDN_WIDTH)), const((1, LANES)), const((1, LANES)), const((1, GDN_DIM)),
        ],
        out_specs=pl.BlockSpec((None, tile, GDN_WIDTH), lambda bi, si: (bi, si, 0)),
        out_shape=jax.ShapeDtypeStruct((b, s, GDN_WIDTH), BF16),
        scratch_shapes=[
            pltpu.VMEM((HALO + tile, 3 * GDN_WIDTH), F32),
            pltpu.VMEM((GDN_HEADS, GDN_DIM, GDN_DIM), F32),
        ],
        compiler_params=_cparams(("parallel", "arbitrary")),
        name="gated_deltanet",
    )(proj3, small3, conv_w.astype(F32), arow, dtb, o_gain.astype(F32).reshape(1, GDN_DIM))


def _ssd_kernel(xbc_ref, z_ref, sm_ref, cw_ref, cb_ref, arow_ref, dtb_ref, dsk_ref, ng_ref, o_ref,
                buf_ref, st_ref, y_ref, *, chunk):
    first = pl.program_id(1) == 0
    c = chunk
    p2 = 2 * SSM_HEAD_DIM
    heads_per_group = SSM_HEADS // SSM_GROUPS
    gw = SSM_WIDTH // SSM_GROUPS

    @pl.when(first)
    def _():
        st_ref[...] = jnp.zeros(st_ref.shape, F32)

    xbc = _silu(_causal_conv(xbc_ref[...].astype(F32), cw_ref, buf_ref, first) + cb_ref[...])
    x = xbc[:, :SSM_WIDTH]
    sm = sm_ref[...]
    dt = _softplus(sm + dtb_ref[...])
    causal = _tril(c)
    acum = _dot_f32(causal.astype(F32), dt * arow_ref[...])
    acum_t = acum.T
    lo = lax.broadcasted_iota(jnp.int32, (1, p2), 1) < SSM_HEAD_DIM
    halves = lambda lane, arr: (arr[:, lane:lane + 1], arr[:, lane + 1:lane + 2])
    sel = lambda pair: jnp.where(lo, pair[0], pair[1])

    for pr in range(SSM_HEADS // 2):
        grp = (2 * pr) // heads_per_group
        lane = DT_LANE0 + 2 * pr
        bm = xbc[:, SSM_WIDTH + grp * SSM_STATE:SSM_WIDTH + (grp + 1) * SSM_STATE]
        cm = xbc[:, SSM_WIDTH + (SSM_GROUPS + grp) * SSM_STATE:SSM_WIDTH + (SSM_GROUPS + grp + 1) * SSM_STATE]
        cb = _dot_nt(cm, bm)
        xp = x[:, pr * p2:(pr + 1) * p2]
        cols = halves(lane, acum)
        lasts = halves(lane, acum[c - 1:c, :])
        xdt = xp * sel(halves(lane, dt))
        y_diag = []
        for hh in range(2):
            row = acum_t[lane + hh:lane + hh + 1, :]
            lmat = jnp.exp(jnp.where(causal, cols[hh] - row, -jnp.inf))
            y_diag.append(_dot(cb * lmat, xdt))
        prev = st_ref[pr]
        y_off = _dot(cm, prev) * sel((jnp.exp(cols[0]), jnp.exp(cols[1])))
        decay_in = sel((jnp.exp(lasts[0] - cols[0]), jnp.exp(lasts[1] - cols[1])))
        st_ref[pr] = prev * sel((jnp.exp(lasts[0]), jnp.exp(lasts[1]))) + _dot_tn(bm, xdt * decay_in)
        y_ref[:, pr * p2:(pr + 1) * p2] = jnp.where(lo, y_diag[0], y_diag[1]) + y_off

    y = (y_ref[...] + dsk_ref[...] * x) * _silu(z_ref[...].astype(F32))
    for grp in range(SSM_GROUPS):
        cols = slice(grp * gw, (grp + 1) * gw)
        o_ref[:, cols] = (_rms(y[:, cols]) * ng_ref[:, cols]).astype(o_ref.dtype)


def ssd(proj3, small3, conv_w, conv_b, a_log, dt_bias, d_skip, norm_gain, *, chunk=128):
    b, s, _ = proj3.shape
    pad = lambda v: jnp.zeros((1, LANES), F32).at[0, DT_LANE0:DT_LANE0 + SSM_HEADS].set(v.astype(F32))
    arow = pad(-jnp.exp(a_log.astype(F32)))
    dtb = pad(dt_bias)
    dsk = jnp.repeat(d_skip.astype(F32), SSM_HEAD_DIM).reshape(1, SSM_WIDTH)
    const = lambda shape: pl.BlockSpec(shape, lambda bi, si: (0,) * len(shape))
    return pl.pallas_call(
        functools.partial(_ssd_kernel, chunk=chunk),
        grid=(b, s // chunk),
        in_specs=[
            pl.BlockSpec((None, chunk, SSM_XBC), lambda bi, si: (bi, si, XBC_BLOCK)),
            pl.BlockSpec((None, chunk, SSM_WIDTH), lambda bi, si: (bi, si, SZ_BLOCK)),
            pl.BlockSpec((None, chunk, LANES), lambda bi, si: (bi, si, 0)),
            const((CONV_K, SSM_XBC)), const((1, SSM_XBC)), const((1, LANES)), const((1, LANES)),
            const((1, SSM_WIDTH)), const((1, SSM_WIDTH)),
        ],
        out_specs=pl.BlockSpec((None, chunk, SSM_WIDTH), lambda bi, si: (bi, si, 0)),
        out_shape=jax.ShapeDtypeStruct((b, s, SSM_WIDTH), BF16),
        scratch_shapes=[
            pltpu.VMEM((HALO + chunk, SSM_XBC), F32),
            pltpu.VMEM((SSM_HEADS // 2, SSM_STATE, 2 * SSM_HEAD_DIM), F32),
            pltpu.VMEM((chunk, SSM_WIDTH), F32),
        ],
        compiler_params=_cparams(("parallel", "arbitrary")),
        name="mamba2_ssd",
    )(proj3, proj3, small3, conv_w.astype(F32), conv_b.astype(F32).reshape(1, SSM_XBC), arow, dtb, dsk,
      norm_gain.astype(F32).reshape(1, SSM_WIDTH))


def _outproj_kernel(*refs, with_router):
    if with_router:
        (a_ref, g_ref, s_ref, wa_ref, wg_ref, ws_ref, x_ref, ng_ref, wr_ref,
         xo_ref, h_ref, ri_ref, rw_ref, rk_ref, cnt_ref, run_ref) = refs
    else:
        a_ref, g_ref, s_ref, wa_ref, wg_ref, ws_ref, x_ref, ng_ref, xo_ref, h_ref = refs
    y = (jnp.dot(a_ref[...], wa_ref[...], preferred_element_type=F32)
         + jnp.dot(g_ref[...], wg_ref[...], preferred_element_type=F32)
         + jnp.dot(s_ref[...], ws_ref[...], preferred_element_type=F32))
    xn = x_ref[...] + y
    xo_ref[...] = xn
    hf = _rms(xn) * ng_ref[...]
    h = hf.astype(BF16)
    h_ref[...] = hf.astype(h_ref.dtype)
    if with_router:
        tm = xn.shape[0]
        lane = lax.broadcasted_iota(jnp.int32, (1, LANES), 1)
        logits = jnp.where(lane < N_EXPERTS, jnp.dot(h, wr_ref[...], preferred_element_type=F32), -jnp.inf)
        v1 = jnp.max(logits, axis=-1, keepdims=True)
        i1 = jnp.min(jnp.where(logits == v1, lane, LANES), axis=-1, keepdims=True)
        rest = jnp.where(lane == i1, -jnp.inf, logits)
        v2 = jnp.max(rest, axis=-1, keepdims=True)
        i2 = jnp.min(jnp.where(rest == v2, lane, LANES), axis=-1, keepdims=True)
        e2 = jnp.exp(v2 - v1)
        ri_ref[...] = jnp.where(lane == 0, i1, i2)
        rw_ref[...] = jnp.where(lane == 0, 1.0 / (1.0 + e2), e2 / (1.0 + e2))

        @pl.when(pl.program_id(0) == 0)
        def _():
            run_ref[...] = jnp.zeros(run_ref.shape, F32)

        before = _tril(tm, strict=True).astype(BF16)
        run = run_ref[0:1, :]
        ranks = []
        for idx in (i1, i2):
            hit = lane == idx
            onehot = hit.astype(F32)
            earlier = jnp.dot(before, onehot.astype(BF16), preferred_element_type=F32) + run
            ranks.append(jnp.sum(jnp.where(hit, earlier, 0.0), axis=-1, keepdims=True))
            run = run + jnp.sum(onehot, axis=0, keepdims=True)
        rk_ref[...] = jnp.where(lane == 0, ranks[0], ranks[1]).astype(jnp.int32)
        run_ref[...] = jnp.broadcast_to(run, run_ref.shape)
        cnt_ref[...] = jnp.broadcast_to(run, cnt_ref.shape).astype(jnp.int32)


def outproj(attn_o, gdn_o, ssm_o, w_a, w_g, w_s, x, norm_gain, w_router=None, *, tm=512):
    t, d = x.shape
    with_router = w_router is not None
    row = lambda width: pl.BlockSpec((tm, width), lambda i: (i, 0))
    const = lambda shape: pl.BlockSpec(shape, lambda i: (0, 0))
    in_specs = [row(ATTN_WIDTH), row(GDN_WIDTH), row(SSM_WIDTH),
                const((ATTN_WIDTH, d)), const((GDN_WIDTH, d)), const((SSM_WIDTH, d)),
                row(d), const((1, d))]
    args = [attn_o, gdn_o, ssm_o, w_a, w_g, w_s, x, norm_gain.astype(F32).reshape(1, d)]
    out_specs = [row(d), row(d)]
    out_shape = [jax.ShapeDtypeStruct((t, d), F32), jax.ShapeDtypeStruct((t, d), F32 if with_router else BF16)]
    scratch = []
    if with_router:
        in_specs.append(const((d, LANES)))
        args.append(w_router)
        out_specs += [row(LANES), row(LANES), row(LANES), const((8, LANES))]
        out_shape += [jax.ShapeDtypeStruct((t, LANES), jnp.int32), jax.ShapeDtypeStruct((t, LANES), F32),
                      jax.ShapeDtypeStruct((t, LANES), jnp.int32), jax.ShapeDtypeStruct((8, LANES), jnp.int32)]
        scratch = [pltpu.VMEM((8, LANES), F32)]
    return pl.pallas_call(
        functools.partial(_outproj_kernel, with_router=with_router),
        grid=(t // tm,),
        in_specs=in_specs, out_specs=out_specs, out_shape=out_shape, scratch_shapes=scratch,
        compiler_params=_cparams(("arbitrary" if with_router else "parallel",)),
        name="outproj_router" if with_router else "outproj",
    )(*args)


def _ffn_kernel(h_ref, x_ref, wg_ref, wu_ref, wd_ref, o_ref):
    @pl.when(pl.program_id(1) == 0)
    def _():
        o_ref[...] = x_ref[...]

    h = h_ref[...]
    g = jnp.dot(h, wg_ref[...].astype(BF16), preferred_element_type=F32)
    u = jnp.dot(h, wu_ref[...].astype(BF16), preferred_element_type=F32)
    o_ref[...] += jnp.dot((_silu(g) * u).astype(BF16), wd_ref[...].astype(BF16), preferred_element_type=F32)


def ffn(h, x, wg, wu, wd, *, tm=1024, tf=256):
    t, d = x.shape
    f = wg.shape[1]
    return pl.pallas_call(
        _ffn_kernel,
        grid=(t // tm, f // tf),
        in_specs=[
            pl.BlockSpec((tm, d), lambda i, j: (i, 0)),
            pl.BlockSpec((tm, d), lambda i, j: (i, 0), pipeline_mode=pl.Buffered(1)),
            pl.BlockSpec((d, tf), lambda i, j: (0, j)),
            pl.BlockSpec((d, tf), lambda i, j: (0, j)),
            pl.BlockSpec((tf, d), lambda i, j: (j, 0)),
        ],
        out_specs=pl.BlockSpec((tm, d), lambda i, j: (i, 0)),
        out_shape=jax.ShapeDtypeStruct((t, d), F32),
        compiler_params=_cparams(("parallel", "arbitrary")),
        name="ffn_swiglu",
    )(h, x, wg, wu, wd)


def _moe_ffn_kernel(te_ref, np_ref, nt_ref, h_ref, wg_ref, wu_ref, wd_ref, o_ref, hb_ref, wgu_ref, wdb_ref,
                    *, sub):
    i = pl.program_id(0)
    j = pl.program_id(1)
    parts = o_ref.shape[0] // sub
    tf = wg_ref.shape[1]
    n_valid = np_ref[i]

    @pl.when(j == 0)
    def _():
        o_ref[...] = jnp.zeros(o_ref.shape, F32)

    for part in range(parts):
        rows = slice(part * sub, (part + 1) * sub)

        @pl.when(jnp.logical_and(n_valid > part, j == 0))
        def _(rows=rows):
            hb_ref[rows, :] = h_ref[rows, :].astype(BF16)

        @pl.when(n_valid > part)
        def _(rows=rows, part=part):
            if part == 0:
                wgu_ref[:, 0:tf] = wg_ref[...].astype(BF16)
                wgu_ref[:, tf:2 * tf] = wu_ref[...].astype(BF16)
                wdb_ref[...] = wd_ref[...].astype(BF16)
            h = hb_ref[rows, :]
            gu = jnp.dot(h, wgu_ref[...], preferred_element_type=F32)
            a = (_silu(gu[:, :tf]) * gu[:, tf:]).astype(BF16)
            o_ref[rows, :] += jnp.dot(a, wdb_ref[...], preferred_element_type=F32)


def moe_ffn(hs, tile_expert, tile_parts, n_tiles, wg, wu, wd, *, tm, sub, tf=256):
    npad, d = hs.shape
    f = wg.shape[2]
    nj = f // tf
    col = lambda i, j, nt: jnp.where(i < nt[0], j, nj - 1)
    grid_spec = pltpu.PrefetchScalarGridSpec(
        num_scalar_prefetch=3,
        grid=(npad // tm, f // tf),
        in_specs=[
            pl.BlockSpec((tm, d), lambda i, j, te, tp, nt: (jnp.minimum(i, nt[0] - 1), 0)),
            pl.BlockSpec((None, d, tf), lambda i, j, te, tp, nt: (te[i], 0, col(i, j, nt))),
            pl.BlockSpec((None, d, tf), lambda i, j, te, tp, nt: (te[i], 0, col(i, j, nt))),
            pl.BlockSpec((None, tf, d), lambda i, j, te, tp, nt: (te[i], col(i, j, nt), 0)),
        ],
        out_specs=pl.BlockSpec((tm, d), lambda i, j, te, tp, nt: (i, 0)),
        scratch_shapes=[pltpu.VMEM((tm, d), BF16), pltpu.VMEM((d, 2 * tf), BF16), pltpu.VMEM((tf, d), BF16)],
    )
    return pl.pallas_call(
        functools.partial(_moe_ffn_kernel, sub=sub),
        grid_spec=grid_spec,
        out_shape=jax.ShapeDtypeStruct((npad, d), F32),
        compiler_params=_cparams(("parallel", "arbitrary")),
        name="moe_grouped_swiglu",
    )(tile_expert, tile_parts, n_tiles, hs, wg, wu, wd)


DMA_UNROLL = 8


def _dispatch_kernel(pos_ref, pad_ref, h_ref, xs_ref, zero_ref, sem, *, tg, n_tok):
    i = pl.program_id(0)
    base = i * tg

    def copy(k, r):
        return pltpu.make_async_copy(h_ref.at[pl.ds(r, 1), :],
                                     xs_ref.at[pl.ds(pos_ref[k * n_tok + base + r], 1), :], sem)

    def start(r, carry):
        for k in range(TOP_K):
            copy(k, r).start()
        return carry
    lax.fori_loop(0, tg, start, 0, unroll=DMA_UNROLL)

    def wait(r, carry):
        for k in range(TOP_K):
            copy(k, r).wait()
        return carry
    lax.fori_loop(0, tg, wait, 0, unroll=DMA_UNROLL)

    @pl.when(i == pl.num_programs(0) - 1)
    def _():
        zero_ref[...] = jnp.zeros(zero_ref.shape, F32)

        def zero_row(r):
            return pltpu.make_async_copy(zero_ref.at[pl.ds(0, 1), :], xs_ref.at[pl.ds(r, 1), :], sem)

        for e in range(N_EXPERTS):
            lo, hi = pad_ref[e], pad_ref[N_EXPERTS + e]

            def zstart(r, carry):
                zero_row(r).start()
                return carry
            lax.fori_loop(lo, hi, zstart, 0)

            def zwait(r, carry):
                zero_row(r).wait()
                return carry
            lax.fori_loop(lo, hi, zwait, 0)

        for e in range(N_EXPERTS + 1):
            lo = pad_ref[2 * N_EXPERTS + e]
            n_blk = (pad_ref[3 * N_EXPERTS + 1 + e] - lo) // tg

            def zero_tile(c, lo=lo):
                rows = pl.ds(pl.multiple_of(lo + c * tg, tg), tg)
                return pltpu.make_async_copy(zero_ref, xs_ref.at[rows, :], sem)

            def tstart(c, carry, zero_tile=zero_tile):
                zero_tile(c).start()
                return carry
            lax.fori_loop(0, n_blk, tstart, 0)

            def twait(c, carry, zero_tile=zero_tile):
                zero_tile(c).wait()
                return carry
            lax.fori_loop(0, n_blk, twait, 0)


def dispatch_rows(h, pos, pad_ranges, npad, *, tg=256):
    t, d = h.shape
    grid_spec = pltpu.PrefetchScalarGridSpec(
        num_scalar_prefetch=2,
        grid=(t // tg,),
        in_specs=[pl.BlockSpec((tg, d), lambda i, pos, pad: (i, 0))],
        out_specs=pl.BlockSpec(memory_space=pl.ANY),
        scratch_shapes=[pltpu.VMEM((tg, d), F32), pltpu.SemaphoreType.DMA(())],
    )
    return pl.pallas_call(
        functools.partial(_dispatch_kernel, tg=tg, n_tok=t),
        grid_spec=grid_spec,
        out_shape=jax.ShapeDtypeStruct((npad, d), F32),
        compiler_params=_cparams(("arbitrary",)),
        name="dispatch_scatter",
    )(pos, pad_ranges, h)


def _combine_kernel(pos_ref, x_ref, rw_ref, ys_ref, o_ref, buf_ref, sem, *, tc, n_tok):
    base = pl.program_id(0) * tc

    def copy(k, r):
        return pltpu.make_async_copy(ys_ref.at[pl.ds(pos_ref[k * n_tok + base + r], 1), :],
                                     buf_ref.at[k, pl.ds(r, 1), :], sem)

    def start(r, carry):
        for k in range(TOP_K):
            copy(k, r).start()
        return carry
    lax.fori_loop(0, tc, start, 0, unroll=DMA_UNROLL)

    def wait(r, carry):
        for k in range(TOP_K):
            copy(k, r).wait()
        return carry
    lax.fori_loop(0, tc, wait, 0, unroll=DMA_UNROLL)
    rw = rw_ref[...]
    o_ref[...] = x_ref[...] + rw[:, 0:1] * buf_ref[0] + rw[:, 1:2] * buf_ref[1]


def combine_rows(x, rw, ys, pos, *, tc=256):
    t, d = x.shape
    grid_spec = pltpu.PrefetchScalarGridSpec(
        num_scalar_prefetch=1,
        grid=(t // tc,),
        in_specs=[pl.BlockSpec((tc, d), lambda i, pos: (i, 0)), pl.BlockSpec((tc, LANES), lambda i, pos: (i, 0)),
                  pl.BlockSpec(memory_space=pl.ANY)],
        out_specs=pl.BlockSpec((tc, d), lambda i, pos: (i, 0)),
        scratch_shapes=[pltpu.VMEM((TOP_K, tc, d), F32), pltpu.SemaphoreType.DMA(())],
    )
    return pl.pallas_call(
        functools.partial(_combine_kernel, tc=tc, n_tok=t),
        grid_spec=grid_spec,
        out_shape=jax.ShapeDtypeStruct((t, d), F32),
        compiler_params=_cparams(("arbitrary",)),
        name="combine_gather",
    )(pos, x, rw, ys)


def _routing_plan(ridx, rank, counts, tm, sub, npad):
    alloc = ((counts + tm - 1) // tm) * tm
    used = ((counts + sub - 1) // sub) * sub
    ends = jnp.cumsum(alloc)
    starts = ends - alloc
    hit = ridx[:, :, None] == jnp.arange(N_EXPERTS, dtype=jnp.int32)[None, None, :]
    pos = rank + jnp.sum(jnp.where(hit, starts[None, None, :], 0), axis=-1)
    n_tiles = (ends[-1] // tm).astype(jnp.int32).reshape(1)
    tile_row = jnp.arange(npad // tm, dtype=jnp.int32) * tm
    tile_expert = jnp.sum(jnp.minimum(tile_row, ends[-1] - tm)[:, None] >= ends[None, :], axis=-1).astype(jnp.int32)
    used_end = (starts + used)[tile_expert]
    tile_parts = jnp.where(tile_row < ends[-1], jnp.clip((used_end - tile_row) // sub, 0, tm // sub), 0)
    total = jnp.full((1,), npad, jnp.int32)
    pad_ranges = jnp.concatenate([starts + counts, starts + used,
                                  starts + used, ends[-1:], ends, total])
    return (pos.T.reshape(-1).astype(jnp.int32), pad_ranges.astype(jnp.int32), tile_expert,
            tile_parts.astype(jnp.int32), n_tiles)


def moe(h, x, ridx, rw, rank, counts, wg, wu, wd, *, tm=1024, sub=512, tf=256, tg=256):
    t, d = x.shape
    assert tm % sub == 0 and sub % tg == 0 and t % tg == 0
    npad = pl.cdiv(TOP_K * t + N_EXPERTS * tm, tm) * tm
    pos, pad_ranges, tile_expert, tile_parts, n_tiles = _routing_plan(
        ridx[:, :TOP_K], rank[:, :TOP_K], counts[0, :N_EXPERTS], tm, sub, npad)
    xs = dispatch_rows(h, pos, pad_ranges, npad, tg=tg)
    ys = moe_ffn(xs, tile_expert, tile_parts, n_tiles, wg, wu, wd, tm=tm, sub=sub, tf=tf)
    return combine_rows(x, rw, ys, pos, tc=tg)


def _split_w_in(w):
    sizes = (ATTN_WIDTH, ATTN_WIDTH, ATTN_WIDTH, 3 * GDN_WIDTH, GDN_WIDTH, GDN_HEADS, GDN_HEADS,
             SSM_WIDTH, SSM_XBC, SSM_HEADS)
    pieces, start = [], 0
    for size in sizes:
        pieces.append(w[:, start:start + size])
        start += size
    aq, ak, av, gqkv, ggate, gbeta, ga, sz, sxbc, sdt = pieces
    main = jnp.concatenate([gqkv, ggate, sz, sxbc, aq, ak, av], axis=1).astype(BF16)
    small = jnp.concatenate([gbeta, ga, sdt], axis=1)
    small = jnp.pad(small, ((0, 0), (0, LANES - small.shape[1]))).astype(BF16)
    return main, small


def kernel(x, rel_bias, mix_norm, w_in, attn_q_gain, attn_k_gain, attn_lambda, attn_sub_gain, gdn_conv_w, gdn_A_log, gdn_dt_bias, gdn_o_gain, ssm_conv_w, ssm_conv_b, ssm_A_log, ssm_dt_bias, ssm_D, ssm_norm_gain, w_out, ffn_norm, ffn_w_gate, ffn_w_up, ffn_w_down, moe_router, moe_w_gate, moe_w_up, moe_w_down):
    b, s, d = x.shape
    t = b * s
    depth = w_in.shape[0]
    tq = min(512, s)
    bias_tiles = _bias_tiles(rel_bias, tq)
    xf = x.reshape(t, d).astype(F32)
    for li in range(depth):
        w_main, w_small = _split_w_in(w_in[li])
        proj, small = norm_inproj(xf, mix_norm[li].astype(F32), w_main, w_small, tm=min(512, t))
        proj3 = proj.reshape(b, s, N_MAIN)
        small3 = small.reshape(b, s, LANES)
        attn_o = attention(proj3, attn_q_gain[li], attn_k_gain[li], attn_lambda[li], attn_sub_gain[li],
                           bias_tiles, li, tq=tq)
        gdn_o = gdn(proj3, small3, gdn_conv_w[li], gdn_A_log[li], gdn_dt_bias[li], gdn_o_gain[li])
        ssm_o = ssd(proj3, small3, ssm_conv_w[li], ssm_conv_b[li], ssm_A_log[li], ssm_dt_bias[li],
                    ssm_D[li], ssm_norm_gain[li])
        wo = w_out[li].astype(BF16)
        w_a, w_g, w_s = wo[:ATTN_WIDTH], wo[ATTN_WIDTH:ATTN_WIDTH + GDN_WIDTH], wo[ATTN_WIDTH + GDN_WIDTH:]
        mix = (attn_o.reshape(t, ATTN_WIDTH), gdn_o.reshape(t, GDN_WIDTH), ssm_o.reshape(t, SSM_WIDTH))
        j = li // 2
        if li % 2 == 0:
            xf, h = outproj(*mix, w_a, w_g, w_s, xf, ffn_norm[li], tm=min(512, t))
            xf = ffn(h, xf, ffn_w_gate[j], ffn_w_up[j], ffn_w_down[j], tm=min(1024, t))
        else:
            w_r = jnp.pad(moe_router[j], ((0, 0), (0, LANES - N_EXPERTS))).astype(BF16)
            xf, h, ridx, rw, rank, counts = outproj(*mix, w_a, w_g, w_s, xf, ffn_norm[li], w_r, tm=min(512, t))
            tm = min(1024, t // 2)
            xf = moe(h, xf, ridx, rw, rank, counts, moe_w_gate[j], moe_w_up[j], moe_w_down[j],
                     tm=tm, sub=tm // 2, tg=min(256, tm // 2))
    return xf.reshape(b, s, d).astype(x.dtype)
```

```python
import functools
import math

import jax
import jax.numpy as jnp
from jax import lax
from jax.experimental import pallas as pl
from jax.experimental.pallas import tpu as pltpu

F32 = jnp.float32
BF16 = jnp.bfloat16

NORM_EPS = 1e-6
CONV_K = 4
LANES = 128
HALO = 8

ATTN_HEADS = 4
ATTN_QK_DIM = 64
ATTN_V_DIM = 128
ATTN_WIDTH = 512
REL_BUCKETS = 32
REL_MAX_DIST = 128

GDN_HEADS = 6
GDN_DIM = 128
GDN_WIDTH = 768

SSM_HEADS = 12
SSM_HEAD_DIM = 64
SSM_GROUPS = 2
SSM_STATE = 128
SSM_WIDTH = 768
SSM_XBC = 1280

N_EXPERTS = 8
TOP_K = 2

GDN_COLS = 4 * GDN_WIDTH
SZ_BLOCK = 4
XBC_BLOCK = 3
ATTN_BLOCK0 = (5 * GDN_WIDTH + SSM_XBC) // LANES
N_MAIN = 5 * GDN_WIDTH + SSM_XBC + 3 * ATTN_WIDTH
BETA_LANE0, GA_LANE0, DT_LANE0 = 0, GDN_HEADS, 2 * GDN_HEADS

VMEM_LIMIT = 56 * 1024 * 1024


def _cparams(sem):
    return pltpu.CompilerParams(dimension_semantics=sem, vmem_limit_bytes=VMEM_LIMIT)


def _dot(a, b):
    return jnp.dot(a.astype(BF16), b.astype(BF16), preferred_element_type=F32)


def _dot_nt(a, b):
    return lax.dot_general(a.astype(BF16), b.astype(BF16), (((1,), (1,)), ((), ())),
                           preferred_element_type=F32)


def _dot_tn(a, b):
    return lax.dot_general(a.astype(BF16), b.astype(BF16), (((0,), (0,)), ((), ())),
                           preferred_element_type=F32)


def _dot_f32(a, b):
    return jnp.dot(a, b, preferred_element_type=F32, precision=lax.Precision.HIGHEST)


def _sigmoid(x):
    return 0.5 * jnp.tanh(0.5 * x) + 0.5


def _silu(x):
    return x * _sigmoid(x)


def _softplus(x):
    return jnp.maximum(x, 0.0) + jnp.log(1.0 + jnp.exp(-jnp.abs(x)))


def _rms(x):
    return x * lax.rsqrt(jnp.mean(x * x, axis=-1, keepdims=True) + NORM_EPS)


def _tril(n, strict=False):
    r = lax.broadcasted_iota(jnp.int32, (n, n), 0)
    c = lax.broadcasted_iota(jnp.int32, (n, n), 1)
    return (r > c) if strict else (r >= c)


def _norm_inproj_kernel(x_ref, g_ref, w_ref, ws_ref, o_ref, os_ref, h_ref):
    @pl.when(pl.program_id(1) == 0)
    def _():
        h = (_rms(x_ref[...]) * g_ref[...]).astype(BF16)
        h_ref[...] = h
        os_ref[...] = jnp.dot(h, ws_ref[...], preferred_element_type=F32)

    o_ref[...] = jnp.dot(h_ref[...], w_ref[...], preferred_element_type=F32).astype(o_ref.dtype)


def norm_inproj(x, gain, w_main, w_small, *, tm=512, tn=3328):
    t, d = x.shape
    n = w_main.shape[1]
    return pl.pallas_call(
        _norm_inproj_kernel,
        grid=(t // tm, n // tn),
        in_specs=[
            pl.BlockSpec((tm, d), lambda i, j: (i, 0)),
            pl.BlockSpec((1, d), lambda i, j: (0, 0)),
            pl.BlockSpec((d, tn), lambda i, j: (0, j)),
            pl.BlockSpec((d, LANES), lambda i, j: (0, 0)),
        ],
        out_specs=[
            pl.BlockSpec((tm, tn), lambda i, j: (i, j)),
            pl.BlockSpec((tm, LANES), lambda i, j: (i, 0)),
        ],
        out_shape=[jax.ShapeDtypeStruct((t, n), BF16), jax.ShapeDtypeStruct((t, LANES), F32)],
        scratch_shapes=[pltpu.VMEM((tm, d), BF16)],
        compiler_params=_cparams(("parallel", "arbitrary")),
        name="norm_inproj",
    )(x, gain.reshape(1, d), w_main, w_small)


def _pair_rms(x, gain):
    lo = lax.broadcasted_iota(jnp.int32, (1, LANES), 1) < ATTN_QK_DIM
    sq = x * x
    s_lo = jnp.sum(jnp.where(lo, sq, 0.0), axis=-1, keepdims=True)
    s_hi = jnp.sum(jnp.where(lo, 0.0, sq), axis=-1, keepdims=True)
    ms = jnp.where(lo, s_lo, s_hi) * (1.0 / ATTN_QK_DIM)
    return x * lax.rsqrt(ms + NORM_EPS) * gain


LOG2_E = math.log2(math.e)
ONES_ROWS = 16


def _attn_kernel(lam_ref, qg_ref, kg_ref, sg_ref, bias_ref, q_ref, k_ref, v_ref, o_ref,
                 kn_ref, vt_ref, m_ref, acc_ref, sa_ref, sb_ref, *, tq, lam_init):
    qi = pl.program_id(2)
    lo = lax.broadcasted_iota(jnp.int32, (1, LANES), 1) < ATTN_QK_DIM
    nv = ATTN_V_DIM

    @pl.when(qi == 0)
    def _():
        def body(c, carry):
            rows = pl.ds(pl.multiple_of(c * tq, tq), tq)
            kn_ref[rows, :] = _pair_rms(k_ref[rows, :].astype(F32), kg_ref[...]).astype(BF16)
            vt_ref[c, 0:nv, :] = v_ref[rows, :].astype(F32).T.astype(BF16)
            vt_ref[c, nv:nv + ONES_ROWS, :] = jnp.ones((ONES_ROWS, tq), BF16)
            return carry
        lax.fori_loop(0, k_ref.shape[0] // tq, body, 0)

    q = _pair_rms(q_ref[...].astype(F32), qg_ref[...]) * (ATTN_QK_DIM ** -0.5 * LOG2_E)
    qz = (jnp.where(lo, q, 0.0).astype(BF16), jnp.where(lo, 0.0, q).astype(BF16))

    maps = range(2)

    def scores(kb, n, bias):
        k_blk = kn_ref[pl.ds(pl.multiple_of(kb * tq, tq), n * tq), :]
        s = [lax.dot_general(k_blk, qz[mp], (((1,), (1,)), ((), ())), preferred_element_type=F32) for mp in maps]
        return s if bias is None else [x + bias for x in s]

    def update(s, kb, n, first):
        vt = vt_ref[kb] if n == 1 else jnp.concatenate([vt_ref[kb + i] for i in range(n)], axis=-1)
        s_max = [jnp.max(x, axis=0, keepdims=True) for x in s]
        if first:
            m_new = s_max
        else:
            m_prev = [m_ref[mp] for mp in maps]
            m_new = [jnp.maximum(m_prev[mp], s_max[mp]) for mp in maps]
            alpha = [jnp.exp2(m_prev[mp] - m_new[mp]) for mp in maps]
        p = [jnp.exp2(s[mp] - m_new[mp]).astype(BF16) for mp in maps]
        pv = [jnp.dot(vt, p[mp], preferred_element_type=F32) for mp in maps]
        for mp in maps:
            acc_ref[mp] = pv[mp] if first else alpha[mp] * acc_ref[mp] + pv[mp]
            m_ref[mp] = m_new[mp]

    @pl.when(qi == 0)
    def _():
        update(scores(qi, 1, bias_ref[1]), qi, 1, True)

    n_far = jnp.maximum(qi - 1, 0)
    n_groups = n_far // 2
    bufs = (sa_ref, sb_ref)

    def fill(buf, k, near):
        s = scores(qi - 1, 2, bias_ref[...].reshape(2 * tq, tq)) if near else scores((k - 1) * 2, 2, None)
        for mp in maps:
            buf[mp] = s[mp]

    def drain(buf, k, near):
        update([buf[mp] for mp in maps], (qi - 1) if near else (k - 1) * 2, 2, near)

    def stage(k, cur, nxt, near=False):
        @pl.when(k < n_groups)
        def _():
            fill(nxt, k + 1, False)
            drain(cur, k, near)

        @pl.when(k == n_groups)
        def _():
            drain(cur, k, near)

    @pl.when(qi >= 1)
    def _():
        fill(bufs[0], 0, True)
        stage(0, bufs[0], bufs[1], near=True)

        def far(kk, carry):
            stage(2 * kk + 1, bufs[1], bufs[0])
            stage(2 * kk + 2, bufs[0], bufs[1])
            return carry
        lax.fori_loop(0, (n_groups + 1) // 2, far, 0)

    @pl.when(n_far - 2 * n_groups == 1)
    def _():
        update(scores(n_far - 1, 1, None), n_far - 1, 1, False)

    lam = lam_ref[...]
    lam_full = (jnp.exp(jnp.sum(lam[0:1] * lam[1:2], axis=-1, keepdims=True))
                - jnp.exp(jnp.sum(lam[2:3] * lam[3:4], axis=-1, keepdims=True)) + lam_init)
    a0, a1 = acc_ref[0], acc_ref[1]
    o = a0[0:nv] / a0[nv:nv + 1] - lam_full * (a1[0:nv] / a1[nv:nv + 1])
    o = o * lax.rsqrt(jnp.mean(o * o, axis=0, keepdims=True) + NORM_EPS)
    o_ref[...] = (o.T * sg_ref[...] * (1.0 - lam_init)).astype(o_ref.dtype)


def _t5_bucket(n):
    max_exact = REL_BUCKETS // 2
    nf = jnp.maximum(n, max_exact).astype(F32)
    large = max_exact + (jnp.log(nf / max_exact) / math.log(REL_MAX_DIST / max_exact)
                         * (REL_BUCKETS - max_exact)).astype(jnp.int32)
    return jnp.where(n < max_exact, n, jnp.minimum(large, REL_BUCKETS - 1))


def _bias_tiles(rel_bias, tq):
    assert tq >= REL_MAX_DIST
    table = rel_bias.astype(F32)
    heads = table.shape[1]
    n = 3 * tq
    slot = jnp.arange(n, dtype=jnp.int32)
    rel = tq + jnp.where(slot < tq, slot, slot - n)
    bucket = _t5_bucket(jnp.maximum(rel, 0))
    shifted = (table - table[REL_BUCKETS - 1]).T
    hit = bucket[None, :, None] == jnp.arange(REL_BUCKETS, dtype=jnp.int32)
    vals = jnp.sum(jnp.where(hit, shifted[:, None, :], 0.0), axis=-1)
    vals = jnp.where((rel >= 0)[None], vals * LOG2_E, -jnp.inf)
    rows = 2 * tq
    skew = jnp.tile(vals, (1, rows))[:, :rows * (n - 1)].reshape(heads, rows, n - 1)
    return skew[:, :, :tq].reshape(heads, 2, tq, tq)


def attention(proj3, q_gain, k_gain, lam, sub_gain, bias_tiles, layer_idx, *, tq=512):
    b, s, _ = proj3.shape
    h = ATTN_HEADS
    lam_init = 0.8 - 0.6 * math.exp(-0.3 * layer_idx)
    qg = jnp.tile(q_gain.astype(F32), 2).reshape(1, LANES)
    kg = jnp.tile(k_gain.astype(F32), 2).reshape(1, LANES)
    sg = sub_gain.astype(F32).reshape(1, LANES)
    kern = functools.partial(_attn_kernel, tq=tq, lam_init=lam_init)
    const = lambda shape: pl.BlockSpec(shape, lambda bi, hi, qi: (0,) * len(shape))
    return pl.pallas_call(
        kern,
        grid=(b, h, s // tq),
        in_specs=[
            const((4, ATTN_QK_DIM)), const((1, LANES)), const((1, LANES)), const((1, LANES)),
            pl.BlockSpec((None, 2, tq, tq), lambda bi, hi, qi: (hi, 0, 0, 0)),
            pl.BlockSpec((None, tq, LANES), lambda bi, hi, qi: (bi, qi, ATTN_BLOCK0 + hi)),
            pl.BlockSpec((None, s, LANES), lambda bi, hi, qi: (bi, 0, ATTN_BLOCK0 + h + hi)),
            pl.BlockSpec((None, s, LANES), lambda bi, hi, qi: (bi, 0, ATTN_BLOCK0 + 2 * h + hi)),
        ],
        out_specs=pl.BlockSpec((None, tq, LANES), lambda bi, hi, qi: (bi, qi, hi)),
        out_shape=jax.ShapeDtypeStruct((b, s, ATTN_WIDTH), BF16),
        scratch_shapes=[
            pltpu.VMEM((s, LANES), BF16),
            pltpu.VMEM((s // tq, ATTN_V_DIM + ONES_ROWS, tq), BF16),
            pltpu.VMEM((2, 1, tq), F32),
            pltpu.VMEM((2, ATTN_V_DIM + ONES_ROWS, tq), F32),
            pltpu.VMEM((2, 2 * tq, tq), F32),
            pltpu.VMEM((2, 2 * tq, tq), F32),
        ],
        compiler_params=_cparams(("parallel", "parallel", "arbitrary")),
        name="diff_attention",
    )(lam.astype(F32), qg, kg, sg, bias_tiles, proj3, proj3, proj3)


def _causal_conv(x, w_ref, buf_ref, first):
    rows = x.shape[0]

    @pl.when(first)
    def _():
        buf_ref[0:HALO, :] = jnp.zeros((HALO, x.shape[1]), F32)

    buf_ref[HALO:HALO + rows, :] = x
    y = w_ref[CONV_K - 1:CONV_K, :] * x
    for j in range(CONV_K - 1):
        off = HALO - (CONV_K - 1) + j
        y = y + w_ref[j:j + 1, :] * buf_ref[off:off + rows, :]
    buf_ref[0:HALO, :] = x[rows - HALO:, :]
    return y


def _unit_lower_inverses(mats):
    n = mats[0].shape[0]
    eye = (lax.broadcasted_iota(jnp.int32, (n, n), 0)
           == lax.broadcasted_iota(jnp.int32, (n, n), 1)).astype(F32)
    ps = [eye - a for a in mats]
    bs = [_dot(a, a) for a in mats]
    steps = int(math.log2(n)) - 1
    for i in range(steps):
        if i + 1 < steps:
            both = [_dot(jnp.concatenate([p, b], axis=0), b) for p, b in zip(ps, bs)]
            ps = [p + x[:n] for p, x in zip(ps, both)]
            bs = [x[n:] for x in both]
        else:
            ps = [p + _dot(p, b) for p, b in zip(ps, bs)]
    return ps


def _gdn_kernel(blk_ref, sm_ref, cw_ref, arow_ref, dtb_ref, og_ref, o_ref, buf_ref, st_ref, *, chunk):
    first = pl.program_id(1) == 0
    c = chunk
    d = GDN_DIM
    nq = 3 * GDN_WIDTH
    tile = blk_ref.shape[0]
    heads = range(GDN_HEADS)
    chunks = range(tile // c)

    @pl.when(first)
    def _():
        st_ref[...] = jnp.zeros(st_ref.shape, F32)

    qkv = _silu(_causal_conv(blk_ref[:, 0:nq].astype(F32), cw_ref, buf_ref, first))
    sm = sm_ref[...]
    beta = _sigmoid(sm)
    g = arow_ref[...] * _softplus(sm + dtb_ref[...])
    strict = _tril(c, strict=True)
    causal = _tril(c)
    r = lax.broadcasted_iota(jnp.int32, (tile, tile), 0)
    cc = lax.broadcasted_iota(jnp.int32, (tile, tile), 1)
    same_chunk_tril = jnp.logical_and(r >= cc, (r // c) == (cc // c))
    gc = _dot_f32(same_chunk_tril.astype(F32), g)
    gct = gc.T

    l2 = lambda x: x * lax.rsqrt(jnp.sum(x * x, axis=-1, keepdims=True) + NORM_EPS)
    qn = [l2(qkv[:, h * d:(h + 1) * d]) * (d ** -0.5) for h in heads]
    kn = [l2(qkv[:, GDN_WIDTH + h * d:GDN_WIDTH + (h + 1) * d]) for h in heads]

    pairs = [(ci, h) for ci in chunks for h in heads]
    prep = {}
    for ci, h in pairs:
        rows = slice(ci * c, (ci + 1) * c)
        q, k = qn[h][rows], kn[h][rows]
        v = qkv[rows, 2 * GDN_WIDTH + h * d:2 * GDN_WIDTH + (h + 1) * d]
        bcol = beta[rows, BETA_LANE0 + h:BETA_LANE0 + h + 1]
        col = gc[rows, GA_LANE0 + h:GA_LANE0 + h + 1]
        row = gct[GA_LANE0 + h:GA_LANE0 + h + 1, rows]
        last = gc[(ci + 1) * c - 1:(ci + 1) * c, GA_LANE0 + h:GA_LANE0 + h + 1]
        dec = jnp.exp(jnp.where(causal, col - row, -jnp.inf))
        ecol = jnp.exp(col)
        kb = k * bcol
        prep[ci, h] = dict(k=k, kb=kb, dec=dec, last=last,
                           lhs=jnp.concatenate([kb, q], axis=0),
                           rhs=jnp.concatenate([v * bcol, kb * ecol], axis=-1),
                           qd=q * ecol, kd=k * jnp.exp(last - col))
    for key in pairs:
        x = prep[key]
        both = _dot_nt(x["lhs"], x["k"])
        x["a"] = both[:c] * jnp.where(strict, x["dec"], 0.0)
        x["qk"] = both[c:] * x["dec"]
    t_inv = _unit_lower_inverses([prep[key]["a"] for key in pairs])
    for key, t in zip(pairs, t_inv):
        prep[key]["sol"] = _dot(t, prep[key]["rhs"])

    states = [st_ref[h] for h in heads]
    for ci in chunks:
        xs = [prep[ci, h] for h in heads]
        both = [_dot(jnp.concatenate([x["sol"][:, d:], x["qd"]], axis=0), st) for x, st in zip(xs, states)]
        v_new = [x["sol"][:, :d] - y[:c] for x, y in zip(xs, both)]
        inter = [_dot(x["qk"], vn) for x, vn in zip(xs, v_new)]
        upd = [_dot_tn(x["kd"], vn) for x, vn in zip(xs, v_new)]
        states = [st * jnp.exp(x["last"]) + u for st, x, u in zip(states, xs, upd)]
        rows = slice(ci * c, (ci + 1) * c)
        for h in heads:
            o = both[h][c:] + inter[h]
            gate = blk_ref[rows, nq + h * d:nq + (h + 1) * d].astype(F32)
            o_ref[rows, h * d:(h + 1) * d] = (_rms(o) * og_ref[...] * _silu(gate)).astype(o_ref.dtype)
    for h in heads:
        st_ref[h] = states[h]


def gdn(proj3, small3, conv_w, a_log, dt_bias, o_gain, *, chunk=64, tile=256):
    b, s, _ = proj3.shape
    tile = min(tile, s)
    pad = lambda v, lane0: jnp.zeros((1, LANES), F32).at[0, lane0:lane0 + v.shape[0]].set(v.astype(F32))
    arow = pad(-jnp.exp(a_log.astype(F32)), GA_LANE0)
    dtb = pad(dt_bias, GA_LANE0)
    const = lambda shape: pl.BlockSpec(shape, lambda bi, si: (0,) * len(shape))
    return pl.pallas_call(
        functools.partial(_gdn_kernel, chunk=chunk),
        grid=(b, s // tile),
        in_specs=[
            pl.BlockSpec((None, tile, GDN_COLS), lambda bi, si: (bi, si, 0)),
            pl.BlockSpec((None, tile, LANES), lambda bi, si: (bi, si, 0)),
            const((CONV_K, 3 * GDN_WIDTH)), const((1, LANES)), const((1, LANES)), const((1, GDN_DIM)),
        ],
        out_specs=pl.BlockSpec((None, tile, GDN_WIDTH), lambda bi, si: (bi, si, 0)),
        out_shape=jax.ShapeDtypeStruct((b, s, GDN_WIDTH), BF16),
        scratch_shapes=[
            pltpu.VMEM((HALO + tile, 3 * GDN_WIDTH), F32),
            pltpu.VMEM((GDN_HEADS, GDN_DIM, GDN_DIM), F32),
        ],
        compiler_params=_cparams(("parallel", "arbitrary")),
        name="gated_deltanet",
    )(proj3, small3, conv_w.astype(F32), arow, dtb, o_gain.astype(F32).reshape(1, GDN_DIM))


def _ssd_kernel(xbc_ref, z_ref, sm_ref, cw_ref, cb_ref, arow_ref, dtb_ref, dsk_ref, ng_ref, o_ref,
                buf_ref, st_ref, y_ref, *, chunk):
    first = pl.program_id(1) == 0
    c = chunk
    p2 = 2 * SSM_HEAD_DIM
    heads_per_group = SSM_HEADS // SSM_GROUPS
    gw = SSM_WIDTH // SSM_GROUPS

    @pl.when(first)
    def _():
        st_ref[...] = jnp.zeros(st_ref.shape, F32)

    tile = xbc_ref.shape[0]
    chunks = range(tile // c)
    pairs = range(SSM_HEADS // 2)
    xbc = _silu(_causal_conv(xbc_ref[...].astype(F32), cw_ref, buf_ref, first) + cb_ref[...])
    x = xbc[:, :SSM_WIDTH]
    sm = sm_ref[...]
    dt = _softplus(sm + dtb_ref[...])
    causal = _tril(c)
    r = lax.broadcasted_iota(jnp.int32, (tile, tile), 0)
    cc = lax.broadcasted_iota(jnp.int32, (tile, tile), 1)
    same_chunk_tril = jnp.logical_and(r >= cc, (r // c) == (cc // c))
    acum = _dot_f32(same_chunk_tril.astype(F32), dt * arow_ref[...])
    acum_t = acum.T
    lo = lax.broadcasted_iota(jnp.int32, (1, p2), 1) < SSM_HEAD_DIM
    halves = lambda lane, arr: (arr[:, lane:lane + 1], arr[:, lane + 1:lane + 2])
    sel = lambda pair: jnp.where(lo, pair[0], pair[1])

    bms, cms, cbs = {}, {}, {}
    for ci in chunks:
        rows = slice(ci * c, (ci + 1) * c)
        for grp in range(SSM_GROUPS):
            bms[ci, grp] = xbc[rows, SSM_WIDTH + grp * SSM_STATE:SSM_WIDTH + (grp + 1) * SSM_STATE]
            cms[ci, grp] = xbc[rows, SSM_WIDTH + (SSM_GROUPS + grp) * SSM_STATE:
                               SSM_WIDTH + (SSM_GROUPS + grp + 1) * SSM_STATE]
            cbs[ci, grp] = _dot_nt(cms[ci, grp], bms[ci, grp])
    units = [(ci, pr) for ci in chunks for pr in pairs]
    prep = {}
    for ci, pr in units:
        rows = slice(ci * c, (ci + 1) * c)
        grp = (2 * pr) // heads_per_group
        lane = DT_LANE0 + 2 * pr
        cols = halves(lane, acum[rows])
        lasts = halves(lane, acum[(ci + 1) * c - 1:(ci + 1) * c, :])
        xdt = x[rows, pr * p2:(pr + 1) * p2] * sel(halves(lane, dt[rows]))
        lmats = [jnp.exp(jnp.where(causal, cols[hh] - acum_t[lane + hh:lane + hh + 1, rows], -jnp.inf))
                 for hh in range(2)]
        prep[ci, pr] = dict(
            grp=grp, xdt=xdt, lhs=[cbs[ci, grp] * lm for lm in lmats],
            out_scale=sel((jnp.exp(cols[0]), jnp.exp(cols[1]))),
            xdt_in=xdt * sel((jnp.exp(lasts[0] - cols[0]), jnp.exp(lasts[1] - cols[1]))),
            keep=sel((jnp.exp(lasts[0]), jnp.exp(lasts[1]))))
    for key in units:
        u = prep[key]
        u["y_diag"] = jnp.where(lo, _dot(u["lhs"][0], u["xdt"]), _dot(u["lhs"][1], u["xdt"]))

    states = [st_ref[pr] for pr in pairs]
    for ci in chunks:
        rows = slice(ci * c, (ci + 1) * c)
        us = [prep[ci, pr] for pr in pairs]
        y_off = [_dot(cms[ci, u["grp"]], st) * u["out_scale"] for u, st in zip(us, states)]
        upd = [_dot_tn(bms[ci, u["grp"]], u["xdt_in"]) for u in us]
        states = [st * u["keep"] + d for st, u, d in zip(states, us, upd)]
        for pr in pairs:
            y_ref[rows, pr * p2:(pr + 1) * p2] = us[pr]["y_diag"] + y_off[pr]
    for pr in pairs:
        st_ref[pr] = states[pr]

    y = (y_ref[...] + dsk_ref[...] * x) * _silu(z_ref[...].astype(F32))
    for grp in range(SSM_GROUPS):
        cols = slice(grp * gw, (grp + 1) * gw)
        o_ref[:, cols] = (_rms(y[:, cols]) * ng_ref[:, cols]).astype(o_ref.dtype)


def ssd(proj3, small3, conv_w, conv_b, a_log, dt_bias, d_skip, norm_gain, *, chunk=128, tile=256):
    b, s, _ = proj3.shape
    tile = min(tile, s)
    pad = lambda v: jnp.zeros((1, LANES), F32).at[0, DT_LANE0:DT_LANE0 + SSM_HEADS].set(v.astype(F32))
    arow = pad(-jnp.exp(a_log.astype(F32)))
    dtb = pad(dt_bias)
    dsk = jnp.repeat(d_skip.astype(F32), SSM_HEAD_DIM).reshape(1, SSM_WIDTH)
    const = lambda shape: pl.BlockSpec(shape, lambda bi, si: (0,) * len(shape))
    return pl.pallas_call(
        functools.partial(_ssd_kernel, chunk=chunk),
        grid=(b, s // tile),
        in_specs=[
            pl.BlockSpec((None, tile, SSM_XBC), lambda bi, si: (bi, si, XBC_BLOCK)),
            pl.BlockSpec((None, tile, SSM_WIDTH), lambda bi, si: (bi, si, SZ_BLOCK)),
            pl.BlockSpec((None, tile, LANES), lambda bi, si: (bi, si, 0)),
            const((CONV_K, SSM_XBC)), const((1, SSM_XBC)), const((1, LANES)), const((1, LANES)),
            const((1, SSM_WIDTH)), const((1, SSM_WIDTH)),
        ],
        out_specs=pl.BlockSpec((None, tile, SSM_WIDTH), lambda bi, si: (bi, si, 0)),
        out_shape=jax.ShapeDtypeStruct((b, s, SSM_WIDTH), BF16),
        scratch_shapes=[
            pltpu.VMEM((HALO + tile, SSM_XBC), F32),
            pltpu.VMEM((SSM_HEADS // 2, SSM_STATE, 2 * SSM_HEAD_DIM), F32),
            pltpu.VMEM((tile, SSM_WIDTH), F32),
        ],
        compiler_params=_cparams(("parallel", "arbitrary")),
        name="mamba2_ssd",
    )(proj3, proj3, small3, conv_w.astype(F32), conv_b.astype(F32).reshape(1, SSM_XBC), arow, dtb, dsk,
      norm_gain.astype(F32).reshape(1, SSM_WIDTH))


def _outproj_kernel(*refs, with_router):
    if with_router:
        (a_ref, g_ref, s_ref, w_ref, x_ref, ng_ref, wr_ref,
         xo_ref, h_ref, ri_ref, rw_ref, rk_ref, cnt_ref, run_ref) = refs
    else:
        a_ref, g_ref, s_ref, w_ref, x_ref, ng_ref, xo_ref, h_ref = refs
    g0, s0 = ATTN_WIDTH, ATTN_WIDTH + GDN_WIDTH
    y = (jnp.dot(a_ref[...], w_ref[0:g0, :], preferred_element_type=F32)
         + jnp.dot(g_ref[...], w_ref[g0:s0, :], preferred_element_type=F32)
         + jnp.dot(s_ref[...], w_ref[s0:, :], preferred_element_type=F32))
    xn = x_ref[...] + y
    xo_ref[...] = xn
    hf = _rms(xn) * ng_ref[...]
    h = hf.astype(BF16)
    h_ref[...] = hf.astype(h_ref.dtype)
    if with_router:
        tm = xn.shape[0]
        lane = lax.broadcasted_iota(jnp.int32, (1, LANES), 1)
        logits = jnp.where(lane < N_EXPERTS, jnp.dot(h, wr_ref[...], preferred_element_type=F32), -jnp.inf)
        v1 = jnp.max(logits, axis=-1, keepdims=True)
        i1 = jnp.min(jnp.where(logits == v1, lane, LANES), axis=-1, keepdims=True)
        rest = jnp.where(lane == i1, -jnp.inf, logits)
        v2 = jnp.max(rest, axis=-1, keepdims=True)
        i2 = jnp.min(jnp.where(rest == v2, lane, LANES), axis=-1, keepdims=True)
        e2 = jnp.exp(v2 - v1)
        ri_ref[...] = jnp.where(lane == 0, i1, i2)
        rw_ref[...] = jnp.where(lane == 0, 1.0 / (1.0 + e2), e2 / (1.0 + e2))

        @pl.when(pl.program_id(0) == 0)
        def _():
            run_ref[...] = jnp.zeros(run_ref.shape, F32)

        before = _tril(tm, strict=True).astype(BF16)
        run = run_ref[0:1, :]
        ranks = []
        for idx in (i1, i2):
            hit = lane == idx
            onehot = hit.astype(F32)
            earlier = jnp.dot(before, onehot.astype(BF16), preferred_element_type=F32) + run
            ranks.append(jnp.sum(jnp.where(hit, earlier, 0.0), axis=-1, keepdims=True))
            run = run + jnp.sum(onehot, axis=0, keepdims=True)
        rk_ref[...] = jnp.where(lane == 0, ranks[0], ranks[1]).astype(jnp.int32)
        run_ref[...] = jnp.broadcast_to(run, run_ref.shape)
        cnt_ref[...] = jnp.broadcast_to(run, cnt_ref.shape).astype(jnp.int32)


def outproj(attn_o, gdn_o, ssm_o, w_out, x, norm_gain, w_router=None, *, tm=512):
    t, d = x.shape
    with_router = w_router is not None
    row = lambda width: pl.BlockSpec((tm, width), lambda i: (i, 0))
    const = lambda shape: pl.BlockSpec(shape, lambda i: (0, 0))
    in_specs = [row(ATTN_WIDTH), row(GDN_WIDTH), row(SSM_WIDTH), const(w_out.shape), row(d), const((1, d))]
    args = [attn_o, gdn_o, ssm_o, w_out, x, norm_gain.astype(F32).reshape(1, d)]
    out_specs = [row(d), row(d)]
    out_shape = [jax.ShapeDtypeStruct((t, d), F32), jax.ShapeDtypeStruct((t, d), F32 if with_router else BF16)]
    scratch = []
    if with_router:
        in_specs.append(const((d, LANES)))
        args.append(w_router)
        out_specs += [row(LANES), row(LANES), row(LANES), const((8, LANES))]
        out_shape += [jax.ShapeDtypeStruct((t, LANES), jnp.int32), jax.ShapeDtypeStruct((t, LANES), F32),
                      jax.ShapeDtypeStruct((t, LANES), jnp.int32), jax.ShapeDtypeStruct((8, LANES), jnp.int32)]
        scratch = [pltpu.VMEM((8, LANES), F32)]
    return pl.pallas_call(
        functools.partial(_outproj_kernel, with_router=with_router),
        grid=(t // tm,),
        in_specs=in_specs, out_specs=out_specs, out_shape=out_shape, scratch_shapes=scratch,
        compiler_params=_cparams(("arbitrary" if with_router else "parallel",)),
        name="outproj_router" if with_router else "outproj",
    )(*args)


def _ffn_kernel(h_ref, x_ref, wg_ref, wu_ref, wd_ref, o_ref, wgu_ref, wdb_ref, *, sub):
    @pl.when(pl.program_id(1) == 0)
    def _():
        o_ref[...] = x_ref[...]

    tf = wg_ref.shape[1]
    wgu_ref[:, 0:tf] = wg_ref[...].astype(BF16)
    wgu_ref[:, tf:2 * tf] = wu_ref[...].astype(BF16)
    wdb_ref[...] = wd_ref[...].astype(BF16)
    for part in range(o_ref.shape[0] // sub):
        rows = slice(part * sub, (part + 1) * sub)
        gu = jnp.dot(h_ref[rows, :], wgu_ref[...], preferred_element_type=F32)
        a = (_silu(gu[:, :tf]) * gu[:, tf:]).astype(BF16)
        o_ref[rows, :] += jnp.dot(a, wdb_ref[...], preferred_element_type=F32)


def ffn(h, x, wg, wu, wd, *, tm=1024, tf=256):
    t, d = x.shape
    f = wg.shape[1]
    return pl.pallas_call(
        functools.partial(_ffn_kernel, sub=min(512, tm)),
        grid=(t // tm, f // tf),
        in_specs=[
            pl.BlockSpec((tm, d), lambda i, j: (i, 0)),
            pl.BlockSpec((tm, d), lambda i, j: (i, 0), pipeline_mode=pl.Buffered(1)),
            pl.BlockSpec((d, tf), lambda i, j: (0, j)),
            pl.BlockSpec((d, tf), lambda i, j: (0, j)),
            pl.BlockSpec((tf, d), lambda i, j: (j, 0)),
        ],
        out_specs=pl.BlockSpec((tm, d), lambda i, j: (i, 0)),
        out_shape=jax.ShapeDtypeStruct((t, d), F32),
        scratch_shapes=[pltpu.VMEM((d, 2 * tf), BF16), pltpu.VMEM((tf, d), BF16)],
        compiler_params=_cparams(("parallel", "arbitrary")),
        name="ffn_swiglu",
    )(h, x, wg, wu, wd)


def _moe_ffn_kernel(te_ref, np_ref, nt_ref, h_ref, wg_ref, wu_ref, wd_ref, o_ref, hb_ref, wgu_ref, wdb_ref,
                    *, sub):
    i = pl.program_id(0)
    j = pl.program_id(1)
    parts = o_ref.shape[0] // sub
    tf = wg_ref.shape[1]
    n_valid = np_ref[i]

    @pl.when(j == 0)
    def _():
        o_ref[...] = jnp.zeros(o_ref.shape, F32)

    for part in range(parts):
        rows = slice(part * sub, (part + 1) * sub)

        @pl.when(jnp.logical_and(n_valid > part, j == 0))
        def _(rows=rows):
            hb_ref[rows, :] = h_ref[rows, :].astype(BF16)

        @pl.when(n_valid > part)
        def _(rows=rows, part=part):
            if part == 0:
                wgu_ref[:, 0:tf] = wg_ref[...].astype(BF16)
                wgu_ref[:, tf:2 * tf] = wu_ref[...].astype(BF16)
                wdb_ref[...] = wd_ref[...].astype(BF16)
            h = hb_ref[rows, :]
            gu = jnp.dot(h, wgu_ref[...], preferred_element_type=F32)
            a = (_silu(gu[:, :tf]) * gu[:, tf:]).astype(BF16)
            o_ref[rows, :] += jnp.dot(a, wdb_ref[...], preferred_element_type=F32)


def moe_ffn(hs, tile_expert, tile_parts, n_tiles, wg, wu, wd, *, tm, sub, tf=256):
    npad, d = hs.shape
    f = wg.shape[2]
    nj = f // tf
    col = lambda i, j, nt: jnp.where(i < nt[0], j, nj - 1)
    grid_spec = pltpu.PrefetchScalarGridSpec(
        num_scalar_prefetch=3,
        grid=(npad // tm, f // tf),
        in_specs=[
            pl.BlockSpec((tm, d), lambda i, j, te, tp, nt: (jnp.minimum(i, nt[0] - 1), 0)),
            pl.BlockSpec((None, d, tf), lambda i, j, te, tp, nt: (te[i], 0, col(i, j, nt))),
            pl.BlockSpec((None, d, tf), lambda i, j, te, tp, nt: (te[i], 0, col(i, j, nt))),
            pl.BlockSpec((None, tf, d), lambda i, j, te, tp, nt: (te[i], col(i, j, nt), 0)),
        ],
        out_specs=pl.BlockSpec((tm, d), lambda i, j, te, tp, nt: (i, 0)),
        scratch_shapes=[pltpu.VMEM((tm, d), BF16), pltpu.VMEM((d, 2 * tf), BF16), pltpu.VMEM((tf, d), BF16)],
    )
    return pl.pallas_call(
        functools.partial(_moe_ffn_kernel, sub=sub),
        grid_spec=grid_spec,
        out_shape=jax.ShapeDtypeStruct((npad, d), F32),
        compiler_params=_cparams(("parallel", "arbitrary")),
        name="moe_grouped_swiglu",
    )(tile_expert, tile_parts, n_tiles, hs, wg, wu, wd)


DMA_UNROLL = 8


def _dispatch_kernel(pos_ref, pad_ref, h_ref, xs_ref, zero_ref, sem, *, tg, n_tok):
    i = pl.program_id(0)
    base = i * tg

    def copy(k, r):
        return pltpu.make_async_copy(h_ref.at[pl.ds(r, 1), :],
                                     xs_ref.at[pl.ds(pos_ref[k * n_tok + base + r], 1), :], sem)

    def start(r, carry):
        for k in range(TOP_K):
            copy(k, r).start()
        return carry
    lax.fori_loop(0, tg, start, 0, unroll=DMA_UNROLL)

    def wait(r, carry):
        for k in range(TOP_K):
            copy(k, r).wait()
        return carry
    lax.fori_loop(0, tg, wait, 0, unroll=DMA_UNROLL)

    @pl.when(i == pl.num_programs(0) - 1)
    def _():
        zero_ref[...] = jnp.zeros(zero_ref.shape, F32)

        def zero_row(r):
            return pltpu.make_async_copy(zero_ref.at[pl.ds(0, 1), :], xs_ref.at[pl.ds(r, 1), :], sem)

        for e in range(N_EXPERTS):
            lo, hi = pad_ref[e], pad_ref[N_EXPERTS + e]

            def zstart(r, carry):
                zero_row(r).start()
                return carry
            lax.fori_loop(lo, hi, zstart, 0)

            def zwait(r, carry):
                zero_row(r).wait()
                return carry
            lax.fori_loop(lo, hi, zwait, 0)

        for e in range(N_EXPERTS + 1):
            lo = pad_ref[2 * N_EXPERTS + e]
            n_blk = (pad_ref[3 * N_EXPERTS + 1 + e] - lo) // tg

            def zero_tile(c, lo=lo):
                rows = pl.ds(pl.multiple_of(lo + c * tg, tg), tg)
                return pltpu.make_async_copy(zero_ref, xs_ref.at[rows, :], sem)

            def tstart(c, carry, zero_tile=zero_tile):
                zero_tile(c).start()
                return carry
            lax.fori_loop(0, n_blk, tstart, 0)

            def twait(c, carry, zero_tile=zero_tile):
                zero_tile(c).wait()
                return carry
            lax.fori_loop(0, n_blk, twait, 0)


def dispatch_rows(h, pos, pad_ranges, npad, *, tg=256):
    t, d = h.shape
    grid_spec = pltpu.PrefetchScalarGridSpec(
        num_scalar_prefetch=2,
        grid=(t // tg,),
        in_specs=[pl.BlockSpec((tg, d), lambda i, pos, pad: (i, 0))],
        out_specs=pl.BlockSpec(memory_space=pl.ANY),
        scratch_shapes=[pltpu.VMEM((tg, d), F32), pltpu.SemaphoreType.DMA(())],
    )
    return pl.pallas_call(
        functools.partial(_dispatch_kernel, tg=tg, n_tok=t),
        grid_spec=grid_spec,
        out_shape=jax.ShapeDtypeStruct((npad, d), F32),
        compiler_params=_cparams(("arbitrary",)),
        name="dispatch_scatter",
    )(pos, pad_ranges, h)


def _combine_kernel(pos_ref, x_ref, rw_ref, ys_ref, o_ref, buf_ref, sem, *, tc, n_tok):
    base = pl.program_id(0) * tc

    def copy(k, r):
        return pltpu.make_async_copy(ys_ref.at[pl.ds(pos_ref[k * n_tok + base + r], 1), :],
                                     buf_ref.at[k, pl.ds(r, 1), :], sem)

    def start(r, carry):
        for k in range(TOP_K):
            copy(k, r).start()
        return carry
    lax.fori_loop(0, tc, start, 0, unroll=DMA_UNROLL)

    def wait(r, carry):
        for k in range(TOP_K):
            copy(k, r).wait()
        return carry
    lax.fori_loop(0, tc, wait, 0, unroll=DMA_UNROLL)
    rw = rw_ref[...]
    o_ref[...] = x_ref[...] + rw[:, 0:1] * buf_ref[0] + rw[:, 1:2] * buf_ref[1]


def combine_rows(x, rw, ys, pos, *, tc=256):
    t, d = x.shape
    grid_spec = pltpu.PrefetchScalarGridSpec(
        num_scalar_prefetch=1,
        grid=(t // tc,),
        in_specs=[pl.BlockSpec((tc, d), lambda i, pos: (i, 0)), pl.BlockSpec((tc, LANES), lambda i, pos: (i, 0)),
                  pl.BlockSpec(memory_space=pl.ANY)],
        out_specs=pl.BlockSpec((tc, d), lambda i, pos: (i, 0)),
        scratch_shapes=[pltpu.VMEM((TOP_K, tc, d), F32), pltpu.SemaphoreType.DMA(())],
    )
    return pl.pallas_call(
        functools.partial(_combine_kernel, tc=tc, n_tok=t),
        grid_spec=grid_spec,
        out_shape=jax.ShapeDtypeStruct((t, d), F32),
        compiler_params=_cparams(("arbitrary",)),
        name="combine_gather",
    )(pos, x, rw, ys)


def _routing_plan(ridx, rank, counts, tm, sub, npad):
    alloc = ((counts + tm - 1) // tm) * tm
    used = ((counts + sub - 1) // sub) * sub
    ends = jnp.cumsum(alloc)
    starts = ends - alloc
    hit = ridx[:, :, None] == jnp.arange(N_EXPERTS, dtype=jnp.int32)[None, None, :]
    pos = rank + jnp.sum(jnp.where(hit, starts[None, None, :], 0), axis=-1)
    n_tiles = (ends[-1] // tm).astype(jnp.int32).reshape(1)
    tile_row = jnp.arange(npad // tm, dtype=jnp.int32) * tm
    tile_expert = jnp.sum(jnp.minimum(tile_row, ends[-1] - tm)[:, None] >= ends[None, :], axis=-1).astype(jnp.int32)
    used_end = (starts + used)[tile_expert]
    tile_parts = jnp.where(tile_row < ends[-1], jnp.clip((used_end - tile_row) // sub, 0, tm // sub), 0)
    total = jnp.full((1,), npad, jnp.int32)
    pad_ranges = jnp.concatenate([starts + counts, starts + used,
                                  starts + used, ends[-1:], ends, total])
    return (pos.T.reshape(-1).astype(jnp.int32), pad_ranges.astype(jnp.int32), tile_expert,
            tile_parts.astype(jnp.int32), n_tiles)


def moe(h, x, ridx, rw, rank, counts, wg, wu, wd, *, tm=1024, sub=512, tf=256, tg=256):
    t, d = x.shape
    assert tm % sub == 0 and sub % tg == 0 and t % tg == 0
    npad = pl.cdiv(TOP_K * t + N_EXPERTS * tm, tm) * tm
    pos, pad_ranges, tile_expert, tile_parts, n_tiles = _routing_plan(
        ridx[:, :TOP_K], rank[:, :TOP_K], counts[0, :N_EXPERTS], tm, sub, npad)
    xs = dispatch_rows(h, pos, pad_ranges, npad, tg=tg)
    ys = moe_ffn(xs, tile_expert, tile_parts, n_tiles, wg, wu, wd, tm=tm, sub=sub, tf=tf)
    return combine_rows(x, rw, ys, pos, tc=tg)


def _split_w_in(w):
    sizes = (ATTN_WIDTH, ATTN_WIDTH, ATTN_WIDTH, 3 * GDN_WIDTH, GDN_WIDTH, GDN_HEADS, GDN_HEADS,
             SSM_WIDTH, SSM_XBC, SSM_HEADS)
    pieces, start = [], 0
    for size in sizes:
        pieces.append(w[:, start:start + size])
        start += size
    aq, ak, av, gqkv, ggate, gbeta, ga, sz, sxbc, sdt = pieces
    main = jnp.concatenate([gqkv, ggate, sz, sxbc, aq, ak, av], axis=1).astype(BF16)
    small = jnp.concatenate([gbeta, ga, sdt], axis=1)
    small = jnp.pad(small, ((0, 0), (0, LANES - small.shape[1]))).astype(BF16)
    return main, small


def kernel(x, rel_bias, mix_norm, w_in, attn_q_gain, attn_k_gain, attn_lambda, attn_sub_gain, gdn_conv_w, gdn_A_log, gdn_dt_bias, gdn_o_gain, ssm_conv_w, ssm_conv_b, ssm_A_log, ssm_dt_bias, ssm_D, ssm_norm_gain, w_out, ffn_norm, ffn_w_gate, ffn_w_up, ffn_w_down, moe_router, moe_w_gate, moe_w_up, moe_w_down):
    b, s, d = x.shape
    t = b * s
    depth = w_in.shape[0]
    tq = min(512, s)
    bias_tiles = _bias_tiles(rel_bias, tq)
    xf = x.reshape(t, d).astype(F32)
    for li in range(depth):
        w_main, w_small = _split_w_in(w_in[li])
        proj, small = norm_inproj(xf, mix_norm[li].astype(F32), w_main, w_small, tm=min(512, t))
        proj3 = proj.reshape(b, s, N_MAIN)
        small3 = small.reshape(b, s, LANES)
        attn_o = attention(proj3, attn_q_gain[li], attn_k_gain[li], attn_lambda[li], attn_sub_gain[li],
                           bias_tiles, li, tq=tq)
        gdn_o = gdn(proj3, small3, gdn_conv_w[li], gdn_A_log[li], gdn_dt_bias[li], gdn_o_gain[li])
        ssm_o = ssd(proj3, small3, ssm_conv_w[li], ssm_conv_b[li], ssm_A_log[li], ssm_dt_bias[li],
                    ssm_D[li], ssm_norm_gain[li])
        wo = w_out[li].astype(BF16)
        mix = (attn_o.reshape(t, ATTN_WIDTH), gdn_o.reshape(t, GDN_WIDTH), ssm_o.reshape(t, SSM_WIDTH))
        j = li // 2
        if li % 2 == 0:
            xf, h = outproj(*mix, wo, xf, ffn_norm[li], tm=min(512, t))
            xf = ffn(h, xf, ffn_w_gate[j], ffn_w_up[j], ffn_w_down[j], tm=min(1024, t))
        else:
            w_r = jnp.pad(moe_router[j], ((0, 0), (0, LANES - N_EXPERTS))).astype(BF16)
            xf, h, ridx, rw, rank, counts = outproj(*mix, wo, xf, ffn_norm[li], w_r, tm=min(512, t))
            tm = min(1024, t // 2)
            xf = moe(h, xf, ridx, rw, rank, counts, moe_w_gate[j], moe_w_up[j], moe_w_down[j],
                     tm=tm, sub=tm // 2, tg=min(256, tm // 2))
    return xf.reshape(b, s, d).astype(x.dtype)
```

```python
import functools
import math

import jax
import jax.numpy as jnp
from jax import lax
from jax.experimental import pallas as pl
from jax.experimental.pallas import tpu as pltpu

F32 = jnp.float32
BF16 = jnp.bfloat16

NORM_EPS = 1e-6
CONV_K = 4
LANES = 128
HALO = 8

ATTN_HEADS = 4
ATTN_QK_DIM = 64
ATTN_V_DIM = 128
ATTN_WIDTH = 512
REL_BUCKETS = 32
REL_MAX_DIST = 128

GDN_HEADS = 6
GDN_DIM = 128
GDN_WIDTH = 768

SSM_HEADS = 12
SSM_HEAD_DIM = 64
SSM_GROUPS = 2
SSM_STATE = 128
SSM_WIDTH = 768
SSM_XBC = 1280

N_EXPERTS = 8
TOP_K = 2

GDN_COLS = 4 * GDN_WIDTH
SZ_BLOCK = 4
XBC_BLOCK = 3
ATTN_BLOCK0 = (5 * GDN_WIDTH + SSM_XBC) // LANES
N_MAIN = 5 * GDN_WIDTH + SSM_XBC + 3 * ATTN_WIDTH
BETA_LANE0, GA_LANE0, DT_LANE0 = 0, GDN_HEADS, 2 * GDN_HEADS

VMEM_LIMIT = 56 * 1024 * 1024


def _cparams(sem):
    return pltpu.CompilerParams(dimension_semantics=sem, vmem_limit_bytes=VMEM_LIMIT)


def _dot(a, b):
    return jnp.dot(a.astype(BF16), b.astype(BF16), preferred_element_type=F32)


def _dot_nt(a, b):
    return lax.dot_general(a.astype(BF16), b.astype(BF16), (((1,), (1,)), ((), ())),
                           preferred_element_type=F32)


def _dot_tn(a, b):
    return lax.dot_general(a.astype(BF16), b.astype(BF16), (((0,), (0,)), ((), ())),
                           preferred_element_type=F32)


def _dot_f32(a, b):
    return jnp.dot(a, b, preferred_element_type=F32, precision=lax.Precision.HIGHEST)


def _sigmoid(x):
    return 0.5 * jnp.tanh(0.5 * x) + 0.5


def _silu(x):
    return x * _sigmoid(x)


def _softplus(x):
    return jnp.maximum(x, 0.0) + jnp.log(1.0 + jnp.exp(-jnp.abs(x)))


def _rms(x):
    return x * lax.rsqrt(jnp.mean(x * x, axis=-1, keepdims=True) + NORM_EPS)


def _tril(n, strict=False):
    r = lax.broadcasted_iota(jnp.int32, (n, n), 0)
    c = lax.broadcasted_iota(jnp.int32, (n, n), 1)
    return (r > c) if strict else (r >= c)


def _norm_inproj_kernel(x_ref, g_ref, w_ref, ws_ref, o_ref, os_ref, h_ref):
    @pl.when(pl.program_id(1) == 0)
    def _():
        h = (_rms(x_ref[...]) * g_ref[...]).astype(BF16)
        h_ref[...] = h
        os_ref[...] = jnp.dot(h, ws_ref[...], preferred_element_type=F32)

    o_ref[...] = jnp.dot(h_ref[...], w_ref[...], preferred_element_type=F32).astype(o_ref.dtype)


def norm_inproj(x, gain, w_main, w_small, *, tm=512, tn=3328):
    t, d = x.shape
    n = w_main.shape[1]
    return pl.pallas_call(
        _norm_inproj_kernel,
        grid=(t // tm, n // tn),
        in_specs=[
            pl.BlockSpec((tm, d), lambda i, j: (i, 0)),
            pl.BlockSpec((1, d), lambda i, j: (0, 0)),
            pl.BlockSpec((d, tn), lambda i, j: (0, j)),
            pl.BlockSpec((d, LANES), lambda i, j: (0, 0)),
        ],
        out_specs=[
            pl.BlockSpec((tm, tn), lambda i, j: (i, j)),
            pl.BlockSpec((tm, LANES), lambda i, j: (i, 0)),
        ],
        out_shape=[jax.ShapeDtypeStruct((t, n), BF16), jax.ShapeDtypeStruct((t, LANES), F32)],
        scratch_shapes=[pltpu.VMEM((tm, d), BF16)],
        compiler_params=_cparams(("parallel", "arbitrary")),
        name="norm_inproj",
    )(x, gain.reshape(1, d), w_main, w_small)


def _pair_rms(x, gain):
    lo = lax.broadcasted_iota(jnp.int32, (1, LANES), 1) < ATTN_QK_DIM
    sq = x * x
    s_lo = jnp.sum(jnp.where(lo, sq, 0.0), axis=-1, keepdims=True)
    s_hi = jnp.sum(jnp.where(lo, 0.0, sq), axis=-1, keepdims=True)
    ms = jnp.where(lo, s_lo, s_hi) * (1.0 / ATTN_QK_DIM)
    return x * lax.rsqrt(ms + NORM_EPS) * gain


LOG2_E = math.log2(math.e)
ONES_ROWS = 16


def _attn_kernel(lam_ref, qg_ref, kg_ref, sg_ref, bias_ref, q_ref, k_ref, v_ref, o_ref,
                 kn_ref, vt_ref, m_ref, acc_ref, sa_ref, sb_ref, *, tq, lam_init):
    qi = pl.program_id(2)
    lo = lax.broadcasted_iota(jnp.int32, (1, LANES), 1) < ATTN_QK_DIM
    nv = ATTN_V_DIM

    @pl.when(qi == 0)
    def _():
        def body(c, carry):
            rows = pl.ds(pl.multiple_of(c * tq, tq), tq)
            kn_ref[rows, :] = _pair_rms(k_ref[rows, :].astype(F32), kg_ref[...]).astype(BF16)
            vt_ref[c, 0:nv, :] = v_ref[rows, :].astype(F32).T.astype(BF16)
            vt_ref[c, nv:nv + ONES_ROWS, :] = jnp.ones((ONES_ROWS, tq), BF16)
            return carry
        lax.fori_loop(0, k_ref.shape[0] // tq, body, 0)

    q = _pair_rms(q_ref[...].astype(F32), qg_ref[...]) * (ATTN_QK_DIM ** -0.5 * LOG2_E)
    qz = (jnp.where(lo, q, 0.0).astype(BF16), jnp.where(lo, 0.0, q).astype(BF16))

    maps = range(2)

    def scores(kb, n, bias):
        k_blk = kn_ref[pl.ds(pl.multiple_of(kb * tq, tq), n * tq), :]
        s = [lax.dot_general(k_blk, qz[mp], (((1,), (1,)), ((), ())), preferred_element_type=F32) for mp in maps]
        return s if bias is None else [x + bias for x in s]

    def update(s, kb, n, first):
        vt = vt_ref[kb] if n == 1 else jnp.concatenate([vt_ref[kb + i] for i in range(n)], axis=-1)
        s_max = [jnp.max(x, axis=0, keepdims=True) for x in s]
        if first:
            m_new = s_max
        else:
            m_prev = [m_ref[mp] for mp in maps]
            m_new = [jnp.maximum(m_prev[mp], s_max[mp]) for mp in maps]
            alpha = [jnp.exp2(m_prev[mp] - m_new[mp]) for mp in maps]
        p = [jnp.exp2(s[mp] - m_new[mp]).astype(BF16) for mp in maps]
        pv = [jnp.dot(vt, p[mp], preferred_element_type=F32) for mp in maps]
        for mp in maps:
            acc_ref[mp] = pv[mp] if first else alpha[mp] * acc_ref[mp] + pv[mp]
            m_ref[mp] = m_new[mp]

    @pl.when(qi == 0)
    def _():
        update(scores(qi, 1, bias_ref[1]), qi, 1, True)

    n_far = jnp.maximum(qi - 1, 0)
    n_groups = n_far // 2
    bufs = (sa_ref, sb_ref)

    def fill(buf, k, near):
        s = scores(qi - 1, 2, bias_ref[...].reshape(2 * tq, tq)) if near else scores((k - 1) * 2, 2, None)
        for mp in maps:
            buf[mp] = s[mp]

    def drain(buf, k, near):
        update([buf[mp] for mp in maps], (qi - 1) if near else (k - 1) * 2, 2, near)

    def stage(k, cur, nxt, near=False):
        @pl.when(k < n_groups)
        def _():
            fill(nxt, k + 1, False)
            drain(cur, k, near)

        @pl.when(k == n_groups)
        def _():
            drain(cur, k, near)

    @pl.when(qi >= 1)
    def _():
        fill(bufs[0], 0, True)
        stage(0, bufs[0], bufs[1], near=True)

        def far(kk, carry):
            stage(2 * kk + 1, bufs[1], bufs[0])
            stage(2 * kk + 2, bufs[0], bufs[1])
            return carry
        lax.fori_loop(0, (n_groups + 1) // 2, far, 0)

    @pl.when(n_far - 2 * n_groups == 1)
    def _():
        update(scores(n_far - 1, 1, None), n_far - 1, 1, False)

    lam = lam_ref[...]
    lam_full = (jnp.exp(jnp.sum(lam[0:1] * lam[1:2], axis=-1, keepdims=True))
                - jnp.exp(jnp.sum(lam[2:3] * lam[3:4], axis=-1, keepdims=True)) + lam_init)
    a0, a1 = acc_ref[0], acc_ref[1]
    o = a0[0:nv] / a0[nv:nv + 1] - lam_full * (a1[0:nv] / a1[nv:nv + 1])
    o = o * lax.rsqrt(jnp.mean(o * o, axis=0, keepdims=True) + NORM_EPS)
    o_ref[...] = (o.T * sg_ref[...] * (1.0 - lam_init)).astype(o_ref.dtype)


def _t5_bucket(n):
    max_exact = REL_BUCKETS // 2
    nf = jnp.maximum(n, max_exact).astype(F32)
    large = max_exact + (jnp.log(nf / max_exact) / math.log(REL_MAX_DIST / max_exact)
                         * (REL_BUCKETS - max_exact)).astype(jnp.int32)
    return jnp.where(n < max_exact, n, jnp.minimum(large, REL_BUCKETS - 1))


def _bias_tiles(rel_bias, tq):
    assert tq >= REL_MAX_DIST
    heads = rel_bias.shape[1]

    def body(tab_ref, o_ref):
        h = pl.program_id(0)
        r = lax.broadcasted_iota(jnp.int32, (2 * tq, tq), 0)
        i = lax.broadcasted_iota(jnp.int32, (2 * tq, tq), 1)
        rel = tq + i - r
        bucket = _t5_bucket(jnp.maximum(rel, 0))
        far = tab_ref[h, REL_BUCKETS - 1]
        bias = jnp.zeros((2 * tq, tq), F32)
        for bkt in range(REL_BUCKETS - 1):
            bias = jnp.where(bucket == bkt, tab_ref[h, bkt] - far, bias)
        o_ref[...] = jnp.where(rel >= 0, bias * LOG2_E, -jnp.inf)

    tiles = pl.pallas_call(
        body,
        grid=(heads,),
        in_specs=[pl.BlockSpec(memory_space=pltpu.SMEM)],
        out_specs=pl.BlockSpec((None, 2 * tq, tq), lambda h: (h, 0, 0)),
        out_shape=jax.ShapeDtypeStruct((heads, 2 * tq, tq), F32),
        compiler_params=_cparams(("parallel",)),
        name="t5_bias_tiles",
    )(rel_bias.astype(F32).T)
    return tiles.reshape(heads, 2, tq, tq)


def attention(proj3, q_gain, k_gain, lam, sub_gain, bias_tiles, layer_idx, *, tq=512):
    b, s, _ = proj3.shape
    h = ATTN_HEADS
    lam_init = 0.8 - 0.6 * math.exp(-0.3 * layer_idx)
    qg = jnp.tile(q_gain.astype(F32), 2).reshape(1, LANES)
    kg = jnp.tile(k_gain.astype(F32), 2).reshape(1, LANES)
    sg = sub_gain.astype(F32).reshape(1, LANES)
    kern = functools.partial(_attn_kernel, tq=tq, lam_init=lam_init)
    const = lambda shape: pl.BlockSpec(shape, lambda bi, hi, qi: (0,) * len(shape))
    return pl.pallas_call(
        kern,
        grid=(b, h, s // tq),
        in_specs=[
            const((4, ATTN_QK_DIM)), const((1, LANES)), const((1, LANES)), const((1, LANES)),
            pl.BlockSpec((None, 2, tq, tq), lambda bi, hi, qi: (hi, 0, 0, 0)),
            pl.BlockSpec((None, tq, LANES), lambda bi, hi, qi: (bi, qi, ATTN_BLOCK0 + hi)),
            pl.BlockSpec((None, s, LANES), lambda bi, hi, qi: (bi, 0, ATTN_BLOCK0 + h + hi)),
            pl.BlockSpec((None, s, LANES), lambda bi, hi, qi: (bi, 0, ATTN_BLOCK0 + 2 * h + hi)),
        ],
        out_specs=pl.BlockSpec((None, tq, LANES), lambda bi, hi, qi: (bi, qi, hi)),
        out_shape=jax.ShapeDtypeStruct((b, s, ATTN_WIDTH), BF16),
        scratch_shapes=[
            pltpu.VMEM((s, LANES), BF16),
            pltpu.VMEM((s // tq, ATTN_V_DIM + ONES_ROWS, tq), BF16),
            pltpu.VMEM((2, 1, tq), F32),
            pltpu.VMEM((2, ATTN_V_DIM + ONES_ROWS, tq), F32),
            pltpu.VMEM((2, 2 * tq, tq), F32),
            pltpu.VMEM((2, 2 * tq, tq), F32),
        ],
        compiler_params=_cparams(("parallel", "parallel", "arbitrary")),
        name="diff_attention",
    )(lam.astype(F32), qg, kg, sg, bias_tiles, proj3, proj3, proj3)


def _causal_conv(x, w_ref, buf_ref, first):
    rows = x.shape[0]

    @pl.when(first)
    def _():
        buf_ref[0:HALO, :] = jnp.zeros((HALO, x.shape[1]), F32)

    buf_ref[HALO:HALO + rows, :] = x
    y = w_ref[CONV_K - 1:CONV_K, :] * x
    for j in range(CONV_K - 1):
        off = HALO - (CONV_K - 1) + j
        y = y + w_ref[j:j + 1, :] * buf_ref[off:off + rows, :]
    buf_ref[0:HALO, :] = x[rows - HALO:, :]
    return y


def _unit_lower_inverses(mats):
    n = mats[0].shape[0]
    eye = (lax.broadcasted_iota(jnp.int32, (n, n), 0)
           == lax.broadcasted_iota(jnp.int32, (n, n), 1)).astype(F32)
    ps = [eye - a for a in mats]
    bs = [_dot(a, a) for a in mats]
    steps = int(math.log2(n)) - 1
    for i in range(steps):
        if i + 1 < steps:
            both = [_dot(jnp.concatenate([p, b], axis=0), b) for p, b in zip(ps, bs)]
            ps = [p + x[:n] for p, x in zip(ps, both)]
            bs = [x[n:] for x in both]
        else:
            ps = [p + _dot(p, b) for p, b in zip(ps, bs)]
    return ps


def _gdn_kernel(blk_ref, sm_ref, cw_ref, arow_ref, dtb_ref, og_ref, o_ref, buf_ref, st_ref, *, chunk):
    first = pl.program_id(1) == 0
    c = chunk
    d = GDN_DIM
    nq = 3 * GDN_WIDTH
    tile = blk_ref.shape[0]
    heads = range(GDN_HEADS)
    chunks = range(tile // c)

    @pl.when(first)
    def _():
        st_ref[...] = jnp.zeros(st_ref.shape, F32)

    qkv = _silu(_causal_conv(blk_ref[:, 0:nq].astype(F32), cw_ref, buf_ref, first))
    sm = sm_ref[...]
    beta = _sigmoid(sm)
    g = arow_ref[...] * _softplus(sm + dtb_ref[...])
    strict = _tril(c, strict=True)
    causal = _tril(c)
    r = lax.broadcasted_iota(jnp.int32, (tile, tile), 0)
    cc = lax.broadcasted_iota(jnp.int32, (tile, tile), 1)
    same_chunk_tril = jnp.logical_and(r >= cc, (r // c) == (cc // c))
    gc = _dot_f32(same_chunk_tril.astype(F32), g)
    gct = gc.T

    l2 = lambda x: x * lax.rsqrt(jnp.sum(x * x, axis=-1, keepdims=True) + NORM_EPS)
    qn = [l2(qkv[:, h * d:(h + 1) * d]) * (d ** -0.5) for h in heads]
    kn = [l2(qkv[:, GDN_WIDTH + h * d:GDN_WIDTH + (h + 1) * d]) for h in heads]

    pairs = [(ci, h) for ci in chunks for h in heads]
    prep = {}
    for ci, h in pairs:
        rows = slice(ci * c, (ci + 1) * c)
        q, k = qn[h][rows], kn[h][rows]
        v = qkv[rows, 2 * GDN_WIDTH + h * d:2 * GDN_WIDTH + (h + 1) * d]
        bcol = beta[rows, BETA_LANE0 + h:BETA_LANE0 + h + 1]
        col = gc[rows, GA_LANE0 + h:GA_LANE0 + h + 1]
        row = gct[GA_LANE0 + h:GA_LANE0 + h + 1, rows]
        last = gc[(ci + 1) * c - 1:(ci + 1) * c, GA_LANE0 + h:GA_LANE0 + h + 1]
        dec = jnp.exp(jnp.where(causal, col - row, -jnp.inf))
        ecol = jnp.exp(col)
        kb = k * bcol
        prep[ci, h] = dict(k=k, kb=kb, dec=dec, last=last,
                           lhs=jnp.concatenate([kb, q], axis=0),
                           rhs=jnp.concatenate([v * bcol, kb * ecol], axis=-1),
                           qd=q * ecol, kd=k * jnp.exp(last - col))
    for key in pairs:
        x = prep[key]
        both = _dot_nt(x["lhs"], x["k"])
        x["a"] = both[:c] * jnp.where(strict, x["dec"], 0.0)
        x["qk"] = both[c:] * x["dec"]
    t_inv = _unit_lower_inverses([prep[key]["a"] for key in pairs])
    for key, t in zip(pairs, t_inv):
        prep[key]["sol"] = _dot(t, prep[key]["rhs"])

    states = [st_ref[h] for h in heads]
    for ci in chunks:
        xs = [prep[ci, h] for h in heads]
        both = [_dot(jnp.concatenate([x["sol"][:, d:], x["qd"]], axis=0), st) for x, st in zip(xs, states)]
        v_new = [x["sol"][:, :d] - y[:c] for x, y in zip(xs, both)]
        inter = [_dot(x["qk"], vn) for x, vn in zip(xs, v_new)]
        upd = [_dot_tn(x["kd"], vn) for x, vn in zip(xs, v_new)]
        states = [st * jnp.exp(x["last"]) + u for st, x, u in zip(states, xs, upd)]
        rows = slice(ci * c, (ci + 1) * c)
        for h in heads:
            o = both[h][c:] + inter[h]
            gate = blk_ref[rows, nq + h * d:nq + (h + 1) * d].astype(F32)
            o_ref[rows, h * d:(h + 1) * d] = (_rms(o) * og_ref[...] * _silu(gate)).astype(o_ref.dtype)
    for h in heads:
        st_ref[h] = states[h]


def gdn(proj3, small3, conv_w, a_log, dt_bias, o_gain, *, chunk=64, tile=256):
    b, s, _ = proj3.shape
    tile = min(tile, s)
    pad = lambda v, lane0: jnp.zeros((1, LANES), F32).at[0, lane0:lane0 + v.shape[0]].set(v.astype(F32))
    arow = pad(-jnp.exp(a_log.astype(F32)), GA_LANE0)
    dtb = pad(dt_bias, GA_LANE0)
    const = lambda shape: pl.BlockSpec(shape, lambda bi, si: (0,) * len(shape))
    return pl.pallas_call(
        functools.partial(_gdn_kernel, chunk=chunk),
        grid=(b, s // tile),
        in_specs=[
            pl.BlockSpec((None, tile, GDN_COLS), lambda bi, si: (bi, si, 0)),
            pl.BlockSpec((None, tile, LANES), lambda bi, si: (bi, si, 0)),
            const((CONV_K, 3 * GDN_WIDTH)), const((1, LANES)), const((1, LANES)), const((1, GDN_DIM)),
        ],
        out_specs=pl.BlockSpec((None, tile, GDN_WIDTH), lambda bi, si: (bi, si, 0)),
        out_shape=jax.ShapeDtypeStruct((b, s, GDN_WIDTH), BF16),
        scratch_shapes=[
            pltpu.VMEM((HALO + tile, 3 * GDN_WIDTH), F32),
            pltpu.VMEM((GDN_HEADS, GDN_DIM, GDN_DIM), F32),
        ],
        compiler_params=_cparams(("parallel", "arbitrary")),
        name="gated_deltanet",
    )(proj3, small3, conv_w.astype(F32), arow, dtb, o_gain.astype(F32).reshape(1, GDN_DIM))


def _ssd_kernel(xbc_ref, z_ref, sm_ref, cw_ref, cb_ref, arow_ref, dtb_ref, dsk_ref, ng_ref, o_ref,
                buf_ref, st_ref, y_ref, *, chunk):
    first = pl.program_id(1) == 0
    c = chunk
    p2 = 2 * SSM_HEAD_DIM
    heads_per_group = SSM_HEADS // SSM_GROUPS
    gw = SSM_WIDTH // SSM_GROUPS

    @pl.when(first)
    def _():
        st_ref[...] = jnp.zeros(st_ref.shape, F32)

    tile = xbc_ref.shape[0]
    chunks = range(tile // c)
    pairs = range(SSM_HEADS // 2)
    xbc = _silu(_causal_conv(xbc_ref[...].astype(F32), cw_ref, buf_ref, first) + cb_ref[...])
    x = xbc[:, :SSM_WIDTH]
    sm = sm_ref[...]
    dt = _softplus(sm + dtb_ref[...])
    causal = _tril(c)
    r = lax.broadcasted_iota(jnp.int32, (tile, tile), 0)
    cc = lax.broadcasted_iota(jnp.int32, (tile, tile), 1)
    same_chunk_tril = jnp.logical_and(r >= cc, (r // c) == (cc // c))
    acum = _dot_f32(same_chunk_tril.astype(F32), dt * arow_ref[...])
    acum_t = acum.T
    lo = lax.broadcasted_iota(jnp.int32, (1, p2), 1) < SSM_HEAD_DIM
    halves = lambda lane, arr: (arr[:, lane:lane + 1], arr[:, lane + 1:lane + 2])
    sel = lambda pair: jnp.where(lo, pair[0], pair[1])

    bms, cms, cbs = {}, {}, {}
    for ci in chunks:
        rows = slice(ci * c, (ci + 1) * c)
        for grp in range(SSM_GROUPS):
            bms[ci, grp] = xbc[rows, SSM_WIDTH + grp * SSM_STATE:SSM_WIDTH + (grp + 1) * SSM_STATE]
            cms[ci, grp] = xbc[rows, SSM_WIDTH + (SSM_GROUPS + grp) * SSM_STATE:
                               SSM_WIDTH + (SSM_GROUPS + grp + 1) * SSM_STATE]
            cbs[ci, grp] = _dot_nt(cms[ci, grp], bms[ci, grp])
    units = [(ci, pr) for ci in chunks for pr in pairs]
    prep = {}
    for ci, pr in units:
        rows = slice(ci * c, (ci + 1) * c)
        grp = (2 * pr) // heads_per_group
        lane = DT_LANE0 + 2 * pr
        cols = halves(lane, acum[rows])
        lasts = halves(lane, acum[(ci + 1) * c - 1:(ci + 1) * c, :])
        xdt = x[rows, pr * p2:(pr + 1) * p2] * sel(halves(lane, dt[rows]))
        lmats = [jnp.exp(jnp.where(causal, cols[hh] - acum_t[lane + hh:lane + hh + 1, rows], -jnp.inf))
                 for hh in range(2)]
        prep[ci, pr] = dict(
            grp=grp, xdt=xdt, lhs=[cbs[ci, grp] * lm for lm in lmats],
            out_scale=sel((jnp.exp(cols[0]), jnp.exp(cols[1]))),
            xdt_in=xdt * sel((jnp.exp(lasts[0] - cols[0]), jnp.exp(lasts[1] - cols[1]))),
            keep=sel((jnp.exp(lasts[0]), jnp.exp(lasts[1]))))
    for key in units:
        u = prep[key]
        u["y_diag"] = jnp.where(lo, _dot(u["lhs"][0], u["xdt"]), _dot(u["lhs"][1], u["xdt"]))

    states = [st_ref[pr] for pr in pairs]
    for ci in chunks:
        rows = slice(ci * c, (ci + 1) * c)
        us = [prep[ci, pr] for pr in pairs]
        y_off = [_dot(cms[ci, u["grp"]], st) * u["out_scale"] for u, st in zip(us, states)]
        upd = [_dot_tn(bms[ci, u["grp"]], u["xdt_in"]) for u in us]
        states = [st * u["keep"] + d for st, u, d in zip(states, us, upd)]
        for pr in pairs:
            y_ref[rows, pr * p2:(pr + 1) * p2] = us[pr]["y_diag"] + y_off[pr]
    for pr in pairs:
        st_ref[pr] = states[pr]

    y = (y_ref[...] + dsk_ref[...] * x) * _silu(z_ref[...].astype(F32))
    for grp in range(SSM_GROUPS):
        cols = slice(grp * gw, (grp + 1) * gw)
        o_ref[:, cols] = (_rms(y[:, cols]) * ng_ref[:, cols]).astype(o_ref.dtype)


def ssd(proj3, small3, conv_w, conv_b, a_log, dt_bias, d_skip, norm_gain, *, chunk=128, tile=256):
    b, s, _ = proj3.shape
    tile = min(tile, s)
    pad = lambda v: jnp.zeros((1, LANES), F32).at[0, DT_LANE0:DT_LANE0 + SSM_HEADS].set(v.astype(F32))
    arow = pad(-jnp.exp(a_log.astype(F32)))
    dtb = pad(dt_bias)
    dsk = jnp.repeat(d_skip.astype(F32), SSM_HEAD_DIM).reshape(1, SSM_WIDTH)
    const = lambda shape: pl.BlockSpec(shape, lambda bi, si: (0,) * len(shape))
    return pl.pallas_call(
        functools.partial(_ssd_kernel, chunk=chunk),
        grid=(b, s // tile),
        in_specs=[
            pl.BlockSpec((None, tile, SSM_XBC), lambda bi, si: (bi, si, XBC_BLOCK)),
            pl.BlockSpec((None, tile, SSM_WIDTH), lambda bi, si: (bi, si, SZ_BLOCK)),
            pl.BlockSpec((None, tile, LANES), lambda bi, si: (bi, si, 0)),
            const((CONV_K, SSM_XBC)), const((1, SSM_XBC)), const((1, LANES)), const((1, LANES)),
            const((1, SSM_WIDTH)), const((1, SSM_WIDTH)),
        ],
        out_specs=pl.BlockSpec((None, tile, SSM_WIDTH), lambda bi, si: (bi, si, 0)),
        out_shape=jax.ShapeDtypeStruct((b, s, SSM_WIDTH), BF16),
        scratch_shapes=[
            pltpu.VMEM((HALO + tile, SSM_XBC), F32),
            pltpu.VMEM((SSM_HEADS // 2, SSM_STATE, 2 * SSM_HEAD_DIM), F32),
            pltpu.VMEM((tile, SSM_WIDTH), F32),
        ],
        compiler_params=_cparams(("parallel", "arbitrary")),
        name="mamba2_ssd",
    )(proj3, proj3, small3, conv_w.astype(F32), conv_b.astype(F32).reshape(1, SSM_XBC), arow, dtb, dsk,
      norm_gain.astype(F32).reshape(1, SSM_WIDTH))


def _outproj_kernel(*refs, with_router):
    if with_router:
        (a_ref, g_ref, s_ref, w_ref, x_ref, ng_ref, wr_ref,
         xo_ref, h_ref, ri_ref, rw_ref, rk_ref, cnt_ref, run_ref) = refs
    else:
        a_ref, g_ref, s_ref, w_ref, x_ref, ng_ref, xo_ref, h_ref = refs
    g0, s0 = ATTN_WIDTH, ATTN_WIDTH + GDN_WIDTH
    y = (jnp.dot(a_ref[...], w_ref[0:g0, :], preferred_element_type=F32)
         + jnp.dot(g_ref[...], w_ref[g0:s0, :], preferred_element_type=F32)
         + jnp.dot(s_ref[...], w_ref[s0:, :], preferred_element_type=F32))
    xn = x_ref[...] + y
    xo_ref[...] = xn
    hf = _rms(xn) * ng_ref[...]
    h = hf.astype(BF16)
    h_ref[...] = hf.astype(h_ref.dtype)
    if with_router:
        tm = xn.shape[0]
        lane = lax.broadcasted_iota(jnp.int32, (1, LANES), 1)
        logits = jnp.where(lane < N_EXPERTS, jnp.dot(h, wr_ref[...], preferred_element_type=F32), -jnp.inf)
        v1 = jnp.max(logits, axis=-1, keepdims=True)
        i1 = jnp.min(jnp.where(logits == v1, lane, LANES), axis=-1, keepdims=True)
        rest = jnp.where(lane == i1, -jnp.inf, logits)
        v2 = jnp.max(rest, axis=-1, keepdims=True)
        i2 = jnp.min(jnp.where(rest == v2, lane, LANES), axis=-1, keepdims=True)
        e2 = jnp.exp(v2 - v1)
        ri_ref[...] = jnp.where(lane == 0, i1, i2)
        rw_ref[...] = jnp.where(lane == 0, 1.0 / (1.0 + e2), e2 / (1.0 + e2))

        @pl.when(pl.program_id(0) == 0)
        def _():
            run_ref[...] = jnp.zeros(run_ref.shape, F32)

        before = _tril(tm, strict=True).astype(BF16)
        run = run_ref[0:1, :]
        ranks = []
        for idx in (i1, i2):
            hit = lane == idx
            onehot = hit.astype(F32)
            earlier = jnp.dot(before, onehot.astype(BF16), preferred_element_type=F32) + run
            ranks.append(jnp.sum(jnp.where(hit, earlier, 0.0), axis=-1, keepdims=True))
            run = run + jnp.sum(onehot, axis=0, keepdims=True)
        rk_ref[...] = jnp.where(lane == 0, ranks[0], ranks[1]).astype(jnp.int32)
        run_ref[...] = jnp.broadcast_to(run, run_ref.shape)
        cnt_ref[...] = jnp.broadcast_to(run, cnt_ref.shape).astype(jnp.int32)


def outproj(attn_o, gdn_o, ssm_o, w_out, x, norm_gain, w_router=None, *, tm=512):
    t, d = x.shape
    with_router = w_router is not None
    row = lambda width: pl.BlockSpec((tm, width), lambda i: (i, 0))
    const = lambda shape: pl.BlockSpec(shape, lambda i: (0, 0))
    in_specs = [row(ATTN_WIDTH), row(GDN_WIDTH), row(SSM_WIDTH), const(w_out.shape), row(d), const((1, d))]
    args = [attn_o, gdn_o, ssm_o, w_out, x, norm_gain.astype(F32).reshape(1, d)]
    out_specs = [row(d), row(d)]
    out_shape = [jax.ShapeDtypeStruct((t, d), F32), jax.ShapeDtypeStruct((t, d), F32 if with_router else BF16)]
    scratch = []
    if with_router:
        in_specs.append(const((d, LANES)))
        args.append(w_router)
        out_specs += [row(LANES), row(LANES), row(LANES), const((8, LANES))]
        out_shape += [jax.ShapeDtypeStruct((t, LANES), jnp.int32), jax.ShapeDtypeStruct((t, LANES), F32),
                      jax.ShapeDtypeStruct((t, LANES), jnp.int32), jax.ShapeDtypeStruct((8, LANES), jnp.int32)]
        scratch = [pltpu.VMEM((8, LANES), F32)]
    return pl.pallas_call(
        functools.partial(_outproj_kernel, with_router=with_router),
        grid=(t // tm,),
        in_specs=in_specs, out_specs=out_specs, out_shape=out_shape, scratch_shapes=scratch,
        compiler_params=_cparams(("arbitrary" if with_router else "parallel",)),
        name="outproj_router" if with_router else "outproj",
    )(*args)


def _ffn_kernel(h_ref, x_ref, wg_ref, wu_ref, wd_ref, o_ref, wgu_ref, wdb_ref, *, sub):
    @pl.when(pl.program_id(1) == 0)
    def _():
        o_ref[...] = x_ref[...]

    tf = wg_ref.shape[1]
    wgu_ref[:, 0:tf] = wg_ref[...].astype(BF16)
    wgu_ref[:, tf:2 * tf] = wu_ref[...].astype(BF16)
    wdb_ref[...] = wd_ref[...].astype(BF16)
    for part in range(o_ref.shape[0] // sub):
        rows = slice(part * sub, (part + 1) * sub)
        gu = jnp.dot(h_ref[rows, :], wgu_ref[...], preferred_element_type=F32)
        a = (_silu(gu[:, :tf]) * gu[:, tf:]).astype(BF16)
        o_ref[rows, :] += jnp.dot(a, wdb_ref[...], preferred_element_type=F32)


def ffn(h, x, wg, wu, wd, *, tm=1024, tf=256):
    t, d = x.shape
    f = wg.shape[1]
    return pl.pallas_call(
        functools.partial(_ffn_kernel, sub=min(512, tm)),
        grid=(t // tm, f // tf),
        in_specs=[
            pl.BlockSpec((tm, d), lambda i, j: (i, 0)),
            pl.BlockSpec((tm, d), lambda i, j: (i, 0), pipeline_mode=pl.Buffered(1)),
            pl.BlockSpec((d, tf), lambda i, j: (0, j)),
            pl.BlockSpec((d, tf), lambda i, j: (0, j)),
            pl.BlockSpec((tf, d), lambda i, j: (j, 0)),
        ],
        out_specs=pl.BlockSpec((tm, d), lambda i, j: (i, 0)),
        out_shape=jax.ShapeDtypeStruct((t, d), F32),
        scratch_shapes=[pltpu.VMEM((d, 2 * tf), BF16), pltpu.VMEM((tf, d), BF16)],
        compiler_params=_cparams(("parallel", "arbitrary")),
        name="ffn_swiglu",
    )(h, x, wg, wu, wd)


def _moe_ffn_kernel(te_ref, np_ref, nt_ref, h_ref, wg_ref, wu_ref, wd_ref, o_ref, hb_ref, wgu_ref, wdb_ref,
                    *, sub):
    i = pl.program_id(0)
    j = pl.program_id(1)
    parts = o_ref.shape[0] // sub
    tf = wg_ref.shape[1]
    n_valid = np_ref[i]

    @pl.when(j == 0)
    def _():
        o_ref[...] = jnp.zeros(o_ref.shape, F32)

    for part in range(parts):
        rows = slice(part * sub, (part + 1) * sub)

        @pl.when(jnp.logical_and(n_valid > part, j == 0))
        def _(rows=rows):
            hb_ref[rows, :] = h_ref[rows, :].astype(BF16)

        @pl.when(n_valid > part)
        def _(rows=rows, part=part):
            if part == 0:
                wgu_ref[:, 0:tf] = wg_ref[...].astype(BF16)
                wgu_ref[:, tf:2 * tf] = wu_ref[...].astype(BF16)
                wdb_ref[...] = wd_ref[...].astype(BF16)
            h = hb_ref[rows, :]
            gu = jnp.dot(h, wgu_ref[...], preferred_element_type=F32)
            a = (_silu(gu[:, :tf]) * gu[:, tf:]).astype(BF16)
            o_ref[rows, :] += jnp.dot(a, wdb_ref[...], preferred_element_type=F32)


def moe_ffn(hs, tile_expert, tile_parts, n_tiles, wg, wu, wd, *, tm, sub, tf=256):
    npad, d = hs.shape
    f = wg.shape[2]
    nj = f // tf
    col = lambda i, j, nt: jnp.where(i < nt[0], j, nj - 1)
    grid_spec = pltpu.PrefetchScalarGridSpec(
        num_scalar_prefetch=3,
        grid=(npad // tm, f // tf),
        in_specs=[
            pl.BlockSpec((tm, d), lambda i, j, te, tp, nt: (jnp.minimum(i, nt[0] - 1), 0)),
            pl.BlockSpec((None, d, tf), lambda i, j, te, tp, nt: (te[i], 0, col(i, j, nt))),
            pl.BlockSpec((None, d, tf), lambda i, j, te, tp, nt: (te[i], 0, col(i, j, nt))),
            pl.BlockSpec((None, tf, d), lambda i, j, te, tp, nt: (te[i], col(i, j, nt), 0)),
        ],
        out_specs=pl.BlockSpec((tm, d), lambda i, j, te, tp, nt: (i, 0)),
        scratch_shapes=[pltpu.VMEM((tm, d), BF16), pltpu.VMEM((d, 2 * tf), BF16), pltpu.VMEM((tf, d), BF16)],
    )
    return pl.pallas_call(
        functools.partial(_moe_ffn_kernel, sub=sub),
        grid_spec=grid_spec,
        out_shape=jax.ShapeDtypeStruct((npad, d), F32),
        compiler_params=_cparams(("parallel", "arbitrary")),
        name="moe_grouped_swiglu",
    )(tile_expert, tile_parts, n_tiles, hs, wg, wu, wd)


DMA_UNROLL = 8


def _dispatch_kernel(pos_ref, pad_ref, h_ref, xs_ref, zero_ref, sem, *, tg, n_tok):
    i = pl.program_id(0)
    base = i * tg

    def copy(k, r):
        return pltpu.make_async_copy(h_ref.at[pl.ds(r, 1), :],
                                     xs_ref.at[pl.ds(pos_ref[k * n_tok + base + r], 1), :], sem)

    def start(r, carry):
        for k in range(TOP_K):
            copy(k, r).start()
        return carry
    lax.fori_loop(0, tg, start, 0, unroll=DMA_UNROLL)

    def wait(r, carry):
        for k in range(TOP_K):
            copy(k, r).wait()
        return carry
    lax.fori_loop(0, tg, wait, 0, unroll=DMA_UNROLL)

    @pl.when(i == pl.num_programs(0) - 1)
    def _():
        zero_ref[...] = jnp.zeros(zero_ref.shape, F32)

        def zero_row(r):
            return pltpu.make_async_copy(zero_ref.at[pl.ds(0, 1), :], xs_ref.at[pl.ds(r, 1), :], sem)

        for e in range(N_EXPERTS):
            lo, hi = pad_ref[e], pad_ref[N_EXPERTS + e]

            def zstart(r, carry):
                zero_row(r).start()
                return carry
            lax.fori_loop(lo, hi, zstart, 0)

            def zwait(r, carry):
                zero_row(r).wait()
                return carry
            lax.fori_loop(lo, hi, zwait, 0)

        for e in range(N_EXPERTS + 1):
            lo = pad_ref[2 * N_EXPERTS + e]
            n_blk = (pad_ref[3 * N_EXPERTS + 1 + e] - lo) // tg

            def zero_tile(c, lo=lo):
                rows = pl.ds(pl.multiple_of(lo + c * tg, tg), tg)
                return pltpu.make_async_copy(zero_ref, xs_ref.at[rows, :], sem)

            def tstart(c, carry, zero_tile=zero_tile):
                zero_tile(c).start()
                return carry
            lax.fori_loop(0, n_blk, tstart, 0)

            def twait(c, carry, zero_tile=zero_tile):
                zero_tile(c).wait()
                return carry
            lax.fori_loop(0, n_blk, twait, 0)


def dispatch_rows(h, pos, pad_ranges, npad, *, tg=256):
    t, d = h.shape
    grid_spec = pltpu.PrefetchScalarGridSpec(
        num_scalar_prefetch=2,
        grid=(t // tg,),
        in_specs=[pl.BlockSpec((tg, d), lambda i, pos, pad: (i, 0))],
        out_specs=pl.BlockSpec(memory_space=pl.ANY),
        scratch_shapes=[pltpu.VMEM((tg, d), F32), pltpu.SemaphoreType.DMA(())],
    )
    return pl.pallas_call(
        functools.partial(_dispatch_kernel, tg=tg, n_tok=t),
        grid_spec=grid_spec,
        out_shape=jax.ShapeDtypeStruct((npad, d), F32),
        compiler_params=_cparams(("arbitrary",)),
        name="dispatch_scatter",
    )(pos, pad_ranges, h)


def _combine_kernel(pos_ref, x_ref, rw_ref, ys_ref, o_ref, buf_ref, sem, *, tc, n_tok):
    base = pl.program_id(0) * tc

    def copy(k, r):
        return pltpu.make_async_copy(ys_ref.at[pl.ds(pos_ref[k * n_tok + base + r], 1), :],
                                     buf_ref.at[k, pl.ds(r, 1), :], sem)

    def start(r, carry):
        for k in range(TOP_K):
            copy(k, r).start()
        return carry
    lax.fori_loop(0, tc, start, 0, unroll=DMA_UNROLL)

    def wait(r, carry):
        for k in range(TOP_K):
            copy(k, r).wait()
        return carry
    lax.fori_loop(0, tc, wait, 0, unroll=DMA_UNROLL)
    rw = rw_ref[...]
    o_ref[...] = x_ref[...] + rw[:, 0:1] * buf_ref[0] + rw[:, 1:2] * buf_ref[1]


def combine_rows(x, rw, ys, pos, *, tc=256):
    t, d = x.shape
    grid_spec = pltpu.PrefetchScalarGridSpec(
        num_scalar_prefetch=1,
        grid=(t // tc,),
        in_specs=[pl.BlockSpec((tc, d), lambda i, pos: (i, 0)), pl.BlockSpec((tc, LANES), lambda i, pos: (i, 0)),
                  pl.BlockSpec(memory_space=pl.ANY)],
        out_specs=pl.BlockSpec((tc, d), lambda i, pos: (i, 0)),
        scratch_shapes=[pltpu.VMEM((TOP_K, tc, d), F32), pltpu.SemaphoreType.DMA(())],
    )
    return pl.pallas_call(
        functools.partial(_combine_kernel, tc=tc, n_tok=t),
        grid_spec=grid_spec,
        out_shape=jax.ShapeDtypeStruct((t, d), F32),
        compiler_params=_cparams(("arbitrary",)),
        name="combine_gather",
    )(pos, x, rw, ys)


def _routing_plan(ridx, rank, counts, tm, sub, npad):
    alloc = ((counts + tm - 1) // tm) * tm
    used = ((counts + sub - 1) // sub) * sub
    ends = jnp.cumsum(alloc)
    starts = ends - alloc
    hit = ridx[:, :, None] == jnp.arange(N_EXPERTS, dtype=jnp.int32)[None, None, :]
    pos = rank + jnp.sum(jnp.where(hit, starts[None, None, :], 0), axis=-1)
    n_tiles = (ends[-1] // tm).astype(jnp.int32).reshape(1)
    tile_row = jnp.arange(npad // tm, dtype=jnp.int32) * tm
    tile_expert = jnp.sum(jnp.minimum(tile_row, ends[-1] - tm)[:, None] >= ends[None, :], axis=-1).astype(jnp.int32)
    used_end = (starts + used)[tile_expert]
    tile_parts = jnp.where(tile_row < ends[-1], jnp.clip((used_end - tile_row) // sub, 0, tm // sub), 0)
    total = jnp.full((1,), npad, jnp.int32)
    pad_ranges = jnp.concatenate([starts + counts, starts + used,
                                  starts + used, ends[-1:], ends, total])
    return (pos.T.reshape(-1).astype(jnp.int32), pad_ranges.astype(jnp.int32), tile_expert,
            tile_parts.astype(jnp.int32), n_tiles)


def moe(h, x, ridx, rw, rank, counts, wg, wu, wd, *, tm=1024, sub=512, tf=256, tg=256):
    t, d = x.shape
    assert tm % sub == 0 and sub % tg == 0 and t % tg == 0
    npad = pl.cdiv(TOP_K * t + N_EXPERTS * tm, tm) * tm
    pos, pad_ranges, tile_expert, tile_parts, n_tiles = _routing_plan(
        ridx[:, :TOP_K], rank[:, :TOP_K], counts[0, :N_EXPERTS], tm, sub, npad)
    xs = dispatch_rows(h, pos, pad_ranges, npad, tg=tg)
    ys = moe_ffn(xs, tile_expert, tile_parts, n_tiles, wg, wu, wd, tm=tm, sub=sub, tf=tf)
    return combine_rows(x, rw, ys, pos, tc=tg)


def _split_w_in(w):
    sizes = (ATTN_WIDTH, ATTN_WIDTH, ATTN_WIDTH, 3 * GDN_WIDTH, GDN_WIDTH, GDN_HEADS, GDN_HEADS,
             SSM_WIDTH, SSM_XBC, SSM_HEADS)
    pieces, start = [], 0
    for size in sizes:
        pieces.append(w[:, start:start + size])
        start += size
    aq, ak, av, gqkv, ggate, gbeta, ga, sz, sxbc, sdt = pieces
    main = jnp.concatenate([gqkv, ggate, sz, sxbc, aq, ak, av], axis=1).astype(BF16)
    small = jnp.concatenate([gbeta, ga, sdt], axis=1)
    small = jnp.pad(small, ((0, 0), (0, LANES - small.shape[1]))).astype(BF16)
    return main, small


def kernel(x, rel_bias, mix_norm, w_in, attn_q_gain, attn_k_gain, attn_lambda, attn_sub_gain, gdn_conv_w, gdn_A_log, gdn_dt_bias, gdn_o_gain, ssm_conv_w, ssm_conv_b, ssm_A_log, ssm_dt_bias, ssm_D, ssm_norm_gain, w_out, ffn_norm, ffn_w_gate, ffn_w_up, ffn_w_down, moe_router, moe_w_gate, moe_w_up, moe_w_down):
    b, s, d = x.shape
    t = b * s
    depth = w_in.shape[0]
    tq = min(512, s)
    bias_tiles = _bias_tiles(rel_bias, tq)
    xf = x.reshape(t, d).astype(F32)
    for li in range(depth):
        w_main, w_small = _split_w_in(w_in[li])
        proj, small = norm_inproj(xf, mix_norm[li].astype(F32), w_main, w_small, tm=min(512, t))
        proj3 = proj.reshape(b, s, N_MAIN)
        small3 = small.reshape(b, s, LANES)
        attn_o = attention(proj3, attn_q_gain[li], attn_k_gain[li], attn_lambda[li], attn_sub_gain[li],
                           bias_tiles, li, tq=tq)
        gdn_o = gdn(proj3, small3, gdn_conv_w[li], gdn_A_log[li], gdn_dt_bias[li], gdn_o_gain[li])
        ssm_o = ssd(proj3, small3, ssm_conv_w[li], ssm_conv_b[li], ssm_A_log[li], ssm_dt_bias[li],
                    ssm_D[li], ssm_norm_gain[li])
        wo = w_out[li].astype(BF16)
        mix = (attn_o.reshape(t, ATTN_WIDTH), gdn_o.reshape(t, GDN_WIDTH), ssm_o.reshape(t, SSM_WIDTH))
        j = li // 2
        if li % 2 == 0:
            xf, h = outproj(*mix, wo, xf, ffn_norm[li], tm=min(512, t))
            xf = ffn(h, xf, ffn_w_gate[j], ffn_w_up[j], ffn_w_down[j], tm=min(1024, t))
        else:
            w_r = jnp.pad(moe_router[j], ((0, 0), (0, LANES - N_EXPERTS))).astype(BF16)
            xf, h, ridx, rw, rank, counts = outproj(*mix, wo, xf, ffn_norm[li], w_r, tm=min(512, t))
            tm = min(1024, t // 2)
            xf = moe(h, xf, ridx, rw, rank, counts, moe_w_gate[j], moe_w_up[j], moe_w_down[j],
                     tm=tm, sub=tm // 2, tg=min(256, tm // 2))
    return xf.reshape(b, s, d).astype(x.dtype)
```

```python
import functools
import math

import jax
import jax.numpy as jnp
from jax import lax
from jax.experimental import pallas as pl
from jax.experimental.pallas import tpu as pltpu

F32 = jnp.float32
BF16 = jnp.bfloat16

NORM_EPS = 1e-6
CONV_K = 4
LANES = 128
HALO = 8

ATTN_HEADS = 4
ATTN_QK_DIM = 64
ATTN_V_DIM = 128
ATTN_WIDTH = 512
REL_BUCKETS = 32
REL_MAX_DIST = 128

GDN_HEADS = 6
GDN_DIM = 128
GDN_WIDTH = 768

SSM_HEADS = 12
SSM_HEAD_DIM = 64
SSM_GROUPS = 2
SSM_STATE = 128
SSM_WIDTH = 768
SSM_XBC = 1280

N_EXPERTS = 8
TOP_K = 2

GDN_COLS = 4 * GDN_WIDTH
SZ_BLOCK = 4
XBC_BLOCK = 3
ATTN_BLOCK0 = (5 * GDN_WIDTH + SSM_XBC) // LANES
N_MAIN = 5 * GDN_WIDTH + SSM_XBC + 3 * ATTN_WIDTH
BETA_LANE0, GA_LANE0, DT_LANE0 = 0, GDN_HEADS, 2 * GDN_HEADS

VMEM_BYTES = 64 * 1024 * 1024
VMEM_LIMIT = VMEM_BYTES - 8 * 1024 * 1024


def _cparams(sem):
    return pltpu.CompilerParams(dimension_semantics=sem, vmem_limit_bytes=VMEM_LIMIT)


def _tile_plan(t, s):
    moe_tm = min(1024, t // 2)
    return dict(
        proj_tm=min(512, t),
        attn_tq=min(512, s),
        scan_tile=min(256, s),
        ffn_tm=min(1024, t),
        moe_tm=moe_tm,
        row_dma_tile=min(512, moe_tm // 2),
    )


def _dot(a, b):
    return jnp.dot(a.astype(BF16), b.astype(BF16), preferred_element_type=F32)


def _dot_nt(a, b):
    return lax.dot_general(a.astype(BF16), b.astype(BF16), (((1,), (1,)), ((), ())),
                           preferred_element_type=F32)


def _dot_tn(a, b):
    return lax.dot_general(a.astype(BF16), b.astype(BF16), (((0,), (0,)), ((), ())),
                           preferred_element_type=F32)


def _dot_f32(a, b):
    return jnp.dot(a, b, preferred_element_type=F32, precision=lax.Precision.HIGHEST)


def _sigmoid(x):
    return 0.5 * jnp.tanh(0.5 * x) + 0.5


def _silu(x):
    return x * _sigmoid(x)


def _softplus(x):
    return jnp.maximum(x, 0.0) + jnp.log(1.0 + jnp.exp(-jnp.abs(x)))


def _rms(x):
    return x * lax.rsqrt(jnp.mean(x * x, axis=-1, keepdims=True) + NORM_EPS)


def _tril(n, strict=False):
    r = lax.broadcasted_iota(jnp.int32, (n, n), 0)
    c = lax.broadcasted_iota(jnp.int32, (n, n), 1)
    return (r > c) if strict else (r >= c)


def _norm_inproj_kernel(x_ref, g_ref, w_ref, ws_ref, o_ref, os_ref, h_ref):
    @pl.when(pl.program_id(1) == 0)
    def _():
        h = (_rms(x_ref[...]) * g_ref[...]).astype(BF16)
        h_ref[...] = h
        os_ref[...] = jnp.dot(h, ws_ref[...], preferred_element_type=F32)

    o_ref[...] = jnp.dot(h_ref[...], w_ref[...], preferred_element_type=F32).astype(o_ref.dtype)


def norm_inproj(x, gain, w_main, w_small, *, tm=512, tn=3328):
    t, d = x.shape
    n = w_main.shape[1]
    return pl.pallas_call(
        _norm_inproj_kernel,
        grid=(t // tm, n // tn),
        in_specs=[
            pl.BlockSpec((tm, d), lambda i, j: (i, 0)),
            pl.BlockSpec((1, d), lambda i, j: (0, 0)),
            pl.BlockSpec((d, tn), lambda i, j: (0, j)),
            pl.BlockSpec((d, LANES), lambda i, j: (0, 0)),
        ],
        out_specs=[
            pl.BlockSpec((tm, tn), lambda i, j: (i, j)),
            pl.BlockSpec((tm, LANES), lambda i, j: (i, 0)),
        ],
        out_shape=[jax.ShapeDtypeStruct((t, n), BF16), jax.ShapeDtypeStruct((t, LANES), F32)],
        scratch_shapes=[pltpu.VMEM((tm, d), BF16)],
        compiler_params=_cparams(("parallel", "arbitrary")),
        name="norm_inproj",
    )(x, gain.reshape(1, d), w_main, w_small)


def _pair_rms(x, gain):
    lo = lax.broadcasted_iota(jnp.int32, (1, LANES), 1) < ATTN_QK_DIM
    sq = x * x
    s_lo = jnp.sum(jnp.where(lo, sq, 0.0), axis=-1, keepdims=True)
    s_hi = jnp.sum(jnp.where(lo, 0.0, sq), axis=-1, keepdims=True)
    ms = jnp.where(lo, s_lo, s_hi) * (1.0 / ATTN_QK_DIM)
    return x * lax.rsqrt(ms + NORM_EPS) * gain


LOG2_E = math.log2(math.e)
ONES_ROWS = 16


def _attn_kernel(lam_ref, qg_ref, kg_ref, sg_ref, bias_ref, q_ref, k_ref, v_ref, o_ref,
                 kn_ref, vt_ref, m_ref, acc_ref, sa_ref, sb_ref, *, tq, lam_init):
    qi = pl.program_id(2)
    lo = lax.broadcasted_iota(jnp.int32, (1, LANES), 1) < ATTN_QK_DIM
    nv = ATTN_V_DIM

    @pl.when(qi == 0)
    def _():
        def body(c, carry):
            rows = pl.ds(pl.multiple_of(c * tq, tq), tq)
            kn_ref[rows, :] = _pair_rms(k_ref[rows, :].astype(F32), kg_ref[...]).astype(BF16)
            vt_ref[c, 0:nv, :] = v_ref[rows, :].astype(F32).T.astype(BF16)
            vt_ref[c, nv:nv + ONES_ROWS, :] = jnp.ones((ONES_ROWS, tq), BF16)
            return carry
        lax.fori_loop(0, k_ref.shape[0] // tq, body, 0)

    q = _pair_rms(q_ref[...].astype(F32), qg_ref[...]) * (ATTN_QK_DIM ** -0.5 * LOG2_E)
    qz = (jnp.where(lo, q, 0.0).astype(BF16), jnp.where(lo, 0.0, q).astype(BF16))

    maps = range(2)

    def scores(kb, n, bias):
        k_blk = kn_ref[pl.ds(pl.multiple_of(kb * tq, tq), n * tq), :]
        s = [lax.dot_general(k_blk, qz[mp], (((1,), (1,)), ((), ())), preferred_element_type=F32) for mp in maps]
        return s if bias is None else [x + bias for x in s]

    def update(s, kb, n, first):
        vt = vt_ref[kb] if n == 1 else jnp.concatenate([vt_ref[kb + i] for i in range(n)], axis=-1)
        s_max = [jnp.max(x, axis=0, keepdims=True) for x in s]
        if first:
            m_new = s_max
        else:
            m_prev = [m_ref[mp] for mp in maps]
            m_new = [jnp.maximum(m_prev[mp], s_max[mp]) for mp in maps]
            alpha = [jnp.exp2(m_prev[mp] - m_new[mp]) for mp in maps]
        p = [jnp.exp2(s[mp] - m_new[mp]).astype(BF16) for mp in maps]
        pv = [jnp.dot(vt, p[mp], preferred_element_type=F32) for mp in maps]
        for mp in maps:
            acc_ref[mp] = pv[mp] if first else alpha[mp] * acc_ref[mp] + pv[mp]
            m_ref[mp] = m_new[mp]

    @pl.when(qi == 0)
    def _():
        update(scores(qi, 1, bias_ref[1]), qi, 1, True)

    n_far = jnp.maximum(qi - 1, 0)
    n_groups = n_far // 2
    bufs = (sa_ref, sb_ref)

    def fill(buf, k, near):
        s = scores(qi - 1, 2, bias_ref[...].reshape(2 * tq, tq)) if near else scores((k - 1) * 2, 2, None)
        for mp in maps:
            buf[mp] = s[mp]

    def drain(buf, k, near):
        update([buf[mp] for mp in maps], (qi - 1) if near else (k - 1) * 2, 2, near)

    def stage(k, cur, nxt, near=False):
        @pl.when(k < n_groups)
        def _():
            fill(nxt, k + 1, False)
            drain(cur, k, near)

        @pl.when(k == n_groups)
        def _():
            drain(cur, k, near)

    @pl.when(qi >= 1)
    def _():
        fill(bufs[0], 0, True)
        stage(0, bufs[0], bufs[1], near=True)

        def far(kk, carry):
            stage(2 * kk + 1, bufs[1], bufs[0])
            stage(2 * kk + 2, bufs[0], bufs[1])
            return carry
        lax.fori_loop(0, (n_groups + 1) // 2, far, 0)

    @pl.when(n_far - 2 * n_groups == 1)
    def _():
        update(scores(n_far - 1, 1, None), n_far - 1, 1, False)

    lam = lam_ref[...]
    lam_full = (jnp.exp(jnp.sum(lam[0:1] * lam[1:2], axis=-1, keepdims=True))
                - jnp.exp(jnp.sum(lam[2:3] * lam[3:4], axis=-1, keepdims=True)) + lam_init)
    a0, a1 = acc_ref[0], acc_ref[1]
    o = a0[0:nv] / a0[nv:nv + 1] - lam_full * (a1[0:nv] / a1[nv:nv + 1])
    o = o * lax.rsqrt(jnp.mean(o * o, axis=0, keepdims=True) + NORM_EPS)
    o_ref[...] = (o.T * sg_ref[...] * (1.0 - lam_init)).astype(o_ref.dtype)


def _t5_bucket(n):
    max_exact = REL_BUCKETS // 2
    nf = jnp.maximum(n, max_exact).astype(F32)
    large = max_exact + (jnp.log(nf / max_exact) / math.log(REL_MAX_DIST / max_exact)
                         * (REL_BUCKETS - max_exact)).astype(jnp.int32)
    return jnp.where(n < max_exact, n, jnp.minimum(large, REL_BUCKETS - 1))


def _bias_tiles(rel_bias, tq):
    assert tq >= REL_MAX_DIST
    heads = rel_bias.shape[1]

    def body(tab_ref, o_ref):
        h = pl.program_id(0)
        r = lax.broadcasted_iota(jnp.int32, (2 * tq, tq), 0)
        i = lax.broadcasted_iota(jnp.int32, (2 * tq, tq), 1)
        rel = tq + i - r
        bucket = _t5_bucket(jnp.maximum(rel, 0))
        far = tab_ref[h, REL_BUCKETS - 1]
        bias = jnp.zeros((2 * tq, tq), F32)
        for bkt in range(REL_BUCKETS - 1):
            bias = jnp.where(bucket == bkt, tab_ref[h, bkt] - far, bias)
        o_ref[...] = jnp.where(rel >= 0, bias * LOG2_E, -jnp.inf)

    tiles = pl.pallas_call(
        body,
        grid=(heads,),
        in_specs=[pl.BlockSpec(memory_space=pltpu.SMEM)],
        out_specs=pl.BlockSpec((None, 2 * tq, tq), lambda h: (h, 0, 0)),
        out_shape=jax.ShapeDtypeStruct((heads, 2 * tq, tq), F32),
        compiler_params=_cparams(("parallel",)),
        name="t5_bias_tiles",
    )(rel_bias.astype(F32).T)
    return tiles.reshape(heads, 2, tq, tq)


def attention(proj3, q_gain, k_gain, lam, sub_gain, bias_tiles, layer_idx, *, tq=512):
    b, s, _ = proj3.shape
    h = ATTN_HEADS
    lam_init = 0.8 - 0.6 * math.exp(-0.3 * layer_idx)
    qg = jnp.tile(q_gain.astype(F32), 2).reshape(1, LANES)
    kg = jnp.tile(k_gain.astype(F32), 2).reshape(1, LANES)
    sg = sub_gain.astype(F32).reshape(1, LANES)
    kern = functools.partial(_attn_kernel, tq=tq, lam_init=lam_init)
    const = lambda shape: pl.BlockSpec(shape, lambda bi, hi, qi: (0,) * len(shape))
    return pl.pallas_call(
        kern,
        grid=(b, h, s // tq),
        in_specs=[
            const((4, ATTN_QK_DIM)), const((1, LANES)), const((1, LANES)), const((1, LANES)),
            pl.BlockSpec((None, 2, tq, tq), lambda bi, hi, qi: (hi, 0, 0, 0)),
            pl.BlockSpec((None, tq, LANES), lambda bi, hi, qi: (bi, qi, ATTN_BLOCK0 + hi)),
            pl.BlockSpec((None, s, LANES), lambda bi, hi, qi: (bi, 0, ATTN_BLOCK0 + h + hi)),
            pl.BlockSpec((None, s, LANES), lambda bi, hi, qi: (bi, 0, ATTN_BLOCK0 + 2 * h + hi)),
        ],
        out_specs=pl.BlockSpec((None, tq, LANES), lambda bi, hi, qi: (bi, qi, hi)),
        out_shape=jax.ShapeDtypeStruct((b, s, ATTN_WIDTH), BF16),
        scratch_shapes=[
            pltpu.VMEM((s, LANES), BF16),
            pltpu.VMEM((s // tq, ATTN_V_DIM + ONES_ROWS, tq), BF16),
            pltpu.VMEM((2, 1, tq), F32),
            pltpu.VMEM((2, ATTN_V_DIM + ONES_ROWS, tq), F32),
            pltpu.VMEM((2, 2 * tq, tq), F32),
            pltpu.VMEM((2, 2 * tq, tq), F32),
        ],
        compiler_params=_cparams(("parallel", "parallel", "arbitrary")),
        name="diff_attention",
    )(lam.astype(F32), qg, kg, sg, bias_tiles, proj3, proj3, proj3)


def _causal_conv(x, w_ref, buf_ref, first):
    rows = x.shape[0]

    @pl.when(first)
    def _():
        buf_ref[0:HALO, :] = jnp.zeros((HALO, x.shape[1]), F32)

    buf_ref[HALO:HALO + rows, :] = x
    y = w_ref[CONV_K - 1:CONV_K, :] * x
    for j in range(CONV_K - 1):
        off = HALO - (CONV_K - 1) + j
        y = y + w_ref[j:j + 1, :] * buf_ref[off:off + rows, :]
    buf_ref[0:HALO, :] = x[rows - HALO:, :]
    return y


def _unit_lower_inverses(mats):
    n = mats[0].shape[0]
    eye = (lax.broadcasted_iota(jnp.int32, (n, n), 0)
           == lax.broadcasted_iota(jnp.int32, (n, n), 1)).astype(F32)
    ps = [eye - a for a in mats]
    bs = [_dot(a, a) for a in mats]
    steps = int(math.log2(n)) - 1
    for i in range(steps):
        if i + 1 < steps:
            both = [_dot(jnp.concatenate([p, b], axis=0), b) for p, b in zip(ps, bs)]
            ps = [p + x[:n] for p, x in zip(ps, both)]
            bs = [x[n:] for x in both]
        else:
            ps = [p + _dot(p, b) for p, b in zip(ps, bs)]
    return ps


def _gdn_kernel(blk_ref, sm_ref, cw_ref, arow_ref, dtb_ref, og_ref, o_ref, buf_ref, st_ref, *, chunk):
    first = pl.program_id(1) == 0
    c = chunk
    d = GDN_DIM
    nq = 3 * GDN_WIDTH
    tile = blk_ref.shape[0]
    heads = range(GDN_HEADS)
    chunks = range(tile // c)

    @pl.when(first)
    def _():
        st_ref[...] = jnp.zeros(st_ref.shape, F32)

    qkv = _silu(_causal_conv(blk_ref[:, 0:nq].astype(F32), cw_ref, buf_ref, first))
    sm = sm_ref[...]
    beta = _sigmoid(sm)
    g = arow_ref[...] * _softplus(sm + dtb_ref[...])
    strict = _tril(c, strict=True)
    causal = _tril(c)
    r = lax.broadcasted_iota(jnp.int32, (tile, tile), 0)
    cc = lax.broadcasted_iota(jnp.int32, (tile, tile), 1)
    same_chunk_tril = jnp.logical_and(r >= cc, (r // c) == (cc // c))
    gc = _dot_f32(same_chunk_tril.astype(F32), g)
    gct = gc.T

    l2 = lambda x: x * lax.rsqrt(jnp.sum(x * x, axis=-1, keepdims=True) + NORM_EPS)
    qn = [l2(qkv[:, h * d:(h + 1) * d]) * (d ** -0.5) for h in heads]
    kn = [l2(qkv[:, GDN_WIDTH + h * d:GDN_WIDTH + (h + 1) * d]) for h in heads]

    pairs = [(ci, h) for ci in chunks for h in heads]
    prep = {}
    for ci, h in pairs:
        rows = slice(ci * c, (ci + 1) * c)
        q, k = qn[h][rows], kn[h][rows]
        v = qkv[rows, 2 * GDN_WIDTH + h * d:2 * GDN_WIDTH + (h + 1) * d]
        bcol = beta[rows, BETA_LANE0 + h:BETA_LANE0 + h + 1]
        col = gc[rows, GA_LANE0 + h:GA_LANE0 + h + 1]
        row = gct[GA_LANE0 + h:GA_LANE0 + h + 1, rows]
        last = gc[(ci + 1) * c - 1:(ci + 1) * c, GA_LANE0 + h:GA_LANE0 + h + 1]
        dec = jnp.exp(jnp.where(causal, col - row, -jnp.inf))
        ecol = jnp.exp(col)
        kb = k * bcol
        prep[ci, h] = dict(k=k, kb=kb, dec=dec, last=last,
                           lhs=jnp.concatenate([kb, q], axis=0),
                           rhs=jnp.concatenate([v * bcol, kb * ecol], axis=-1),
                           qd=q * ecol, kd=k * jnp.exp(last - col))
    for key in pairs:
        x = prep[key]
        both = _dot_nt(x["lhs"], x["k"])
        x["a"] = both[:c] * jnp.where(strict, x["dec"], 0.0)
        x["qk"] = both[c:] * x["dec"]
    t_inv = _unit_lower_inverses([prep[key]["a"] for key in pairs])
    for key, t in zip(pairs, t_inv):
        prep[key]["sol"] = _dot(t, prep[key]["rhs"])

    states = [st_ref[h] for h in heads]
    for ci in chunks:
        xs = [prep[ci, h] for h in heads]
        both = [_dot(jnp.concatenate([x["sol"][:, d:], x["qd"]], axis=0), st) for x, st in zip(xs, states)]
        v_new = [x["sol"][:, :d] - y[:c] for x, y in zip(xs, both)]
        inter = [_dot(x["qk"], vn) for x, vn in zip(xs, v_new)]
        upd = [_dot_tn(x["kd"], vn) for x, vn in zip(xs, v_new)]
        states = [st * jnp.exp(x["last"]) + u for st, x, u in zip(states, xs, upd)]
        rows = slice(ci * c, (ci + 1) * c)
        for h in heads:
            o = both[h][c:] + inter[h]
            gate = blk_ref[rows, nq + h * d:nq + (h + 1) * d].astype(F32)
            o_ref[rows, h * d:(h + 1) * d] = (_rms(o) * og_ref[...] * _silu(gate)).astype(o_ref.dtype)
    for h in heads:
        st_ref[h] = states[h]


def gdn(proj3, small3, conv_w, a_log, dt_bias, o_gain, *, chunk=64, tile=256):
    b, s, _ = proj3.shape
    tile = min(tile, s)
    pad = lambda v, lane0: jnp.zeros((1, LANES), F32).at[0, lane0:lane0 + v.shape[0]].set(v.astype(F32))
    arow = pad(-jnp.exp(a_log.astype(F32)), GA_LANE0)
    dtb = pad(dt_bias, GA_LANE0)
    const = lambda shape: pl.BlockSpec(shape, lambda bi, si: (0,) * len(shape))
    return pl.pallas_call(
        functools.partial(_gdn_kernel, chunk=chunk),
        grid=(b, s // tile),
        in_specs=[
            pl.BlockSpec((None, tile, GDN_COLS), lambda bi, si: (bi, si, 0)),
            pl.BlockSpec((None, tile, LANES), lambda bi, si: (bi, si, 0)),
            const((CONV_K, 3 * GDN_WIDTH)), const((1, LANES)), const((1, LANES)), const((1, GDN_DIM)),
        ],
        out_specs=pl.BlockSpec((None, tile, GDN_WIDTH), lambda bi, si: (bi, si, 0)),
        out_shape=jax.ShapeDtypeStruct((b, s, GDN_WIDTH), BF16),
        scratch_shapes=[
            pltpu.VMEM((HALO + tile, 3 * GDN_WIDTH), F32),
            pltpu.VMEM((GDN_HEADS, GDN_DIM, GDN_DIM), F32),
        ],
        compiler_params=_cparams(("parallel", "arbitrary")),
        name="gated_deltanet",
    )(proj3, small3, conv_w.astype(F32), arow, dtb, o_gain.astype(F32).reshape(1, GDN_DIM))


def _ssd_kernel(xbc_ref, z_ref, sm_ref, cw_ref, cb_ref, arow_ref, dtb_ref, dsk_ref, ng_ref, o_ref,
                buf_ref, st_ref, y_ref, *, chunk):
    first = pl.program_id(1) == 0
    c = chunk
    p2 = 2 * SSM_HEAD_DIM
    heads_per_group = SSM_HEADS // SSM_GROUPS
    gw = SSM_WIDTH // SSM_GROUPS

    @pl.when(first)
    def _():
        st_ref[...] = jnp.zeros(st_ref.shape, F32)

    tile = xbc_ref.shape[0]
    chunks = range(tile // c)
    pairs = range(SSM_HEADS // 2)
    xbc = _silu(_causal_conv(xbc_ref[...].astype(F32), cw_ref, buf_ref, first) + cb_ref[...])
    x = xbc[:, :SSM_WIDTH]
    sm = sm_ref[...]
    dt = _softplus(sm + dtb_ref[...])
    causal = _tril(c)
    r = lax.broadcasted_iota(jnp.int32, (tile, tile), 0)
    cc = lax.broadcasted_iota(jnp.int32, (tile, tile), 1)
    same_chunk_tril = jnp.logical_and(r >= cc, (r // c) == (cc // c))
    acum = _dot_f32(same_chunk_tril.astype(F32), dt * arow_ref[...])
    acum_t = acum.T
    lo = lax.broadcasted_iota(jnp.int32, (1, p2), 1) < SSM_HEAD_DIM
    halves = lambda lane, arr: (arr[:, lane:lane + 1], arr[:, lane + 1:lane + 2])
    sel = lambda pair: jnp.where(lo, pair[0], pair[1])

    bms, cms, cbs = {}, {}, {}
    for ci in chunks:
        rows = slice(ci * c, (ci + 1) * c)
        for grp in range(SSM_GROUPS):
            bms[ci, grp] = xbc[rows, SSM_WIDTH + grp * SSM_STATE:SSM_WIDTH + (grp + 1) * SSM_STATE]
            cms[ci, grp] = xbc[rows, SSM_WIDTH + (SSM_GROUPS + grp) * SSM_STATE:
                               SSM_WIDTH + (SSM_GROUPS + grp + 1) * SSM_STATE]
            cbs[ci, grp] = _dot_nt(cms[ci, grp], bms[ci, grp])
    units = [(ci, pr) for ci in chunks for pr in pairs]
    prep = {}
    for ci, pr in units:
        rows = slice(ci * c, (ci + 1) * c)
        grp = (2 * pr) // heads_per_group
        lane = DT_LANE0 + 2 * pr
        cols = halves(lane, acum[rows])
        lasts = halves(lane, acum[(ci + 1) * c - 1:(ci + 1) * c, :])
        xdt = x[rows, pr * p2:(pr + 1) * p2] * sel(halves(lane, dt[rows]))
        lmats = [jnp.exp(jnp.where(causal, cols[hh] - acum_t[lane + hh:lane + hh + 1, rows], -jnp.inf))
                 for hh in range(2)]
        prep[ci, pr] = dict(
            grp=grp, xdt=xdt, lhs=[cbs[ci, grp] * lm for lm in lmats],
            out_scale=sel((jnp.exp(cols[0]), jnp.exp(cols[1]))),
            xdt_in=xdt * sel((jnp.exp(lasts[0] - cols[0]), jnp.exp(lasts[1] - cols[1]))),
            keep=sel((jnp.exp(lasts[0]), jnp.exp(lasts[1]))))
    for key in units:
        u = prep[key]
        u["y_diag"] = jnp.where(lo, _dot(u["lhs"][0], u["xdt"]), _dot(u["lhs"][1], u["xdt"]))

    states = [st_ref[pr] for pr in pairs]
    for ci in chunks:
        rows = slice(ci * c, (ci + 1) * c)
        us = [prep[ci, pr] for pr in pairs]
        y_off = [_dot(cms[ci, u["grp"]], st) * u["out_scale"] for u, st in zip(us, states)]
        upd = [_dot_tn(bms[ci, u["grp"]], u["xdt_in"]) for u in us]
        states = [st * u["keep"] + d for st, u, d in zip(states, us, upd)]
        for pr in pairs:
            y_ref[rows, pr * p2:(pr + 1) * p2] = us[pr]["y_diag"] + y_off[pr]
    for pr in pairs:
        st_ref[pr] = states[pr]

    y = (y_ref[...] + dsk_ref[...] * x) * _silu(z_ref[...].astype(F32))
    for grp in range(SSM_GROUPS):
        cols = slice(grp * gw, (grp + 1) * gw)
        o_ref[:, cols] = (_rms(y[:, cols]) * ng_ref[:, cols]).astype(o_ref.dtype)


def ssd(proj3, small3, conv_w, conv_b, a_log, dt_bias, d_skip, norm_gain, *, chunk=128, tile=256):
    b, s, _ = proj3.shape
    tile = min(tile, s)
    pad = lambda v: jnp.zeros((1, LANES), F32).at[0, DT_LANE0:DT_LANE0 + SSM_HEADS].set(v.astype(F32))
    arow = pad(-jnp.exp(a_log.astype(F32)))
    dtb = pad(dt_bias)
    dsk = jnp.repeat(d_skip.astype(F32), SSM_HEAD_DIM).reshape(1, SSM_WIDTH)
    const = lambda shape: pl.BlockSpec(shape, lambda bi, si: (0,) * len(shape))
    return pl.pallas_call(
        functools.partial(_ssd_kernel, chunk=chunk),
        grid=(b, s // tile),
        in_specs=[
            pl.BlockSpec((None, tile, SSM_XBC), lambda bi, si: (bi, si, XBC_BLOCK)),
            pl.BlockSpec((None, tile, SSM_WIDTH), lambda bi, si: (bi, si, SZ_BLOCK)),
            pl.BlockSpec((None, tile, LANES), lambda bi, si: (bi, si, 0)),
            const((CONV_K, SSM_XBC)), const((1, SSM_XBC)), const((1, LANES)), const((1, LANES)),
            const((1, SSM_WIDTH)), const((1, SSM_WIDTH)),
        ],
        out_specs=pl.BlockSpec((None, tile, SSM_WIDTH), lambda bi, si: (bi, si, 0)),
        out_shape=jax.ShapeDtypeStruct((b, s, SSM_WIDTH), BF16),
        scratch_shapes=[
            pltpu.VMEM((HALO + tile, SSM_XBC), F32),
            pltpu.VMEM((SSM_HEADS // 2, SSM_STATE, 2 * SSM_HEAD_DIM), F32),
            pltpu.VMEM((tile, SSM_WIDTH), F32),
        ],
        compiler_params=_cparams(("parallel", "arbitrary")),
        name="mamba2_ssd",
    )(proj3, proj3, small3, conv_w.astype(F32), conv_b.astype(F32).reshape(1, SSM_XBC), arow, dtb, dsk,
      norm_gain.astype(F32).reshape(1, SSM_WIDTH))


def _outproj_kernel(*refs, with_router):
    if with_router:
        (a_ref, g_ref, s_ref, w_ref, x_ref, ng_ref, wr_ref,
         xo_ref, h_ref, ri_ref, rw_ref, rk_ref, cnt_ref, run_ref) = refs
    else:
        a_ref, g_ref, s_ref, w_ref, x_ref, ng_ref, xo_ref, h_ref = refs
    g0, s0 = ATTN_WIDTH, ATTN_WIDTH + GDN_WIDTH
    y = (jnp.dot(a_ref[...], w_ref[0:g0, :], preferred_element_type=F32)
         + jnp.dot(g_ref[...], w_ref[g0:s0, :], preferred_element_type=F32)
         + jnp.dot(s_ref[...], w_ref[s0:, :], preferred_element_type=F32))
    xn = x_ref[...] + y
    xo_ref[...] = xn
    hf = _rms(xn) * ng_ref[...]
    h = hf.astype(BF16)
    h_ref[...] = hf.astype(h_ref.dtype)
    if with_router:
        tm = xn.shape[0]
        lane = lax.broadcasted_iota(jnp.int32, (1, LANES), 1)
        logits = jnp.where(lane < N_EXPERTS, jnp.dot(h, wr_ref[...], preferred_element_type=F32), -jnp.inf)
        v1 = jnp.max(logits, axis=-1, keepdims=True)
        i1 = jnp.min(jnp.where(logits == v1, lane, LANES), axis=-1, keepdims=True)
        rest = jnp.where(lane == i1, -jnp.inf, logits)
        v2 = jnp.max(rest, axis=-1, keepdims=True)
        i2 = jnp.min(jnp.where(rest == v2, lane, LANES), axis=-1, keepdims=True)
        e2 = jnp.exp(v2 - v1)
        ri_ref[...] = jnp.where(lane == 0, i1, i2)
        rw_ref[...] = jnp.where(lane == 0, 1.0 / (1.0 + e2), e2 / (1.0 + e2))

        @pl.when(pl.program_id(0) == 0)
        def _():
            run_ref[...] = jnp.zeros(run_ref.shape, F32)

        before = _tril(tm, strict=True).astype(BF16)
        run = run_ref[0:1, :]
        ranks = []
        for idx in (i1, i2):
            hit = lane == idx
            onehot = hit.astype(F32)
            earlier = jnp.dot(before, onehot.astype(BF16), preferred_element_type=F32) + run
            ranks.append(jnp.sum(jnp.where(hit, earlier, 0.0), axis=-1, keepdims=True))
            run = run + jnp.sum(onehot, axis=0, keepdims=True)
        rk_ref[...] = jnp.where(lane == 0, ranks[0], ranks[1]).astype(jnp.int32)
        run_ref[...] = jnp.broadcast_to(run, run_ref.shape)
        cnt_ref[...] = jnp.broadcast_to(run, cnt_ref.shape).astype(jnp.int32)


def outproj(attn_o, gdn_o, ssm_o, w_out, x, norm_gain, w_router=None, *, tm=512):
    t, d = x.shape
    with_router = w_router is not None
    row = lambda width: pl.BlockSpec((tm, width), lambda i: (i, 0))
    const = lambda shape: pl.BlockSpec(shape, lambda i: (0, 0))
    in_specs = [row(ATTN_WIDTH), row(GDN_WIDTH), row(SSM_WIDTH), const(w_out.shape), row(d), const((1, d))]
    args = [attn_o, gdn_o, ssm_o, w_out, x, norm_gain.astype(F32).reshape(1, d)]
    out_specs = [row(d), row(d)]
    out_shape = [jax.ShapeDtypeStruct((t, d), F32), jax.ShapeDtypeStruct((t, d), F32 if with_router else BF16)]
    scratch = []
    if with_router:
        in_specs.append(const((d, LANES)))
        args.append(w_router)
        out_specs += [row(LANES), row(LANES), row(LANES), const((8, LANES))]
        out_shape += [jax.ShapeDtypeStruct((t, LANES), jnp.int32), jax.ShapeDtypeStruct((t, LANES), F32),
                      jax.ShapeDtypeStruct((t, LANES), jnp.int32), jax.ShapeDtypeStruct((8, LANES), jnp.int32)]
        scratch = [pltpu.VMEM((8, LANES), F32)]
    return pl.pallas_call(
        functools.partial(_outproj_kernel, with_router=with_router),
        grid=(t // tm,),
        in_specs=in_specs, out_specs=out_specs, out_shape=out_shape, scratch_shapes=scratch,
        compiler_params=_cparams(("arbitrary" if with_router else "parallel",)),
        name="outproj_router" if with_router else "outproj",
    )(*args)


def _ffn_kernel(h_ref, x_ref, wg_ref, wu_ref, wd_ref, o_ref, wgu_ref, wdb_ref, *, sub):
    @pl.when(pl.program_id(1) == 0)
    def _():
        o_ref[...] = x_ref[...]

    tf = wg_ref.shape[1]
    wgu_ref[:, 0:tf] = wg_ref[...].astype(BF16)
    wgu_ref[:, tf:2 * tf] = wu_ref[...].astype(BF16)
    wdb_ref[...] = wd_ref[...].astype(BF16)
    for part in range(o_ref.shape[0] // sub):
        rows = slice(part * sub, (part + 1) * sub)
        gu = jnp.dot(h_ref[rows, :], wgu_ref[...], preferred_element_type=F32)
        a = (_silu(gu[:, :tf]) * gu[:, tf:]).astype(BF16)
        o_ref[rows, :] += jnp.dot(a, wdb_ref[...], preferred_element_type=F32)


def ffn(h, x, wg, wu, wd, *, tm=1024, tf=256):
    t, d = x.shape
    f = wg.shape[1]
    return pl.pallas_call(
        functools.partial(_ffn_kernel, sub=min(512, tm)),
        grid=(t // tm, f // tf),
        in_specs=[
            pl.BlockSpec((tm, d), lambda i, j: (i, 0)),
            pl.BlockSpec((tm, d), lambda i, j: (i, 0), pipeline_mode=pl.Buffered(1)),
            pl.BlockSpec((d, tf), lambda i, j: (0, j)),
            pl.BlockSpec((d, tf), lambda i, j: (0, j)),
            pl.BlockSpec((tf, d), lambda i, j: (j, 0)),
        ],
        out_specs=pl.BlockSpec((tm, d), lambda i, j: (i, 0)),
        out_shape=jax.ShapeDtypeStruct((t, d), F32),
        scratch_shapes=[pltpu.VMEM((d, 2 * tf), BF16), pltpu.VMEM((tf, d), BF16)],
        compiler_params=_cparams(("parallel", "arbitrary")),
        name="ffn_swiglu",
    )(h, x, wg, wu, wd)


def _moe_ffn_kernel(te_ref, np_ref, nt_ref, h_ref, wg_ref, wu_ref, wd_ref, o_ref, hb_ref, wgu_ref, wdb_ref,
                    *, sub):
    i = pl.program_id(0)
    j = pl.program_id(1)
    parts = o_ref.shape[0] // sub
    tf = wg_ref.shape[1]
    n_valid = np_ref[i]

    @pl.when(j == 0)
    def _():
        o_ref[...] = jnp.zeros(o_ref.shape, F32)

    for part in range(parts):
        rows = slice(part * sub, (part + 1) * sub)

        @pl.when(jnp.logical_and(n_valid > part, j == 0))
        def _(rows=rows):
            hb_ref[rows, :] = h_ref[rows, :].astype(BF16)

        @pl.when(n_valid > part)
        def _(rows=rows, part=part):
            if part == 0:
                wgu_ref[:, 0:tf] = wg_ref[...].astype(BF16)
                wgu_ref[:, tf:2 * tf] = wu_ref[...].astype(BF16)
                wdb_ref[...] = wd_ref[...].astype(BF16)
            h = hb_ref[rows, :]
            gu = jnp.dot(h, wgu_ref[...], preferred_element_type=F32)
            a = (_silu(gu[:, :tf]) * gu[:, tf:]).astype(BF16)
            o_ref[rows, :] += jnp.dot(a, wdb_ref[...], preferred_element_type=F32)


def moe_ffn(hs, tile_expert, tile_parts, n_tiles, wg, wu, wd, *, tm, sub, tf=256):
    npad, d = hs.shape
    f = wg.shape[2]
    nj = f // tf
    col = lambda i, j, nt: jnp.where(i < nt[0], j, nj - 1)
    grid_spec = pltpu.PrefetchScalarGridSpec(
        num_scalar_prefetch=3,
        grid=(npad // tm, f // tf),
        in_specs=[
            pl.BlockSpec((tm, d), lambda i, j, te, tp, nt: (jnp.minimum(i, nt[0] - 1), 0)),
            pl.BlockSpec((None, d, tf), lambda i, j, te, tp, nt: (te[i], 0, col(i, j, nt))),
            pl.BlockSpec((None, d, tf), lambda i, j, te, tp, nt: (te[i], 0, col(i, j, nt))),
            pl.BlockSpec((None, tf, d), lambda i, j, te, tp, nt: (te[i], col(i, j, nt), 0)),
        ],
        out_specs=pl.BlockSpec((tm, d), lambda i, j, te, tp, nt: (i, 0)),
        scratch_shapes=[pltpu.VMEM((tm, d), BF16), pltpu.VMEM((d, 2 * tf), BF16), pltpu.VMEM((tf, d), BF16)],
    )
    return pl.pallas_call(
        functools.partial(_moe_ffn_kernel, sub=sub),
        grid_spec=grid_spec,
        out_shape=jax.ShapeDtypeStruct((npad, d), F32),
        compiler_params=_cparams(("parallel", "arbitrary")),
        name="moe_grouped_swiglu",
    )(tile_expert, tile_parts, n_tiles, hs, wg, wu, wd)


DMA_UNROLL = 8


def _dispatch_kernel(pos_ref, pad_ref, h_ref, xs_ref, zero_ref, sem, *, tg, n_tok):
    i = pl.program_id(0)
    base = i * tg

    def copy(k, r):
        return pltpu.make_async_copy(h_ref.at[pl.ds(r, 1), :],
                                     xs_ref.at[pl.ds(pos_ref[k * n_tok + base + r], 1), :], sem)

    def start(r, carry):
        for k in range(TOP_K):
            copy(k, r).start()
        return carry
    lax.fori_loop(0, tg, start, 0, unroll=DMA_UNROLL)

    def wait(r, carry):
        for k in range(TOP_K):
            copy(k, r).wait()
        return carry
    lax.fori_loop(0, tg, wait, 0, unroll=DMA_UNROLL)

    @pl.when(i == pl.num_programs(0) - 1)
    def _():
        zero_ref[...] = jnp.zeros(zero_ref.shape, F32)

        def zero_row(r):
            return pltpu.make_async_copy(zero_ref.at[pl.ds(0, 1), :], xs_ref.at[pl.ds(r, 1), :], sem)

        for e in range(N_EXPERTS):
            lo, hi = pad_ref[e], pad_ref[N_EXPERTS + e]

            def zstart(r, carry):
                zero_row(r).start()
                return carry
            lax.fori_loop(lo, hi, zstart, 0)

            def zwait(r, carry):
                zero_row(r).wait()
                return carry
            lax.fori_loop(lo, hi, zwait, 0)

        for e in range(N_EXPERTS + 1):
            lo = pad_ref[2 * N_EXPERTS + e]
            n_blk = (pad_ref[3 * N_EXPERTS + 1 + e] - lo) // tg

            def zero_tile(c, lo=lo):
                rows = pl.ds(pl.multiple_of(lo + c * tg, tg), tg)
                return pltpu.make_async_copy(zero_ref, xs_ref.at[rows, :], sem)

            def tstart(c, carry, zero_tile=zero_tile):
                zero_tile(c).start()
                return carry
            lax.fori_loop(0, n_blk, tstart, 0)

            def twait(c, carry, zero_tile=zero_tile):
                zero_tile(c).wait()
                return carry
            lax.fori_loop(0, n_blk, twait, 0)


def dispatch_rows(h, pos, pad_ranges, npad, *, tg=256):
    t, d = h.shape
    grid_spec = pltpu.PrefetchScalarGridSpec(
        num_scalar_prefetch=2,
        grid=(t // tg,),
        in_specs=[pl.BlockSpec((tg, d), lambda i, pos, pad: (i, 0))],
        out_specs=pl.BlockSpec(memory_space=pl.ANY),
        scratch_shapes=[pltpu.VMEM((tg, d), F32), pltpu.SemaphoreType.DMA(())],
    )
    return pl.pallas_call(
        functools.partial(_dispatch_kernel, tg=tg, n_tok=t),
        grid_spec=grid_spec,
        out_shape=jax.ShapeDtypeStruct((npad, d), F32),
        compiler_params=_cparams(("arbitrary",)),
        name="dispatch_scatter",
    )(pos, pad_ranges, h)


def _combine_kernel(pos_ref, x_ref, rw_ref, ys_ref, o_ref, buf_ref, sem, *, tc, n_tok):
    base = pl.program_id(0) * tc

    def copy(k, r):
        return pltpu.make_async_copy(ys_ref.at[pl.ds(pos_ref[k * n_tok + base + r], 1), :],
                                     buf_ref.at[k, pl.ds(r, 1), :], sem)

    def start(r, carry):
        for k in range(TOP_K):
            copy(k, r).start()
        return carry
    lax.fori_loop(0, tc, start, 0, unroll=DMA_UNROLL)

    def wait(r, carry):
        for k in range(TOP_K):
            copy(k, r).wait()
        return carry
    lax.fori_loop(0, tc, wait, 0, unroll=DMA_UNROLL)
    rw = rw_ref[...]
    o_ref[...] = x_ref[...] + rw[:, 0:1] * buf_ref[0] + rw[:, 1:2] * buf_ref[1]


def combine_rows(x, rw, ys, pos, *, tc=256):
    t, d = x.shape
    grid_spec = pltpu.PrefetchScalarGridSpec(
        num_scalar_prefetch=1,
        grid=(t // tc,),
        in_specs=[pl.BlockSpec((tc, d), lambda i, pos: (i, 0)), pl.BlockSpec((tc, LANES), lambda i, pos: (i, 0)),
                  pl.BlockSpec(memory_space=pl.ANY)],
        out_specs=pl.BlockSpec((tc, d), lambda i, pos: (i, 0)),
        scratch_shapes=[pltpu.VMEM((TOP_K, tc, d), F32), pltpu.SemaphoreType.DMA(())],
    )
    return pl.pallas_call(
        functools.partial(_combine_kernel, tc=tc, n_tok=t),
        grid_spec=grid_spec,
        out_shape=jax.ShapeDtypeStruct((t, d), F32),
        compiler_params=_cparams(("arbitrary",)),
        name="combine_gather",
    )(pos, x, rw, ys)


def _routing_plan(ridx, rank, counts, tm, sub, npad):
    alloc = ((counts + tm - 1) // tm) * tm
    used = ((counts + sub - 1) // sub) * sub
    ends = jnp.cumsum(alloc)
    starts = ends - alloc
    hit = ridx[:, :, None] == jnp.arange(N_EXPERTS, dtype=jnp.int32)[None, None, :]
    pos = rank + jnp.sum(jnp.where(hit, starts[None, None, :], 0), axis=-1)
    n_tiles = (ends[-1] // tm).astype(jnp.int32).reshape(1)
    tile_row = jnp.arange(npad // tm, dtype=jnp.int32) * tm
    tile_expert = jnp.sum(jnp.minimum(tile_row, ends[-1] - tm)[:, None] >= ends[None, :], axis=-1).astype(jnp.int32)
    used_end = (starts + used)[tile_expert]
    tile_parts = jnp.where(tile_row < ends[-1], jnp.clip((used_end - tile_row) // sub, 0, tm // sub), 0)
    total = jnp.full((1,), npad, jnp.int32)
    pad_ranges = jnp.concatenate([starts + counts, starts + used,
                                  starts + used, ends[-1:], ends, total])
    return (pos.T.reshape(-1).astype(jnp.int32), pad_ranges.astype(jnp.int32), tile_expert,
            tile_parts.astype(jnp.int32), n_tiles)


def moe(h, x, ridx, rw, rank, counts, wg, wu, wd, *, tm=1024, sub=512, tf=256, tg=256):
    t, d = x.shape
    assert tm % sub == 0 and sub % tg == 0 and t % tg == 0
    npad = (TOP_K * t + N_EXPERTS * (tm - 1)) // tm * tm
    pos, pad_ranges, tile_expert, tile_parts, n_tiles = _routing_plan(
        ridx[:, :TOP_K], rank[:, :TOP_K], counts[0, :N_EXPERTS], tm, sub, npad)
    xs = dispatch_rows(h, pos, pad_ranges, npad, tg=tg)
    ys = moe_ffn(xs, tile_expert, tile_parts, n_tiles, wg, wu, wd, tm=tm, sub=sub, tf=tf)
    return combine_rows(x, rw, ys, pos, tc=tg)


def _split_w_in(w):
    sizes = (ATTN_WIDTH, ATTN_WIDTH, ATTN_WIDTH, 3 * GDN_WIDTH, GDN_WIDTH, GDN_HEADS, GDN_HEADS,
             SSM_WIDTH, SSM_XBC, SSM_HEADS)
    pieces, start = [], 0
    for size in sizes:
        pieces.append(w[:, start:start + size])
        start += size
    aq, ak, av, gqkv, ggate, gbeta, ga, sz, sxbc, sdt = pieces
    main = jnp.concatenate([gqkv, ggate, sz, sxbc, aq, ak, av], axis=1).astype(BF16)
    small = jnp.concatenate([gbeta, ga, sdt], axis=1)
    small = jnp.pad(small, ((0, 0), (0, LANES - small.shape[1]))).astype(BF16)
    return main, small


def kernel(x, rel_bias, mix_norm, w_in, attn_q_gain, attn_k_gain, attn_lambda, attn_sub_gain, gdn_conv_w, gdn_A_log, gdn_dt_bias, gdn_o_gain, ssm_conv_w, ssm_conv_b, ssm_A_log, ssm_dt_bias, ssm_D, ssm_norm_gain, w_out, ffn_norm, ffn_w_gate, ffn_w_up, ffn_w_down, moe_router, moe_w_gate, moe_w_up, moe_w_down):
    b, s, d = x.shape
    t = b * s
    depth = w_in.shape[0]
    tiles = _tile_plan(t, s)
    bias_tiles = _bias_tiles(rel_bias, tiles["attn_tq"])
    xf = x.reshape(t, d).astype(F32)
    for li in range(depth):
        w_main, w_small = _split_w_in(w_in[li])
        proj, small = norm_inproj(xf, mix_norm[li].astype(F32), w_main, w_small, tm=tiles["proj_tm"])
        proj3 = proj.reshape(b, s, N_MAIN)
        small3 = small.reshape(b, s, LANES)
        attn_o = attention(proj3, attn_q_gain[li], attn_k_gain[li], attn_lambda[li], attn_sub_gain[li],
                           bias_tiles, li, tq=tiles["attn_tq"])
        gdn_o = gdn(proj3, small3, gdn_conv_w[li], gdn_A_log[li], gdn_dt_bias[li], gdn_o_gain[li],
                    tile=tiles["scan_tile"])
        ssm_o = ssd(proj3, small3, ssm_conv_w[li], ssm_conv_b[li], ssm_A_log[li], ssm_dt_bias[li],
                    ssm_D[li], ssm_norm_gain[li], tile=tiles["scan_tile"])
        wo = w_out[li].astype(BF16)
        mix = (attn_o.reshape(t, ATTN_WIDTH), gdn_o.reshape(t, GDN_WIDTH), ssm_o.reshape(t, SSM_WIDTH))
        j = li // 2
        if li % 2 == 0:
            xf, h = outproj(*mix, wo, xf, ffn_norm[li], tm=tiles["proj_tm"])
            xf = ffn(h, xf, ffn_w_gate[j], ffn_w_up[j], ffn_w_down[j], tm=tiles["ffn_tm"])
        else:
            w_r = jnp.pad(moe_router[j], ((0, 0), (0, LANES - N_EXPERTS))).astype(BF16)
            xf, h, ridx, rw, rank, counts = outproj(*mix, wo, xf, ffn_norm[li], w_r, tm=tiles["proj_tm"])
            xf = moe(h, xf, ridx, rw, rank, counts, moe_w_gate[j], moe_w_up[j], moe_w_down[j],
                     tm=tiles["moe_tm"], sub=tiles["moe_tm"] // 2, tg=tiles["row_dma_tile"])
    return xf.reshape(b, s, d).astype(x.dtype)
```

```python
import functools
import math

import jax
import jax.numpy as jnp
from jax import lax
from jax.experimental import pallas as pl
from jax.experimental.pallas import tpu as pltpu

F32 = jnp.float32
BF16 = jnp.bfloat16

NORM_EPS = 1e-6
CONV_K = 4
LANES = 128
HALO = 8

ATTN_HEADS = 4
ATTN_QK_DIM = 64
ATTN_V_DIM = 128
ATTN_WIDTH = 512
REL_BUCKETS = 32
REL_MAX_DIST = 128

GDN_HEADS = 6
GDN_DIM = 128
GDN_WIDTH = 768

SSM_HEADS = 12
SSM_HEAD_DIM = 64
SSM_GROUPS = 2
SSM_STATE = 128
SSM_WIDTH = 768
SSM_XBC = 1280

N_EXPERTS = 8
TOP_K = 2

GDN_COLS = 4 * GDN_WIDTH
SZ_BLOCK = 4
XBC_BLOCK = 3
ATTN_BLOCK0 = (5 * GDN_WIDTH + SSM_XBC) // LANES
N_MAIN = 5 * GDN_WIDTH + SSM_XBC + 3 * ATTN_WIDTH
BETA_LANE0, GA_LANE0, DT_LANE0 = 0, GDN_HEADS, 2 * GDN_HEADS

VMEM_BYTES = 64 * 1024 * 1024
VMEM_LIMIT = VMEM_BYTES - 8 * 1024 * 1024


def _cparams(sem):
    return pltpu.CompilerParams(dimension_semantics=sem, vmem_limit_bytes=VMEM_LIMIT)


def _tile_plan(t, s):
    moe_tm = min(1024, t // 2)
    return dict(
        proj_tm=min(512, t),
        attn_tq=min(512, s),
        scan_tile=min(256, s),
        ffn_tm=min(1024, t),
        moe_tm=moe_tm,
        row_dma_tile=min(1024, moe_tm),
    )


def _dot(a, b):
    return jnp.dot(a.astype(BF16), b.astype(BF16), preferred_element_type=F32)


def _dot_nt(a, b):
    return lax.dot_general(a.astype(BF16), b.astype(BF16), (((1,), (1,)), ((), ())),
                           preferred_element_type=F32)


def _dot_tn(a, b):
    return lax.dot_general(a.astype(BF16), b.astype(BF16), (((0,), (0,)), ((), ())),
                           preferred_element_type=F32)


def _dot_f32(a, b):
    return jnp.dot(a, b, preferred_element_type=F32, precision=lax.Precision.HIGHEST)


def _sigmoid(x):
    return 0.5 * jnp.tanh(0.5 * x) + 0.5


def _silu(x):
    return x * _sigmoid(x)


def _softplus(x):
    return jnp.maximum(x, 0.0) + jnp.log(1.0 + jnp.exp(-jnp.abs(x)))


def _rms(x):
    return x * lax.rsqrt(jnp.mean(x * x, axis=-1, keepdims=True) + NORM_EPS)


def _tril(n, strict=False):
    r = lax.broadcasted_iota(jnp.int32, (n, n), 0)
    c = lax.broadcasted_iota(jnp.int32, (n, n), 1)
    return (r > c) if strict else (r >= c)


def _norm_inproj_kernel(x_ref, g_ref, w_ref, ws_ref, o_ref, os_ref, h_ref):
    @pl.when(pl.program_id(1) == 0)
    def _():
        h = (_rms(x_ref[...]) * g_ref[...]).astype(BF16)
        h_ref[...] = h
        os_ref[...] = jnp.dot(h, ws_ref[...], preferred_element_type=F32)

    o_ref[...] = jnp.dot(h_ref[...], w_ref[...], preferred_element_type=F32).astype(o_ref.dtype)


def norm_inproj(x, gain, w_main, w_small, *, tm=512, tn=3328):
    t, d = x.shape
    n = w_main.shape[1]
    return pl.pallas_call(
        _norm_inproj_kernel,
        grid=(t // tm, n // tn),
        in_specs=[
            pl.BlockSpec((tm, d), lambda i, j: (i, 0)),
            pl.BlockSpec((1, d), lambda i, j: (0, 0)),
            pl.BlockSpec((d, tn), lambda i, j: (0, j)),
            pl.BlockSpec((d, LANES), lambda i, j: (0, 0)),
        ],
        out_specs=[
            pl.BlockSpec((tm, tn), lambda i, j: (i, j)),
            pl.BlockSpec((tm, LANES), lambda i, j: (i, 0)),
        ],
        out_shape=[jax.ShapeDtypeStruct((t, n), BF16), jax.ShapeDtypeStruct((t, LANES), F32)],
        scratch_shapes=[pltpu.VMEM((tm, d), BF16)],
        compiler_params=_cparams(("parallel", "arbitrary")),
        name="norm_inproj",
    )(x, gain.reshape(1, d), w_main, w_small)


def _pair_rms(x, gain):
    lo = lax.broadcasted_iota(jnp.int32, (1, LANES), 1) < ATTN_QK_DIM
    sq = x * x
    s_lo = jnp.sum(jnp.where(lo, sq, 0.0), axis=-1, keepdims=True)
    s_hi = jnp.sum(jnp.where(lo, 0.0, sq), axis=-1, keepdims=True)
    ms = jnp.where(lo, s_lo, s_hi) * (1.0 / ATTN_QK_DIM)
    return x * lax.rsqrt(ms + NORM_EPS) * gain


LOG2_E = math.log2(math.e)
ONES_ROWS = 16


def _attn_kernel(lam_ref, qg_ref, kg_ref, sg_ref, bias_ref, q_ref, k_ref, v_ref, o_ref,
                 kn_ref, vt_ref, m_ref, acc_ref, sa_ref, sb_ref, *, tq, lam_init):
    qi = pl.program_id(2)
    lo = lax.broadcasted_iota(jnp.int32, (1, LANES), 1) < ATTN_QK_DIM
    nv = ATTN_V_DIM

    @pl.when(qi == 0)
    def _():
        def body(c, carry):
            rows = pl.ds(pl.multiple_of(c * tq, tq), tq)
            kn_ref[rows, :] = _pair_rms(k_ref[rows, :].astype(F32), kg_ref[...]).astype(BF16)
            vt_ref[c, 0:nv, :] = v_ref[rows, :].astype(F32).T.astype(BF16)
            vt_ref[c, nv:nv + ONES_ROWS, :] = jnp.ones((ONES_ROWS, tq), BF16)
            return carry
        lax.fori_loop(0, k_ref.shape[0] // tq, body, 0)

    q = _pair_rms(q_ref[...].astype(F32), qg_ref[...]) * (ATTN_QK_DIM ** -0.5 * LOG2_E)
    qz = (jnp.where(lo, q, 0.0).astype(BF16), jnp.where(lo, 0.0, q).astype(BF16))

    maps = range(2)

    def scores(kb, n, bias):
        k_blk = kn_ref[pl.ds(pl.multiple_of(kb * tq, tq), n * tq), :]
        s = [lax.dot_general(k_blk, qz[mp], (((1,), (1,)), ((), ())), preferred_element_type=F32) for mp in maps]
        return s if bias is None else [x + bias for x in s]

    def update(s, kb, n, first):
        vt = vt_ref[kb] if n == 1 else jnp.concatenate([vt_ref[kb + i] for i in range(n)], axis=-1)
        s_max = [jnp.max(x, axis=0, keepdims=True) for x in s]
        if first:
            m_new = s_max
        else:
            m_prev = [m_ref[mp] for mp in maps]
            m_new = [jnp.maximum(m_prev[mp], s_max[mp]) for mp in maps]
            alpha = [jnp.exp2(m_prev[mp] - m_new[mp]) for mp in maps]
        p = [jnp.exp2(s[mp] - m_new[mp]).astype(BF16) for mp in maps]
        pv = [jnp.dot(vt, p[mp], preferred_element_type=F32) for mp in maps]
        for mp in maps:
            acc_ref[mp] = pv[mp] if first else alpha[mp] * acc_ref[mp] + pv[mp]
            m_ref[mp] = m_new[mp]

    @pl.when(qi == 0)
    def _():
        update(scores(qi, 1, bias_ref[1]), qi, 1, True)

    n_far = jnp.maximum(qi - 1, 0)
    n_groups = n_far // 2
    bufs = (sa_ref, sb_ref)

    def fill(buf, k, near):
        s = scores(qi - 1, 2, bias_ref[...].reshape(2 * tq, tq)) if near else scores((k - 1) * 2, 2, None)
        for mp in maps:
            buf[mp] = s[mp]

    def drain(buf, k, near):
        update([buf[mp] for mp in maps], (qi - 1) if near else (k - 1) * 2, 2, near)

    def stage(k, cur, nxt, near=False):
        @pl.when(k < n_groups)
        def _():
            fill(nxt, k + 1, False)
            drain(cur, k, near)

        @pl.when(k == n_groups)
        def _():
            drain(cur, k, near)

    @pl.when(qi >= 1)
    def _():
        fill(bufs[0], 0, True)
        stage(0, bufs[0], bufs[1], near=True)

        def far(kk, carry):
            stage(2 * kk + 1, bufs[1], bufs[0])
            stage(2 * kk + 2, bufs[0], bufs[1])
            return carry
        lax.fori_loop(0, (n_groups + 1) // 2, far, 0)

    @pl.when(n_far - 2 * n_groups == 1)
    def _():
        update(scores(n_far - 1, 1, None), n_far - 1, 1, False)

    lam = lam_ref[...]
    lam_full = (jnp.exp(jnp.sum(lam[0:1] * lam[1:2], axis=-1, keepdims=True))
                - jnp.exp(jnp.sum(lam[2:3] * lam[3:4], axis=-1, keepdims=True)) + lam_init)
    a0, a1 = acc_ref[0], acc_ref[1]
    o = a0[0:nv] / a0[nv:nv + 1] - lam_full * (a1[0:nv] / a1[nv:nv + 1])
    o = o * lax.rsqrt(jnp.mean(o * o, axis=0, keepdims=True) + NORM_EPS)
    o_ref[...] = (o.T * sg_ref[...] * (1.0 - lam_init)).astype(o_ref.dtype)


def _t5_bucket(n):
    max_exact = REL_BUCKETS // 2
    nf = jnp.maximum(n, max_exact).astype(F32)
    large = max_exact + (jnp.log(nf / max_exact) / math.log(REL_MAX_DIST / max_exact)
                         * (REL_BUCKETS - max_exact)).astype(jnp.int32)
    return jnp.where(n < max_exact, n, jnp.minimum(large, REL_BUCKETS - 1))


def _bias_tiles(rel_bias, tq):
    assert tq >= REL_MAX_DIST
    heads = rel_bias.shape[1]

    def body(tab_ref, o_ref):
        h = pl.program_id(0)
        r = lax.broadcasted_iota(jnp.int32, (2 * tq, tq), 0)
        i = lax.broadcasted_iota(jnp.int32, (2 * tq, tq), 1)
        rel = tq + i - r
        bucket = _t5_bucket(jnp.maximum(rel, 0))
        far = tab_ref[h, REL_BUCKETS - 1]
        bias = jnp.zeros((2 * tq, tq), F32)
        for bkt in range(REL_BUCKETS - 1):
            bias = jnp.where(bucket == bkt, tab_ref[h, bkt] - far, bias)
        o_ref[...] = jnp.where(rel >= 0, bias * LOG2_E, -jnp.inf)

    tiles = pl.pallas_call(
        body,
        grid=(heads,),
        in_specs=[pl.BlockSpec(memory_space=pltpu.SMEM)],
        out_specs=pl.BlockSpec((None, 2 * tq, tq), lambda h: (h, 0, 0)),
        out_shape=jax.ShapeDtypeStruct((heads, 2 * tq, tq), F32),
        compiler_params=_cparams(("parallel",)),
        name="t5_bias_tiles",
    )(rel_bias.astype(F32).T)
    return tiles.reshape(heads, 2, tq, tq)


def attention(proj3, q_gain, k_gain, lam, sub_gain, bias_tiles, layer_idx, *, tq=512):
    b, s, _ = proj3.shape
    h = ATTN_HEADS
    lam_init = 0.8 - 0.6 * math.exp(-0.3 * layer_idx)
    qg = jnp.tile(q_gain.astype(F32), 2).reshape(1, LANES)
    kg = jnp.tile(k_gain.astype(F32), 2).reshape(1, LANES)
    sg = sub_gain.astype(F32).reshape(1, LANES)
    kern = functools.partial(_attn_kernel, tq=tq, lam_init=lam_init)
    const = lambda shape: pl.BlockSpec(shape, lambda bi, hi, qi: (0,) * len(shape))
    return pl.pallas_call(
        kern,
        grid=(b, h, s // tq),
        in_specs=[
            const((4, ATTN_QK_DIM)), const((1, LANES)), const((1, LANES)), const((1, LANES)),
            pl.BlockSpec((None, 2, tq, tq), lambda bi, hi, qi: (hi, 0, 0, 0)),
            pl.BlockSpec((None, tq, LANES), lambda bi, hi, qi: (bi, qi, ATTN_BLOCK0 + hi)),
            pl.BlockSpec((None, s, LANES), lambda bi, hi, qi: (bi, 0, ATTN_BLOCK0 + h + hi)),
            pl.BlockSpec((None, s, LANES), lambda bi, hi, qi: (bi, 0, ATTN_BLOCK0 + 2 * h + hi)),
        ],
        out_specs=pl.BlockSpec((None, tq, LANES), lambda bi, hi, qi: (bi, qi, hi)),
        out_shape=jax.ShapeDtypeStruct((b, s, ATTN_WIDTH), BF16),
        scratch_shapes=[
            pltpu.VMEM((s, LANES), BF16),
            pltpu.VMEM((s // tq, ATTN_V_DIM + ONES_ROWS, tq), BF16),
            pltpu.VMEM((2, 1, tq), F32),
            pltpu.VMEM((2, ATTN_V_DIM + ONES_ROWS, tq), F32),
            pltpu.VMEM((2, 2 * tq, tq), F32),
            pltpu.VMEM((2, 2 * tq, tq), F32),
        ],
        compiler_params=_cparams(("parallel", "parallel", "arbitrary")),
        name="diff_attention",
    )(lam.astype(F32), qg, kg, sg, bias_tiles, proj3, proj3, proj3)


def _causal_conv(x, w_ref, buf_ref, first):
    rows = x.shape[0]

    @pl.when(first)
    def _():
        buf_ref[0:HALO, :] = jnp.zeros((HALO, x.shape[1]), F32)

    buf_ref[HALO:HALO + rows, :] = x
    y = w_ref[CONV_K - 1:CONV_K, :] * x
    for j in range(CONV_K - 1):
        off = HALO - (CONV_K - 1) + j
        y = y + w_ref[j:j + 1, :] * buf_ref[off:off + rows, :]
    buf_ref[0:HALO, :] = x[rows - HALO:, :]
    return y


def _unit_lower_inverses(mats):
    n = mats[0].shape[0]
    eye = (lax.broadcasted_iota(jnp.int32, (n, n), 0)
           == lax.broadcasted_iota(jnp.int32, (n, n), 1)).astype(F32)
    ps = [eye - a for a in mats]
    bs = [_dot(a, a) for a in mats]
    steps = int(math.log2(n)) - 1
    for i in range(steps):
        if i + 1 < steps:
            both = [_dot(jnp.concatenate([p, b], axis=0), b) for p, b in zip(ps, bs)]
            ps = [p + x[:n] for p, x in zip(ps, both)]
            bs = [x[n:] for x in both]
        else:
            ps = [p + _dot(p, b) for p, b in zip(ps, bs)]
    return ps


def _gdn_kernel(blk_ref, sm_ref, cw_ref, arow_ref, dtb_ref, og_ref, o_ref, buf_ref, st_ref, *, chunk):
    first = pl.program_id(1) == 0
    c = chunk
    d = GDN_DIM
    nq = 3 * GDN_WIDTH
    tile = blk_ref.shape[0]
    heads = range(GDN_HEADS)
    chunks = range(tile // c)

    @pl.when(first)
    def _():
        st_ref[...] = jnp.zeros(st_ref.shape, F32)

    qkv = _silu(_causal_conv(blk_ref[:, 0:nq].astype(F32), cw_ref, buf_ref, first))
    sm = sm_ref[...]
    beta = _sigmoid(sm)
    g = arow_ref[...] * _softplus(sm + dtb_ref[...])
    strict = _tril(c, strict=True)
    causal = _tril(c)
    r = lax.broadcasted_iota(jnp.int32, (tile, tile), 0)
    cc = lax.broadcasted_iota(jnp.int32, (tile, tile), 1)
    same_chunk_tril = jnp.logical_and(r >= cc, (r // c) == (cc // c))
    gc = _dot_f32(same_chunk_tril.astype(F32), g)
    gct = gc.T

    l2 = lambda x: x * lax.rsqrt(jnp.sum(x * x, axis=-1, keepdims=True) + NORM_EPS)
    qn = [l2(qkv[:, h * d:(h + 1) * d]) * (d ** -0.5) for h in heads]
    kn = [l2(qkv[:, GDN_WIDTH + h * d:GDN_WIDTH + (h + 1) * d]) for h in heads]

    pairs = [(ci, h) for ci in chunks for h in heads]
    prep = {}
    for ci, h in pairs:
        rows = slice(ci * c, (ci + 1) * c)
        q, k = qn[h][rows], kn[h][rows]
        v = qkv[rows, 2 * GDN_WIDTH + h * d:2 * GDN_WIDTH + (h + 1) * d]
        bcol = beta[rows, BETA_LANE0 + h:BETA_LANE0 + h + 1]
        col = gc[rows, GA_LANE0 + h:GA_LANE0 + h + 1]
        row = gct[GA_LANE0 + h:GA_LANE0 + h + 1, rows]
        last = gc[(ci + 1) * c - 1:(ci + 1) * c, GA_LANE0 + h:GA_LANE0 + h + 1]
        dec = jnp.exp(jnp.where(causal, col - row, -jnp.inf))
        ecol = jnp.exp(col)
        kb = k * bcol
        prep[ci, h] = dict(k=k, kb=kb, dec=dec, last=last,
                           lhs=jnp.concatenate([kb, q], axis=0),
                           rhs=jnp.concatenate([v * bcol, kb * ecol], axis=-1),
                           qd=q * ecol, kd=k * jnp.exp(last - col))
    for key in pairs:
        x = prep[key]
        both = _dot_nt(x["lhs"], x["k"])
        x["a"] = both[:c] * jnp.where(strict, x["dec"], 0.0)
        x["qk"] = both[c:] * x["dec"]
    t_inv = _unit_lower_inverses([prep[key]["a"] for key in pairs])
    for key, t in zip(pairs, t_inv):
        prep[key]["sol"] = _dot(t, prep[key]["rhs"])

    states = [st_ref[h] for h in heads]
    for ci in chunks:
        xs = [prep[ci, h] for h in heads]
        both = [_dot(jnp.concatenate([x["sol"][:, d:], x["qd"]], axis=0), st) for x, st in zip(xs, states)]
        v_new = [x["sol"][:, :d] - y[:c] for x, y in zip(xs, both)]
        inter = [_dot(x["qk"], vn) for x, vn in zip(xs, v_new)]
        upd = [_dot_tn(x["kd"], vn) for x, vn in zip(xs, v_new)]
        states = [st * jnp.exp(x["last"]) + u for st, x, u in zip(states, xs, upd)]
        rows = slice(ci * c, (ci + 1) * c)
        for h in heads:
            o = both[h][c:] + inter[h]
            gate = blk_ref[rows, nq + h * d:nq + (h + 1) * d].astype(F32)
            o_ref[rows, h * d:(h + 1) * d] = (_rms(o) * og_ref[...] * _silu(gate)).astype(o_ref.dtype)
    for h in heads:
        st_ref[h] = states[h]


def gdn(proj3, small3, conv_w, a_log, dt_bias, o_gain, *, chunk=64, tile=256):
    b, s, _ = proj3.shape
    tile = min(tile, s)
    pad = lambda v, lane0: jnp.zeros((1, LANES), F32).at[0, lane0:lane0 + v.shape[0]].set(v.astype(F32))
    arow = pad(-jnp.exp(a_log.astype(F32)), GA_LANE0)
    dtb = pad(dt_bias, GA_LANE0)
    const = lambda shape: pl.BlockSpec(shape, lambda bi, si: (0,) * len(shape))
    return pl.pallas_call(
        functools.partial(_gdn_kernel, chunk=chunk),
        grid=(b, s // tile),
        in_specs=[
            pl.BlockSpec((None, tile, GDN_COLS), lambda bi, si: (bi, si, 0)),
            pl.BlockSpec((None, tile, LANES), lambda bi, si: (bi, si, 0)),
            const((CONV_K, 3 * GDN_WIDTH)), const((1, LANES)), const((1, LANES)), const((1, GDN_DIM)),
        ],
        out_specs=pl.BlockSpec((None, tile, GDN_WIDTH), lambda bi, si: (bi, si, 0)),
        out_shape=jax.ShapeDtypeStruct((b, s, GDN_WIDTH), BF16),
        scratch_shapes=[
            pltpu.VMEM((HALO + tile, 3 * GDN_WIDTH), F32),
            pltpu.VMEM((GDN_HEADS, GDN_DIM, GDN_DIM), F32),
        ],
        compiler_params=_cparams(("parallel", "arbitrary")),
        name="gated_deltanet",
    )(proj3, small3, conv_w.astype(F32), arow, dtb, o_gain.astype(F32).reshape(1, GDN_DIM))


def _ssd_kernel(xbc_ref, z_ref, sm_ref, cw_ref, cb_ref, arow_ref, dtb_ref, dsk_ref, ng_ref, o_ref,
                buf_ref, st_ref, y_ref, *, chunk):
    first = pl.program_id(1) == 0
    c = chunk
    p2 = 2 * SSM_HEAD_DIM
    heads_per_group = SSM_HEADS // SSM_GROUPS
    gw = SSM_WIDTH // SSM_GROUPS

    @pl.when(first)
    def _():
        st_ref[...] = jnp.zeros(st_ref.shape, F32)

    tile = xbc_ref.shape[0]
    chunks = range(tile // c)
    pairs = range(SSM_HEADS // 2)
    xbc = _silu(_causal_conv(xbc_ref[...].astype(F32), cw_ref, buf_ref, first) + cb_ref[...])
    x = xbc[:, :SSM_WIDTH]
    sm = sm_ref[...]
    dt = _softplus(sm + dtb_ref[...])
    causal = _tril(c)
    r = lax.broadcasted_iota(jnp.int32, (tile, tile), 0)
    cc = lax.broadcasted_iota(jnp.int32, (tile, tile), 1)
    same_chunk_tril = jnp.logical_and(r >= cc, (r // c) == (cc // c))
    acum = _dot_f32(same_chunk_tril.astype(F32), dt * arow_ref[...])
    acum_t = acum.T
    lo = lax.broadcasted_iota(jnp.int32, (1, p2), 1) < SSM_HEAD_DIM
    halves = lambda lane, arr: (arr[:, lane:lane + 1], arr[:, lane + 1:lane + 2])
    sel = lambda pair: jnp.where(lo, pair[0], pair[1])

    bms, cms, cbs = {}, {}, {}
    for ci in chunks:
        rows = slice(ci * c, (ci + 1) * c)
        for grp in range(SSM_GROUPS):
            bms[ci, grp] = xbc[rows, SSM_WIDTH + grp * SSM_STATE:SSM_WIDTH + (grp + 1) * SSM_STATE]
            cms[ci, grp] = xbc[rows, SSM_WIDTH + (SSM_GROUPS + grp) * SSM_STATE:
                               SSM_WIDTH + (SSM_GROUPS + grp + 1) * SSM_STATE]
            cbs[ci, grp] = _dot_nt(cms[ci, grp], bms[ci, grp])
    units = [(ci, pr) for ci in chunks for pr in pairs]
    prep = {}
    for ci, pr in units:
        rows = slice(ci * c, (ci + 1) * c)
        grp = (2 * pr) // heads_per_group
        lane = DT_LANE0 + 2 * pr
        cols = halves(lane, acum[rows])
        lasts = halves(lane, acum[(ci + 1) * c - 1:(ci + 1) * c, :])
        xdt = x[rows, pr * p2:(pr + 1) * p2] * sel(halves(lane, dt[rows]))
        lmats = [jnp.exp(jnp.where(causal, cols[hh] - acum_t[lane + hh:lane + hh + 1, rows], -jnp.inf))
                 for hh in range(2)]
        prep[ci, pr] = dict(
            grp=grp, xdt=xdt, lhs=[cbs[ci, grp] * lm for lm in lmats],
            out_scale=sel((jnp.exp(cols[0]), jnp.exp(cols[1]))),
            xdt_in=xdt * sel((jnp.exp(lasts[0] - cols[0]), jnp.exp(lasts[1] - cols[1]))),
            keep=sel((jnp.exp(lasts[0]), jnp.exp(lasts[1]))))
    for key in units:
        u = prep[key]
        u["y_diag"] = jnp.where(lo, _dot(u["lhs"][0], u["xdt"]), _dot(u["lhs"][1], u["xdt"]))

    states = [st_ref[pr] for pr in pairs]
    for ci in chunks:
        rows = slice(ci * c, (ci + 1) * c)
        us = [prep[ci, pr] for pr in pairs]
        y_off = [_dot(cms[ci, u["grp"]], st) * u["out_scale"] for u, st in zip(us, states)]
        upd = [_dot_tn(bms[ci, u["grp"]], u["xdt_in"]) for u in us]
        states = [st * u["keep"] + d for st, u, d in zip(states, us, upd)]
        for pr in pairs:
            y_ref[rows, pr * p2:(pr + 1) * p2] = us[pr]["y_diag"] + y_off[pr]
    for pr in pairs:
        st_ref[pr] = states[pr]

    y = (y_ref[...] + dsk_ref[...] * x) * _silu(z_ref[...].astype(F32))
    for grp in range(SSM_GROUPS):
        cols = slice(grp * gw, (grp + 1) * gw)
        o_ref[:, cols] = (_rms(y[:, cols]) * ng_ref[:, cols]).astype(o_ref.dtype)


def ssd(proj3, small3, conv_w, conv_b, a_log, dt_bias, d_skip, norm_gain, *, chunk=128, tile=256):
    b, s, _ = proj3.shape
    tile = min(tile, s)
    pad = lambda v: jnp.zeros((1, LANES), F32).at[0, DT_LANE0:DT_LANE0 + SSM_HEADS].set(v.astype(F32))
    arow = pad(-jnp.exp(a_log.astype(F32)))
    dtb = pad(dt_bias)
    dsk = jnp.repeat(d_skip.astype(F32), SSM_HEAD_DIM).reshape(1, SSM_WIDTH)
    const = lambda shape: pl.BlockSpec(shape, lambda bi, si: (0,) * len(shape))
    return pl.pallas_call(
        functools.partial(_ssd_kernel, chunk=chunk),
        grid=(b, s // tile),
        in_specs=[
            pl.BlockSpec((None, tile, SSM_XBC), lambda bi, si: (bi, si, XBC_BLOCK)),
            pl.BlockSpec((None, tile, SSM_WIDTH), lambda bi, si: (bi, si, SZ_BLOCK)),
            pl.BlockSpec((None, tile, LANES), lambda bi, si: (bi, si, 0)),
            const((CONV_K, SSM_XBC)), const((1, SSM_XBC)), const((1, LANES)), const((1, LANES)),
            const((1, SSM_WIDTH)), const((1, SSM_WIDTH)),
        ],
        out_specs=pl.BlockSpec((None, tile, SSM_WIDTH), lambda bi, si: (bi, si, 0)),
        out_shape=jax.ShapeDtypeStruct((b, s, SSM_WIDTH), BF16),
        scratch_shapes=[
            pltpu.VMEM((HALO + tile, SSM_XBC), F32),
            pltpu.VMEM((SSM_HEADS // 2, SSM_STATE, 2 * SSM_HEAD_DIM), F32),
            pltpu.VMEM((tile, SSM_WIDTH), F32),
        ],
        compiler_params=_cparams(("parallel", "arbitrary")),
        name="mamba2_ssd",
    )(proj3, proj3, small3, conv_w.astype(F32), conv_b.astype(F32).reshape(1, SSM_XBC), arow, dtb, dsk,
      norm_gain.astype(F32).reshape(1, SSM_WIDTH))


def _outproj_kernel(*refs, with_router):
    if with_router:
        (a_ref, g_ref, s_ref, w_ref, x_ref, ng_ref, wr_ref,
         xo_ref, h_ref, ri_ref, rw_ref, rk_ref, cnt_ref, run_ref) = refs
    else:
        a_ref, g_ref, s_ref, w_ref, x_ref, ng_ref, xo_ref, h_ref = refs
    g0, s0 = ATTN_WIDTH, ATTN_WIDTH + GDN_WIDTH
    y = (jnp.dot(a_ref[...], w_ref[0:g0, :], preferred_element_type=F32)
         + jnp.dot(g_ref[...], w_ref[g0:s0, :], preferred_element_type=F32)
         + jnp.dot(s_ref[...], w_ref[s0:, :], preferred_element_type=F32))
    xn = x_ref[...] + y
    xo_ref[...] = xn
    hf = _rms(xn) * ng_ref[...]
    h = hf.astype(BF16)
    h_ref[...] = hf.astype(h_ref.dtype)
    if with_router:
        tm = xn.shape[0]
        lane = lax.broadcasted_iota(jnp.int32, (1, LANES), 1)
        logits = jnp.where(lane < N_EXPERTS, jnp.dot(h, wr_ref[...], preferred_element_type=F32), -jnp.inf)
        v1 = jnp.max(logits, axis=-1, keepdims=True)
        i1 = jnp.min(jnp.where(logits == v1, lane, LANES), axis=-1, keepdims=True)
        rest = jnp.where(lane == i1, -jnp.inf, logits)
        v2 = jnp.max(rest, axis=-1, keepdims=True)
        i2 = jnp.min(jnp.where(rest == v2, lane, LANES), axis=-1, keepdims=True)
        e2 = jnp.exp(v2 - v1)
        ri_ref[...] = jnp.where(lane == 0, i1, i2)
        rw_ref[...] = jnp.where(lane == 0, 1.0 / (1.0 + e2), e2 / (1.0 + e2))

        @pl.when(pl.program_id(0) == 0)
        def _():
            run_ref[...] = jnp.zeros(run_ref.shape, F32)

        before = _tril(tm, strict=True).astype(BF16)
        run = run_ref[0:1, :]
        ranks = []
        for idx in (i1, i2):
            hit = lane == idx
            onehot = hit.astype(F32)
            earlier = jnp.dot(before, onehot.astype(BF16), preferred_element_type=F32) + run
            ranks.append(jnp.sum(jnp.where(hit, earlier, 0.0), axis=-1, keepdims=True))
            run = run + jnp.sum(onehot, axis=0, keepdims=True)
        rk_ref[...] = jnp.where(lane == 0, ranks[0], ranks[1]).astype(jnp.int32)
        run_ref[...] = jnp.broadcast_to(run, run_ref.shape)
        cnt_ref[...] = jnp.broadcast_to(run, cnt_ref.shape).astype(jnp.int32)


def outproj(attn_o, gdn_o, ssm_o, w_out, x, norm_gain, w_router=None, *, tm=512):
    t, d = x.shape
    with_router = w_router is not None
    row = lambda width: pl.BlockSpec((tm, width), lambda i: (i, 0))
    const = lambda shape: pl.BlockSpec(shape, lambda i: (0, 0))
    in_specs = [row(ATTN_WIDTH), row(GDN_WIDTH), row(SSM_WIDTH), const(w_out.shape), row(d), const((1, d))]
    args = [attn_o, gdn_o, ssm_o, w_out, x, norm_gain.astype(F32).reshape(1, d)]
    out_specs = [row(d), row(d)]
    out_shape = [jax.ShapeDtypeStruct((t, d), F32), jax.ShapeDtypeStruct((t, d), F32 if with_router else BF16)]
    scratch = []
    if with_router:
        in_specs.append(const((d, LANES)))
        args.append(w_router)
        out_specs += [row(LANES), row(LANES), row(LANES), const((8, LANES))]
        out_shape += [jax.ShapeDtypeStruct((t, LANES), jnp.int32), jax.ShapeDtypeStruct((t, LANES), F32),
                      jax.ShapeDtypeStruct((t, LANES), jnp.int32), jax.ShapeDtypeStruct((8, LANES), jnp.int32)]
        scratch = [pltpu.VMEM((8, LANES), F32)]
    return pl.pallas_call(
        functools.partial(_outproj_kernel, with_router=with_router),
        grid=(t // tm,),
        in_specs=in_specs, out_specs=out_specs, out_shape=out_shape, scratch_shapes=scratch,
        compiler_params=_cparams(("arbitrary" if with_router else "parallel",)),
        name="outproj_router" if with_router else "outproj",
    )(*args)


def _ffn_kernel(h_ref, x_ref, wg_ref, wu_ref, wd_ref, o_ref, wgu_ref, wdb_ref, *, sub):
    @pl.when(pl.program_id(1) == 0)
    def _():
        o_ref[...] = x_ref[...]

    tf = wg_ref.shape[1]
    wgu_ref[:, 0:tf] = wg_ref[...].astype(BF16)
    wgu_ref[:, tf:2 * tf] = wu_ref[...].astype(BF16)
    wdb_ref[...] = wd_ref[...].astype(BF16)
    for part in range(o_ref.shape[0] // sub):
        rows = slice(part * sub, (part + 1) * sub)
        gu = jnp.dot(h_ref[rows, :], wgu_ref[...], preferred_element_type=F32)
        a = (_silu(gu[:, :tf]) * gu[:, tf:]).astype(BF16)
        o_ref[rows, :] += jnp.dot(a, wdb_ref[...], preferred_element_type=F32)


def ffn(h, x, wg, wu, wd, *, tm=1024, tf=256):
    t, d = x.shape
    f = wg.shape[1]
    return pl.pallas_call(
        functools.partial(_ffn_kernel, sub=min(512, tm)),
        grid=(t // tm, f // tf),
        in_specs=[
            pl.BlockSpec((tm, d), lambda i, j: (i, 0)),
            pl.BlockSpec((tm, d), lambda i, j: (i, 0), pipeline_mode=pl.Buffered(1)),
            pl.BlockSpec((d, tf), lambda i, j: (0, j)),
            pl.BlockSpec((d, tf), lambda i, j: (0, j)),
            pl.BlockSpec((tf, d), lambda i, j: (j, 0)),
        ],
        out_specs=pl.BlockSpec((tm, d), lambda i, j: (i, 0)),
        out_shape=jax.ShapeDtypeStruct((t, d), F32),
        scratch_shapes=[pltpu.VMEM((d, 2 * tf), BF16), pltpu.VMEM((tf, d), BF16)],
        compiler_params=_cparams(("parallel", "arbitrary")),
        name="ffn_swiglu",
    )(h, x, wg, wu, wd)


def _moe_ffn_kernel(te_ref, np_ref, nt_ref, h_ref, wg_ref, wu_ref, wd_ref, o_ref, hb_ref, wgu_ref, wdb_ref,
                    *, sub):
    i = pl.program_id(0)
    j = pl.program_id(1)
    parts = o_ref.shape[0] // sub
    tf = wg_ref.shape[1]
    n_valid = np_ref[i]

    @pl.when(j == 0)
    def _():
        o_ref[...] = jnp.zeros(o_ref.shape, F32)

    for part in range(parts):
        rows = slice(part * sub, (part + 1) * sub)

        @pl.when(jnp.logical_and(n_valid > part, j == 0))
        def _(rows=rows):
            hb_ref[rows, :] = h_ref[rows, :].astype(BF16)

        @pl.when(n_valid > part)
        def _(rows=rows, part=part):
            if part == 0:
                wgu_ref[:, 0:tf] = wg_ref[...].astype(BF16)
                wgu_ref[:, tf:2 * tf] = wu_ref[...].astype(BF16)
                wdb_ref[...] = wd_ref[...].astype(BF16)
            h = hb_ref[rows, :]
            gu = jnp.dot(h, wgu_ref[...], preferred_element_type=F32)
            a = (_silu(gu[:, :tf]) * gu[:, tf:]).astype(BF16)
            o_ref[rows, :] += jnp.dot(a, wdb_ref[...], preferred_element_type=F32)


def moe_ffn(hs, tile_expert, tile_parts, n_tiles, wg, wu, wd, *, tm, sub, tf=256):
    npad, d = hs.shape
    f = wg.shape[2]
    nj = f // tf
    col = lambda i, j, nt: jnp.where(i < nt[0], j, nj - 1)
    grid_spec = pltpu.PrefetchScalarGridSpec(
        num_scalar_prefetch=3,
        grid=(npad // tm, f // tf),
        in_specs=[
            pl.BlockSpec((tm, d), lambda i, j, te, tp, nt: (jnp.minimum(i, nt[0] - 1), 0)),
            pl.BlockSpec((None, d, tf), lambda i, j, te, tp, nt: (te[i], 0, col(i, j, nt))),
            pl.BlockSpec((None, d, tf), lambda i, j, te, tp, nt: (te[i], 0, col(i, j, nt))),
            pl.BlockSpec((None, tf, d), lambda i, j, te, tp, nt: (te[i], col(i, j, nt), 0)),
        ],
        out_specs=pl.BlockSpec((tm, d), lambda i, j, te, tp, nt: (i, 0)),
        scratch_shapes=[pltpu.VMEM((tm, d), BF16), pltpu.VMEM((d, 2 * tf), BF16), pltpu.VMEM((tf, d), BF16)],
    )
    return pl.pallas_call(
        functools.partial(_moe_ffn_kernel, sub=sub),
        grid_spec=grid_spec,
        out_shape=jax.ShapeDtypeStruct((npad, d), F32),
        compiler_params=_cparams(("parallel", "arbitrary")),
        name="moe_grouped_swiglu",
    )(tile_expert, tile_parts, n_tiles, hs, wg, wu, wd)


DMA_UNROLL = 8


def _dispatch_kernel(pos_ref, pad_ref, h_ref, xs_ref, zero_ref, sem, *, tg, n_tok):
    i = pl.program_id(0)
    base = i * tg

    def copy(k, r):
        return pltpu.make_async_copy(h_ref.at[pl.ds(r, 1), :],
                                     xs_ref.at[pl.ds(pos_ref[k * n_tok + base + r], 1), :], sem)

    def start(r, carry):
        for k in range(TOP_K):
            copy(k, r).start()
        return carry
    lax.fori_loop(0, tg, start, 0, unroll=DMA_UNROLL)

    def wait(r, carry):
        for k in range(TOP_K):
            copy(k, r).wait()
        return carry
    lax.fori_loop(0, tg, wait, 0, unroll=DMA_UNROLL)

    zb = zero_ref.shape[0]

    @pl.when(i == pl.num_programs(0) - 1)
    def _():
        zero_ref[...] = jnp.zeros(zero_ref.shape, F32)

        def zero_row(r):
            return pltpu.make_async_copy(zero_ref.at[pl.ds(0, 1), :], xs_ref.at[pl.ds(r, 1), :], sem)

        for e in range(N_EXPERTS):
            lo, hi = pad_ref[e], pad_ref[N_EXPERTS + e]

            def zstart(r, carry):
                zero_row(r).start()
                return carry
            lax.fori_loop(lo, hi, zstart, 0)

            def zwait(r, carry):
                zero_row(r).wait()
                return carry
            lax.fori_loop(lo, hi, zwait, 0)

        for e in range(N_EXPERTS + 1):
            lo = pad_ref[2 * N_EXPERTS + e]
            n_blk = (pad_ref[3 * N_EXPERTS + 1 + e] - lo) // zb

            def zero_tile(c, lo=lo):
                rows = pl.ds(pl.multiple_of(lo + c * zb, zb), zb)
                return pltpu.make_async_copy(zero_ref, xs_ref.at[rows, :], sem)

            def tstart(c, carry, zero_tile=zero_tile):
                zero_tile(c).start()
                return carry
            lax.fori_loop(0, n_blk, tstart, 0)

            def twait(c, carry, zero_tile=zero_tile):
                zero_tile(c).wait()
                return carry
            lax.fori_loop(0, n_blk, twait, 0)


def dispatch_rows(h, pos, pad_ranges, npad, *, tg=256, zb=256):
    t, d = h.shape
    grid_spec = pltpu.PrefetchScalarGridSpec(
        num_scalar_prefetch=2,
        grid=(t // tg,),
        in_specs=[pl.BlockSpec((tg, d), lambda i, pos, pad: (i, 0))],
        out_specs=pl.BlockSpec(memory_space=pl.ANY),
        scratch_shapes=[pltpu.VMEM((zb, d), F32), pltpu.SemaphoreType.DMA(())],
    )
    return pl.pallas_call(
        functools.partial(_dispatch_kernel, tg=tg, n_tok=t),
        grid_spec=grid_spec,
        out_shape=jax.ShapeDtypeStruct((npad, d), F32),
        compiler_params=_cparams(("arbitrary",)),
        name="dispatch_scatter",
    )(pos, pad_ranges, h)


def _combine_kernel(pos_ref, x_ref, rw_ref, ys_ref, o_ref, buf_ref, sem, *, tc, n_tok):
    base = pl.program_id(0) * tc

    def copy(k, r):
        return pltpu.make_async_copy(ys_ref.at[pl.ds(pos_ref[k * n_tok + base + r], 1), :],
                                     buf_ref.at[k, pl.ds(r, 1), :], sem)

    def start(r, carry):
        for k in range(TOP_K):
            copy(k, r).start()
        return carry
    lax.fori_loop(0, tc, start, 0, unroll=DMA_UNROLL)

    def wait(r, carry):
        for k in range(TOP_K):
            copy(k, r).wait()
        return carry
    lax.fori_loop(0, tc, wait, 0, unroll=DMA_UNROLL)
    rw = rw_ref[...]
    o_ref[...] = x_ref[...] + rw[:, 0:1] * buf_ref[0] + rw[:, 1:2] * buf_ref[1]


def combine_rows(x, rw, ys, pos, *, tc=256):
    t, d = x.shape
    grid_spec = pltpu.PrefetchScalarGridSpec(
        num_scalar_prefetch=1,
        grid=(t // tc,),
        in_specs=[pl.BlockSpec((tc, d), lambda i, pos: (i, 0)), pl.BlockSpec((tc, LANES), lambda i, pos: (i, 0)),
                  pl.BlockSpec(memory_space=pl.ANY)],
        out_specs=pl.BlockSpec((tc, d), lambda i, pos: (i, 0)),
        scratch_shapes=[pltpu.VMEM((TOP_K, tc, d), F32), pltpu.SemaphoreType.DMA(())],
    )
    return pl.pallas_call(
        functools.partial(_combine_kernel, tc=tc, n_tok=t),
        grid_spec=grid_spec,
        out_shape=jax.ShapeDtypeStruct((t, d), F32),
        compiler_params=_cparams(("arbitrary",)),
        name="combine_gather",
    )(pos, x, rw, ys)


def _routing_plan(ridx, rank, counts, tm, sub, npad):
    alloc = ((counts + tm - 1) // tm) * tm
    used = ((counts + sub - 1) // sub) * sub
    ends = jnp.cumsum(alloc)
    starts = ends - alloc
    hit = ridx[:, :, None] == jnp.arange(N_EXPERTS, dtype=jnp.int32)[None, None, :]
    pos = rank + jnp.sum(jnp.where(hit, starts[None, None, :], 0), axis=-1)
    n_tiles = (ends[-1] // tm).astype(jnp.int32).reshape(1)
    tile_row = jnp.arange(npad // tm, dtype=jnp.int32) * tm
    tile_expert = jnp.sum(jnp.minimum(tile_row, ends[-1] - tm)[:, None] >= ends[None, :], axis=-1).astype(jnp.int32)
    used_end = (starts + used)[tile_expert]
    tile_parts = jnp.where(tile_row < ends[-1], jnp.clip((used_end - tile_row) // sub, 0, tm // sub), 0)
    total = jnp.full((1,), npad, jnp.int32)
    pad_ranges = jnp.concatenate([starts + counts, starts + used,
                                  starts + used, ends[-1:], ends, total])
    return (pos.T.reshape(-1).astype(jnp.int32), pad_ranges.astype(jnp.int32), tile_expert,
            tile_parts.astype(jnp.int32), n_tiles)


def moe(h, x, ridx, rw, rank, counts, wg, wu, wd, *, tm=1024, sub=512, tf=256, tg=256):
    t, d = x.shape
    assert tm % sub == 0 and t % tg == 0
    npad = (TOP_K * t + N_EXPERTS * (tm - 1)) // tm * tm
    pos, pad_ranges, tile_expert, tile_parts, n_tiles = _routing_plan(
        ridx[:, :TOP_K], rank[:, :TOP_K], counts[0, :N_EXPERTS], tm, sub, npad)
    xs = dispatch_rows(h, pos, pad_ranges, npad, tg=tg, zb=sub)
    ys = moe_ffn(xs, tile_expert, tile_parts, n_tiles, wg, wu, wd, tm=tm, sub=sub, tf=tf)
    return combine_rows(x, rw, ys, pos, tc=tg)


def _split_w_in(w):
    sizes = (ATTN_WIDTH, ATTN_WIDTH, ATTN_WIDTH, 3 * GDN_WIDTH, GDN_WIDTH, GDN_HEADS, GDN_HEADS,
             SSM_WIDTH, SSM_XBC, SSM_HEADS)
    pieces, start = [], 0
    for size in sizes:
        pieces.append(w[:, start:start + size])
        start += size
    aq, ak, av, gqkv, ggate, gbeta, ga, sz, sxbc, sdt = pieces
    main = jnp.concatenate([gqkv, ggate, sz, sxbc, aq, ak, av], axis=1).astype(BF16)
    small = jnp.concatenate([gbeta, ga, sdt], axis=1)
    small = jnp.pad(small, ((0, 0), (0, LANES - small.shape[1]))).astype(BF16)
    return main, small


def kernel(x, rel_bias, mix_norm, w_in, attn_q_gain, attn_k_gain, attn_lambda, attn_sub_gain, gdn_conv_w, gdn_A_log, gdn_dt_bias, gdn_o_gain, ssm_conv_w, ssm_conv_b, ssm_A_log, ssm_dt_bias, ssm_D, ssm_norm_gain, w_out, ffn_norm, ffn_w_gate, ffn_w_up, ffn_w_down, moe_router, moe_w_gate, moe_w_up, moe_w_down):
    b, s, d = x.shape
    t = b * s
    depth = w_in.shape[0]
    tiles = _tile_plan(t, s)
    bias_tiles = _bias_tiles(rel_bias, tiles["attn_tq"])
    xf = x.reshape(t, d).astype(F32)
    for li in range(depth):
        w_main, w_small = _split_w_in(w_in[li])
        proj, small = norm_inproj(xf, mix_norm[li].astype(F32), w_main, w_small, tm=tiles["proj_tm"])
        proj3 = proj.reshape(b, s, N_MAIN)
        small3 = small.reshape(b, s, LANES)
        attn_o = attention(proj3, attn_q_gain[li], attn_k_gain[li], attn_lambda[li], attn_sub_gain[li],
                           bias_tiles, li, tq=tiles["attn_tq"])
        gdn_o = gdn(proj3, small3, gdn_conv_w[li], gdn_A_log[li], gdn_dt_bias[li], gdn_o_gain[li],
                    tile=tiles["scan_tile"])
        ssm_o = ssd(proj3, small3, ssm_conv_w[li], ssm_conv_b[li], ssm_A_log[li], ssm_dt_bias[li],
                    ssm_D[li], ssm_norm_gain[li], tile=tiles["scan_tile"])
        wo = w_out[li].astype(BF16)
        mix = (attn_o.reshape(t, ATTN_WIDTH), gdn_o.reshape(t, GDN_WIDTH), ssm_o.reshape(t, SSM_WIDTH))
        j = li // 2
        if li % 2 == 0:
            xf, h = outproj(*mix, wo, xf, ffn_norm[li], tm=tiles["proj_tm"])
            xf = ffn(h, xf, ffn_w_gate[j], ffn_w_up[j], ffn_w_down[j], tm=tiles["ffn_tm"])
        else:
            w_r = jnp.pad(moe_router[j], ((0, 0), (0, LANES - N_EXPERTS))).astype(BF16)
            xf, h, ridx, rw, rank, counts = outproj(*mix, wo, xf, ffn_norm[li], w_r, tm=tiles["proj_tm"])
            xf = moe(h, xf, ridx, rw, rank, counts, moe_w_gate[j], moe_w_up[j], moe_w_down[j],
                     tm=tiles["moe_tm"], sub=tiles["moe_tm"] // 2, tg=tiles["row_dma_tile"])
    return xf.reshape(b, s, d).astype(x.dtype)
```

```python
import functools
import math

import jax
import jax.numpy as jnp
from jax import lax
from jax.experimental import pallas as pl
from jax.experimental.pallas import tpu as pltpu

F32 = jnp.float32
BF16 = jnp.bfloat16

NORM_EPS = 1e-6
CONV_K = 4
LANES = 128
HALO = 8

ATTN_HEADS = 4
ATTN_QK_DIM = 64
ATTN_V_DIM = 128
ATTN_WIDTH = 512
REL_BUCKETS = 32
REL_MAX_DIST = 128

GDN_HEADS = 6
GDN_DIM = 128
GDN_WIDTH = 768

SSM_HEADS = 12
SSM_HEAD_DIM = 64
SSM_GROUPS = 2
SSM_STATE = 128
SSM_WIDTH = 768
SSM_XBC = 1280

N_EXPERTS = 8
TOP_K = 2

GDN_COLS = 4 * GDN_WIDTH
SZ_BLOCK = 4
XBC_BLOCK = 3
ATTN_BLOCK0 = (5 * GDN_WIDTH + SSM_XBC) // LANES
N_MAIN = 5 * GDN_WIDTH + SSM_XBC + 3 * ATTN_WIDTH
BETA_LANE0, GA_LANE0, DT_LANE0 = 0, GDN_HEADS, 2 * GDN_HEADS

VMEM_BYTES = 64 * 1024 * 1024
VMEM_LIMIT = VMEM_BYTES - 8 * 1024 * 1024


def _cparams(sem):
    return pltpu.CompilerParams(dimension_semantics=sem, vmem_limit_bytes=VMEM_LIMIT)


def _tile_plan(t, s):
    moe_tm = min(1024, t // 2)
    return dict(
        proj_tm=min(512, t),
        attn_tq=min(512, s),
        scan_tile=min(256, s),
        ffn_tm=min(1024, t),
        moe_tm=moe_tm,
        row_dma_tile=min(512, moe_tm // 2),
    )


def _dot(a, b):
    return jnp.dot(a.astype(BF16), b.astype(BF16), preferred_element_type=F32)


def _dot_nt(a, b):
    return lax.dot_general(a.astype(BF16), b.astype(BF16), (((1,), (1,)), ((), ())),
                           preferred_element_type=F32)


def _dot_tn(a, b):
    return lax.dot_general(a.astype(BF16), b.astype(BF16), (((0,), (0,)), ((), ())),
                           preferred_element_type=F32)


def _dot_f32(a, b):
    return jnp.dot(a, b, preferred_element_type=F32, precision=lax.Precision.HIGHEST)


def _sigmoid(x):
    return 0.5 * jnp.tanh(0.5 * x) + 0.5


def _silu(x):
    return x * _sigmoid(x)


def _softplus(x):
    return jnp.maximum(x, 0.0) + jnp.log(1.0 + jnp.exp(-jnp.abs(x)))


def _rms(x):
    return x * lax.rsqrt(jnp.mean(x * x, axis=-1, keepdims=True) + NORM_EPS)


def _tril(n, strict=False):
    r = lax.broadcasted_iota(jnp.int32, (n, n), 0)
    c = lax.broadcasted_iota(jnp.int32, (n, n), 1)
    return (r > c) if strict else (r >= c)


def _norm_inproj_kernel(x_ref, g_ref, w_ref, ws_ref, o_ref, os_ref, h_ref):
    @pl.when(pl.program_id(1) == 0)
    def _():
        h = (_rms(x_ref[...]) * g_ref[...]).astype(BF16)
        h_ref[...] = h
        os_ref[...] = jnp.dot(h, ws_ref[...], preferred_element_type=F32)

    o_ref[...] = jnp.dot(h_ref[...], w_ref[...], preferred_element_type=F32).astype(o_ref.dtype)


def norm_inproj(x, gain, w_main, w_small, *, tm=512, tn=3328):
    t, d = x.shape
    n = w_main.shape[1]
    return pl.pallas_call(
        _norm_inproj_kernel,
        grid=(t // tm, n // tn),
        in_specs=[
            pl.BlockSpec((tm, d), lambda i, j: (i, 0)),
            pl.BlockSpec((1, d), lambda i, j: (0, 0)),
            pl.BlockSpec((d, tn), lambda i, j: (0, j)),
            pl.BlockSpec((d, LANES), lambda i, j: (0, 0)),
        ],
        out_specs=[
            pl.BlockSpec((tm, tn), lambda i, j: (i, j)),
            pl.BlockSpec((tm, LANES), lambda i, j: (i, 0)),
        ],
        out_shape=[jax.ShapeDtypeStruct((t, n), BF16), jax.ShapeDtypeStruct((t, LANES), F32)],
        scratch_shapes=[pltpu.VMEM((tm, d), BF16)],
        compiler_params=_cparams(("parallel", "arbitrary")),
        name="norm_inproj",
    )(x, gain.reshape(1, d), w_main, w_small)


def _pair_rms(x, gain):
    lo = lax.broadcasted_iota(jnp.int32, (1, LANES), 1) < ATTN_QK_DIM
    sq = x * x
    s_lo = jnp.sum(jnp.where(lo, sq, 0.0), axis=-1, keepdims=True)
    s_hi = jnp.sum(jnp.where(lo, 0.0, sq), axis=-1, keepdims=True)
    ms = jnp.where(lo, s_lo, s_hi) * (1.0 / ATTN_QK_DIM)
    return x * lax.rsqrt(ms + NORM_EPS) * gain


LOG2_E = math.log2(math.e)
ONES_ROWS = 16


def _attn_kernel(lam_ref, qg_ref, kg_ref, sg_ref, bias_ref, q_ref, k_ref, v_ref, o_ref,
                 kn_ref, vt_ref, m_ref, acc_ref, sa_ref, sb_ref, *, tq, lam_init):
    qi = pl.program_id(2)
    lo = lax.broadcasted_iota(jnp.int32, (1, LANES), 1) < ATTN_QK_DIM
    nv = ATTN_V_DIM

    @pl.when(qi == 0)
    def _():
        def body(c, carry):
            rows = pl.ds(pl.multiple_of(c * tq, tq), tq)
            kn_ref[rows, :] = _pair_rms(k_ref[rows, :].astype(F32), kg_ref[...]).astype(BF16)
            vt_ref[c, 0:nv, :] = v_ref[rows, :].astype(F32).T.astype(BF16)
            vt_ref[c, nv:nv + ONES_ROWS, :] = jnp.ones((ONES_ROWS, tq), BF16)
            return carry
        lax.fori_loop(0, k_ref.shape[0] // tq, body, 0)

    q = _pair_rms(q_ref[...].astype(F32), qg_ref[...]) * (ATTN_QK_DIM ** -0.5 * LOG2_E)
    qz = (jnp.where(lo, q, 0.0).astype(BF16), jnp.where(lo, 0.0, q).astype(BF16))

    maps = range(2)

    def scores(kb, n, bias):
        k_blk = kn_ref[pl.ds(pl.multiple_of(kb * tq, tq), n * tq), :]
        s = [lax.dot_general(k_blk, qz[mp], (((1,), (1,)), ((), ())), preferred_element_type=F32) for mp in maps]
        return s if bias is None else [x + bias for x in s]

    def update(s, kb, n, first):
        vt = vt_ref[kb] if n == 1 else jnp.concatenate([vt_ref[kb + i] for i in range(n)], axis=-1)
        s_max = [jnp.max(x, axis=0, keepdims=True) for x in s]
        if first:
            m_new = s_max
        else:
            m_prev = [m_ref[mp] for mp in maps]
            m_new = [jnp.maximum(m_prev[mp], s_max[mp]) for mp in maps]
            alpha = [jnp.exp2(m_prev[mp] - m_new[mp]) for mp in maps]
        p = [jnp.exp2(s[mp] - m_new[mp]).astype(BF16) for mp in maps]
        pv = [jnp.dot(vt, p[mp], preferred_element_type=F32) for mp in maps]
        for mp in maps:
            acc_ref[mp] = pv[mp] if first else alpha[mp] * acc_ref[mp] + pv[mp]
            m_ref[mp] = m_new[mp]

    @pl.when(qi == 0)
    def _():
        update(scores(qi, 1, bias_ref[1]), qi, 1, True)

    n_far = jnp.maximum(qi - 1, 0)
    n_groups = n_far // 2
    bufs = (sa_ref, sb_ref)

    def fill(buf, k, near):
        s = scores(qi - 1, 2, bias_ref[...].reshape(2 * tq, tq)) if near else scores((k - 1) * 2, 2, None)
        for mp in maps:
            buf[mp] = s[mp]

    def drain(buf, k, near):
        update([buf[mp] for mp in maps], (qi - 1) if near else (k - 1) * 2, 2, near)

    def stage(k, cur, nxt, near=False):
        @pl.when(k < n_groups)
        def _():
            fill(nxt, k + 1, False)
            drain(cur, k, near)

        @pl.when(k == n_groups)
        def _():
            drain(cur, k, near)

    @pl.when(qi >= 1)
    def _():
        fill(bufs[0], 0, True)
        stage(0, bufs[0], bufs[1], near=True)

        def far(kk, carry):
            stage(2 * kk + 1, bufs[1], bufs[0])
            stage(2 * kk + 2, bufs[0], bufs[1])
            return carry
        lax.fori_loop(0, (n_groups + 1) // 2, far, 0)

    @pl.when(n_far - 2 * n_groups == 1)
    def _():
        update(scores(n_far - 1, 1, None), n_far - 1, 1, False)

    lam = lam_ref[...]
    lam_full = (jnp.exp(jnp.sum(lam[0:1] * lam[1:2], axis=-1, keepdims=True))
                - jnp.exp(jnp.sum(lam[2:3] * lam[3:4], axis=-1, keepdims=True)) + lam_init)
    a0, a1 = acc_ref[0], acc_ref[1]
    o = a0[0:nv] / a0[nv:nv + 1] - lam_full * (a1[0:nv] / a1[nv:nv + 1])
    o = o * lax.rsqrt(jnp.mean(o * o, axis=0, keepdims=True) + NORM_EPS)
    o_ref[...] = (o.T * sg_ref[...] * (1.0 - lam_init)).astype(o_ref.dtype)


def _t5_bucket(n):
    max_exact = REL_BUCKETS // 2
    nf = jnp.maximum(n, max_exact).astype(F32)
    large = max_exact + (jnp.log(nf / max_exact) / math.log(REL_MAX_DIST / max_exact)
                         * (REL_BUCKETS - max_exact)).astype(jnp.int32)
    return jnp.where(n < max_exact, n, jnp.minimum(large, REL_BUCKETS - 1))


def _bias_tiles(rel_bias, tq):
    assert tq >= REL_MAX_DIST
    heads = rel_bias.shape[1]

    def body(tab_ref, o_ref):
        h = pl.program_id(0)
        r = lax.broadcasted_iota(jnp.int32, (2 * tq, tq), 0)
        i = lax.broadcasted_iota(jnp.int32, (2 * tq, tq), 1)
        rel = tq + i - r
        bucket = _t5_bucket(jnp.maximum(rel, 0))
        far = tab_ref[h, REL_BUCKETS - 1]
        bias = jnp.zeros((2 * tq, tq), F32)
        for bkt in range(REL_BUCKETS - 1):
            bias = jnp.where(bucket == bkt, tab_ref[h, bkt] - far, bias)
        o_ref[...] = jnp.where(rel >= 0, bias * LOG2_E, -jnp.inf)

    tiles = pl.pallas_call(
        body,
        grid=(heads,),
        in_specs=[pl.BlockSpec(memory_space=pltpu.SMEM)],
        out_specs=pl.BlockSpec((None, 2 * tq, tq), lambda h: (h, 0, 0)),
        out_shape=jax.ShapeDtypeStruct((heads, 2 * tq, tq), F32),
        compiler_params=_cparams(("parallel",)),
        name="t5_bias_tiles",
    )(rel_bias.astype(F32).T)
    return tiles.reshape(heads, 2, tq, tq)


def attention(proj3, q_gain, k_gain, lam, sub_gain, bias_tiles, layer_idx, *, tq=512):
    b, s, _ = proj3.shape
    h = ATTN_HEADS
    lam_init = 0.8 - 0.6 * math.exp(-0.3 * layer_idx)
    qg = jnp.tile(q_gain.astype(F32), 2).reshape(1, LANES)
    kg = jnp.tile(k_gain.astype(F32), 2).reshape(1, LANES)
    sg = sub_gain.astype(F32).reshape(1, LANES)
    kern = functools.partial(_attn_kernel, tq=tq, lam_init=lam_init)
    const = lambda shape: pl.BlockSpec(shape, lambda bi, hi, qi: (0,) * len(shape))
    return pl.pallas_call(
        kern,
        grid=(b, h, s // tq),
        in_specs=[
            const((4, ATTN_QK_DIM)), const((1, LANES)), const((1, LANES)), const((1, LANES)),
            pl.BlockSpec((None, 2, tq, tq), lambda bi, hi, qi: (hi, 0, 0, 0)),
            pl.BlockSpec((None, tq, LANES), lambda bi, hi, qi: (bi, qi, ATTN_BLOCK0 + hi)),
            pl.BlockSpec((None, s, LANES), lambda bi, hi, qi: (bi, 0, ATTN_BLOCK0 + h + hi)),
            pl.BlockSpec((None, s, LANES), lambda bi, hi, qi: (bi, 0, ATTN_BLOCK0 + 2 * h + hi)),
        ],
        out_specs=pl.BlockSpec((None, tq, LANES), lambda bi, hi, qi: (bi, qi, hi)),
        out_shape=jax.ShapeDtypeStruct((b, s, ATTN_WIDTH), BF16),
        scratch_shapes=[
            pltpu.VMEM((s, LANES), BF16),
            pltpu.VMEM((s // tq, ATTN_V_DIM + ONES_ROWS, tq), BF16),
            pltpu.VMEM((2, 1, tq), F32),
            pltpu.VMEM((2, ATTN_V_DIM + ONES_ROWS, tq), F32),
            pltpu.VMEM((2, 2 * tq, tq), F32),
            pltpu.VMEM((2, 2 * tq, tq), F32),
        ],
        compiler_params=_cparams(("parallel", "parallel", "arbitrary")),
        name="diff_attention",
    )(lam.astype(F32), qg, kg, sg, bias_tiles, proj3, proj3, proj3)


def _causal_conv(x, w_ref, buf_ref, first):
    rows = x.shape[0]

    @pl.when(first)
    def _():
        buf_ref[0:HALO, :] = jnp.zeros((HALO, x.shape[1]), F32)

    buf_ref[HALO:HALO + rows, :] = x
    y = w_ref[CONV_K - 1:CONV_K, :] * x
    for j in range(CONV_K - 1):
        off = HALO - (CONV_K - 1) + j
        y = y + w_ref[j:j + 1, :] * buf_ref[off:off + rows, :]
    buf_ref[0:HALO, :] = x[rows - HALO:, :]
    return y


def _unit_lower_inverses(mats):
    n = mats[0].shape[0]
    eye = (lax.broadcasted_iota(jnp.int32, (n, n), 0)
           == lax.broadcasted_iota(jnp.int32, (n, n), 1)).astype(F32)
    ps = [eye - a for a in mats]
    bs = [_dot(a, a) for a in mats]
    steps = int(math.log2(n)) - 1
    for i in range(steps):
        if i + 1 < steps:
            both = [_dot(jnp.concatenate([p, b], axis=0), b) for p, b in zip(ps, bs)]
            ps = [p + x[:n] for p, x in zip(ps, both)]
            bs = [x[n:] for x in both]
        else:
            ps = [p + _dot(p, b) for p, b in zip(ps, bs)]
    return ps


def _gdn_kernel(blk_ref, sm_ref, cw_ref, arow_ref, dtb_ref, og_ref, o_ref, buf_ref, st_ref, *, chunk):
    first = pl.program_id(1) == 0
    c = chunk
    d = GDN_DIM
    nq = 3 * GDN_WIDTH
    tile = blk_ref.shape[0]
    heads = range(GDN_HEADS)
    chunks = range(tile // c)

    @pl.when(first)
    def _():
        st_ref[...] = jnp.zeros(st_ref.shape, F32)

    qkv = _silu(_causal_conv(blk_ref[:, 0:nq].astype(F32), cw_ref, buf_ref, first))
    sm = sm_ref[...]
    beta = _sigmoid(sm)
    g = arow_ref[...] * _softplus(sm + dtb_ref[...])
    strict = _tril(c, strict=True)
    causal = _tril(c)
    r = lax.broadcasted_iota(jnp.int32, (tile, tile), 0)
    cc = lax.broadcasted_iota(jnp.int32, (tile, tile), 1)
    same_chunk_tril = jnp.logical_and(r >= cc, (r // c) == (cc // c))
    gc = _dot_f32(same_chunk_tril.astype(F32), g)
    gct = gc.T

    l2 = lambda x: x * lax.rsqrt(jnp.sum(x * x, axis=-1, keepdims=True) + NORM_EPS)
    qn = [l2(qkv[:, h * d:(h + 1) * d]) * (d ** -0.5) for h in heads]
    kn = [l2(qkv[:, GDN_WIDTH + h * d:GDN_WIDTH + (h + 1) * d]) for h in heads]

    pairs = [(ci, h) for ci in chunks for h in heads]
    prep = {}
    for ci, h in pairs:
        rows = slice(ci * c, (ci + 1) * c)
        q, k = qn[h][rows], kn[h][rows]
        v = qkv[rows, 2 * GDN_WIDTH + h * d:2 * GDN_WIDTH + (h + 1) * d]
        bcol = beta[rows, BETA_LANE0 + h:BETA_LANE0 + h + 1]
        col = gc[rows, GA_LANE0 + h:GA_LANE0 + h + 1]
        row = gct[GA_LANE0 + h:GA_LANE0 + h + 1, rows]
        last = gc[(ci + 1) * c - 1:(ci + 1) * c, GA_LANE0 + h:GA_LANE0 + h + 1]
        dec = jnp.exp(jnp.where(causal, col - row, -jnp.inf))
        ecol = jnp.exp(col)
        kb = k * bcol
        prep[ci, h] = dict(k=k, kb=kb, dec=dec, last=last,
                           lhs=jnp.concatenate([kb, q], axis=0),
                           rhs=jnp.concatenate([v * bcol, kb * ecol], axis=-1),
                           qd=q * ecol, kd=k * jnp.exp(last - col))
    for key in pairs:
        x = prep[key]
        both = _dot_nt(x["lhs"], x["k"])
        x["a"] = both[:c] * jnp.where(strict, x["dec"], 0.0)
        x["qk"] = both[c:] * x["dec"]
    t_inv = _unit_lower_inverses([prep[key]["a"] for key in pairs])
    for key, t in zip(pairs, t_inv):
        prep[key]["sol"] = _dot(t, prep[key]["rhs"])

    states = [st_ref[h] for h in heads]
    for ci in chunks:
        xs = [prep[ci, h] for h in heads]
        both = [_dot(jnp.concatenate([x["sol"][:, d:], x["qd"]], axis=0), st) for x, st in zip(xs, states)]
        v_new = [x["sol"][:, :d] - y[:c] for x, y in zip(xs, both)]
        inter = [_dot(x["qk"], vn) for x, vn in zip(xs, v_new)]
        upd = [_dot_tn(x["kd"], vn) for x, vn in zip(xs, v_new)]
        states = [st * jnp.exp(x["last"]) + u for st, x, u in zip(states, xs, upd)]
        rows = slice(ci * c, (ci + 1) * c)
        for h in heads:
            o = both[h][c:] + inter[h]
            gate = blk_ref[rows, nq + h * d:nq + (h + 1) * d].astype(F32)
            o_ref[rows, h * d:(h + 1) * d] = (_rms(o) * og_ref[...] * _silu(gate)).astype(o_ref.dtype)
    for h in heads:
        st_ref[h] = states[h]


def gdn(proj3, small3, conv_w, a_log, dt_bias, o_gain, *, chunk=64, tile=256):
    b, s, _ = proj3.shape
    tile = min(tile, s)
    pad = lambda v, lane0: jnp.zeros((1, LANES), F32).at[0, lane0:lane0 + v.shape[0]].set(v.astype(F32))
    arow = pad(-jnp.exp(a_log.astype(F32)), GA_LANE0)
    dtb = pad(dt_bias, GA_LANE0)
    const = lambda shape: pl.BlockSpec(shape, lambda bi, si: (0,) * len(shape))
    return pl.pallas_call(
        functools.partial(_gdn_kernel, chunk=chunk),
        grid=(b, s // tile),
        in_specs=[
            pl.BlockSpec((None, tile, GDN_COLS), lambda bi, si: (bi, si, 0)),
            pl.BlockSpec((None, tile, LANES), lambda bi, si: (bi, si, 0)),
            const((CONV_K, 3 * GDN_WIDTH)), const((1, LANES)), const((1, LANES)), const((1, GDN_DIM)),
        ],
        out_specs=pl.BlockSpec((None, tile, GDN_WIDTH), lambda bi, si: (bi, si, 0)),
        out_shape=jax.ShapeDtypeStruct((b, s, GDN_WIDTH), BF16),
        scratch_shapes=[
            pltpu.VMEM((HALO + tile, 3 * GDN_WIDTH), F32),
            pltpu.VMEM((GDN_HEADS, GDN_DIM, GDN_DIM), F32),
        ],
        compiler_params=_cparams(("parallel", "arbitrary")),
        name="gated_deltanet",
    )(proj3, small3, conv_w.astype(F32), arow, dtb, o_gain.astype(F32).reshape(1, GDN_DIM))


def _ssd_kernel(xbc_ref, z_ref, sm_ref, cw_ref, cb_ref, arow_ref, dtb_ref, dsk_ref, ng_ref, o_ref,
                buf_ref, st_ref, y_ref, *, chunk):
    first = pl.program_id(1) == 0
    c = chunk
    p2 = 2 * SSM_HEAD_DIM
    heads_per_group = SSM_HEADS // SSM_GROUPS
    gw = SSM_WIDTH // SSM_GROUPS

    @pl.when(first)
    def _():
        st_ref[...] = jnp.zeros(st_ref.shape, F32)

    tile = xbc_ref.shape[0]
    chunks = range(tile // c)
    pairs = range(SSM_HEADS // 2)
    xbc = _silu(_causal_conv(xbc_ref[...].astype(F32), cw_ref, buf_ref, first) + cb_ref[...])
    x = xbc[:, :SSM_WIDTH]
    sm = sm_ref[...]
    dt = _softplus(sm + dtb_ref[...])
    causal = _tril(c)
    r = lax.broadcasted_iota(jnp.int32, (tile, tile), 0)
    cc = lax.broadcasted_iota(jnp.int32, (tile, tile), 1)
    same_chunk_tril = jnp.logical_and(r >= cc, (r // c) == (cc // c))
    acum = _dot_f32(same_chunk_tril.astype(F32), dt * arow_ref[...])
    acum_t = acum.T
    lo = lax.broadcasted_iota(jnp.int32, (1, p2), 1) < SSM_HEAD_DIM
    halves = lambda lane, arr: (arr[:, lane:lane + 1], arr[:, lane + 1:lane + 2])
    sel = lambda pair: jnp.where(lo, pair[0], pair[1])

    bms, cms, cbs = {}, {}, {}
    for ci in chunks:
        rows = slice(ci * c, (ci + 1) * c)
        for grp in range(SSM_GROUPS):
            bms[ci, grp] = xbc[rows, SSM_WIDTH + grp * SSM_STATE:SSM_WIDTH + (grp + 1) * SSM_STATE]
            cms[ci, grp] = xbc[rows, SSM_WIDTH + (SSM_GROUPS + grp) * SSM_STATE:
                               SSM_WIDTH + (SSM_GROUPS + grp + 1) * SSM_STATE]
            cbs[ci, grp] = _dot_nt(cms[ci, grp], bms[ci, grp])
    units = [(ci, pr) for ci in chunks for pr in pairs]
    prep = {}
    for ci, pr in units:
        rows = slice(ci * c, (ci + 1) * c)
        grp = (2 * pr) // heads_per_group
        lane = DT_LANE0 + 2 * pr
        cols = halves(lane, acum[rows])
        lasts = halves(lane, acum[(ci + 1) * c - 1:(ci + 1) * c, :])
        xdt = x[rows, pr * p2:(pr + 1) * p2] * sel(halves(lane, dt[rows]))
        lmats = [jnp.exp(jnp.where(causal, cols[hh] - acum_t[lane + hh:lane + hh + 1, rows], -jnp.inf))
                 for hh in range(2)]
        prep[ci, pr] = dict(
            grp=grp, xdt=xdt, lhs=[cbs[ci, grp] * lm for lm in lmats],
            out_scale=sel((jnp.exp(cols[0]), jnp.exp(cols[1]))),
            xdt_in=xdt * sel((jnp.exp(lasts[0] - cols[0]), jnp.exp(lasts[1] - cols[1]))),
            keep=sel((jnp.exp(lasts[0]), jnp.exp(lasts[1]))))
    for key in units:
        u = prep[key]
        u["y_diag"] = jnp.where(lo, _dot(u["lhs"][0], u["xdt"]), _dot(u["lhs"][1], u["xdt"]))

    states = [st_ref[pr] for pr in pairs]
    for ci in chunks:
        rows = slice(ci * c, (ci + 1) * c)
        us = [prep[ci, pr] for pr in pairs]
        y_off = [_dot(cms[ci, u["grp"]], st) * u["out_scale"] for u, st in zip(us, states)]
        upd = [_dot_tn(bms[ci, u["grp"]], u["xdt_in"]) for u in us]
        states = [st * u["keep"] + d for st, u, d in zip(states, us, upd)]
        for pr in pairs:
            y_ref[rows, pr * p2:(pr + 1) * p2] = us[pr]["y_diag"] + y_off[pr]
    for pr in pairs:
        st_ref[pr] = states[pr]

    y = (y_ref[...] + dsk_ref[...] * x) * _silu(z_ref[...].astype(F32))
    for grp in range(SSM_GROUPS):
        cols = slice(grp * gw, (grp + 1) * gw)
        o_ref[:, cols] = (_rms(y[:, cols]) * ng_ref[:, cols]).astype(o_ref.dtype)


def ssd(proj3, small3, conv_w, conv_b, a_log, dt_bias, d_skip, norm_gain, *, chunk=128, tile=256):
    b, s, _ = proj3.shape
    tile = min(tile, s)
    pad = lambda v: jnp.zeros((1, LANES), F32).at[0, DT_LANE0:DT_LANE0 + SSM_HEADS].set(v.astype(F32))
    arow = pad(-jnp.exp(a_log.astype(F32)))
    dtb = pad(dt_bias)
    dsk = jnp.repeat(d_skip.astype(F32), SSM_HEAD_DIM).reshape(1, SSM_WIDTH)
    const = lambda shape: pl.BlockSpec(shape, lambda bi, si: (0,) * len(shape))
    return pl.pallas_call(
        functools.partial(_ssd_kernel, chunk=chunk),
        grid=(b, s // tile),
        in_specs=[
            pl.BlockSpec((None, tile, SSM_XBC), lambda bi, si: (bi, si, XBC_BLOCK)),
            pl.BlockSpec((None, tile, SSM_WIDTH), lambda bi, si: (bi, si, SZ_BLOCK)),
            pl.BlockSpec((None, tile, LANES), lambda bi, si: (bi, si, 0)),
            const((CONV_K, SSM_XBC)), const((1, SSM_XBC)), const((1, LANES)), const((1, LANES)),
            const((1, SSM_WIDTH)), const((1, SSM_WIDTH)),
        ],
        out_specs=pl.BlockSpec((None, tile, SSM_WIDTH), lambda bi, si: (bi, si, 0)),
        out_shape=jax.ShapeDtypeStruct((b, s, SSM_WIDTH), BF16),
        scratch_shapes=[
            pltpu.VMEM((HALO + tile, SSM_XBC), F32),
            pltpu.VMEM((SSM_HEADS // 2, SSM_STATE, 2 * SSM_HEAD_DIM), F32),
            pltpu.VMEM((tile, SSM_WIDTH), F32),
        ],
        compiler_params=_cparams(("parallel", "arbitrary")),
        name="mamba2_ssd",
    )(proj3, proj3, small3, conv_w.astype(F32), conv_b.astype(F32).reshape(1, SSM_XBC), arow, dtb, dsk,
      norm_gain.astype(F32).reshape(1, SSM_WIDTH))


def _outproj_kernel(*refs, with_router):
    if with_router:
        (a_ref, g_ref, s_ref, w_ref, x_ref, ng_ref, wr_ref,
         xo_ref, h_ref, ri_ref, rw_ref, rk_ref, cnt_ref, run_ref) = refs
    else:
        a_ref, g_ref, s_ref, w_ref, x_ref, ng_ref, xo_ref, h_ref = refs
    g0, s0 = ATTN_WIDTH, ATTN_WIDTH + GDN_WIDTH
    y = (jnp.dot(a_ref[...], w_ref[0:g0, :], preferred_element_type=F32)
         + jnp.dot(g_ref[...], w_ref[g0:s0, :], preferred_element_type=F32)
         + jnp.dot(s_ref[...], w_ref[s0:, :], preferred_element_type=F32))
    xn = x_ref[...] + y
    xo_ref[...] = xn
    hf = _rms(xn) * ng_ref[...]
    h = hf.astype(BF16)
    h_ref[...] = hf.astype(h_ref.dtype)
    if with_router:
        tm = xn.shape[0]
        lane = lax.broadcasted_iota(jnp.int32, (1, LANES), 1)
        logits = jnp.where(lane < N_EXPERTS, jnp.dot(h, wr_ref[...], preferred_element_type=F32), -jnp.inf)
        v1 = jnp.max(logits, axis=-1, keepdims=True)
        i1 = jnp.min(jnp.where(logits == v1, lane, LANES), axis=-1, keepdims=True)
        rest = jnp.where(lane == i1, -jnp.inf, logits)
        v2 = jnp.max(rest, axis=-1, keepdims=True)
        i2 = jnp.min(jnp.where(rest == v2, lane, LANES), axis=-1, keepdims=True)
        e2 = jnp.exp(v2 - v1)
        ri_ref[...] = jnp.where(lane == 0, i1, i2)
        rw_ref[...] = jnp.where(lane == 0, 1.0 / (1.0 + e2), e2 / (1.0 + e2))

        @pl.when(pl.program_id(0) == 0)
        def _():
            run_ref[...] = jnp.zeros(run_ref.shape, F32)

        before = _tril(tm, strict=True).astype(BF16)
        run = run_ref[0:1, :]
        ranks = []
        for idx in (i1, i2):
            hit = lane == idx
            onehot = hit.astype(F32)
            earlier = jnp.dot(before, onehot.astype(BF16), preferred_element_type=F32) + run
            ranks.append(jnp.sum(jnp.where(hit, earlier, 0.0), axis=-1, keepdims=True))
            run = run + jnp.sum(onehot, axis=0, keepdims=True)
        rk_ref[...] = jnp.where(lane == 0, ranks[0], ranks[1]).astype(jnp.int32)
        run_ref[...] = jnp.broadcast_to(run, run_ref.shape)
        cnt_ref[...] = jnp.broadcast_to(run, cnt_ref.shape).astype(jnp.int32)


def outproj(attn_o, gdn_o, ssm_o, w_out, x, norm_gain, w_router=None, *, tm=512):
    t, d = x.shape
    with_router = w_router is not None
    row = lambda width: pl.BlockSpec((tm, width), lambda i: (i, 0))
    const = lambda shape: pl.BlockSpec(shape, lambda i: (0, 0))
    in_specs = [row(ATTN_WIDTH), row(GDN_WIDTH), row(SSM_WIDTH), const(w_out.shape), row(d), const((1, d))]
    args = [attn_o, gdn_o, ssm_o, w_out, x, norm_gain.astype(F32).reshape(1, d)]
    out_specs = [row(d), row(d)]
    out_shape = [jax.ShapeDtypeStruct((t, d), F32), jax.ShapeDtypeStruct((t, d), F32 if with_router else BF16)]
    scratch = []
    if with_router:
        in_specs.append(const((d, LANES)))
        args.append(w_router)
        out_specs += [row(LANES), row(LANES), row(LANES), const((8, LANES))]
        out_shape += [jax.ShapeDtypeStruct((t, LANES), jnp.int32), jax.ShapeDtypeStruct((t, LANES), F32),
                      jax.ShapeDtypeStruct((t, LANES), jnp.int32), jax.ShapeDtypeStruct((8, LANES), jnp.int32)]
        scratch = [pltpu.VMEM((8, LANES), F32)]
    return pl.pallas_call(
        functools.partial(_outproj_kernel, with_router=with_router),
        grid=(t // tm,),
        in_specs=in_specs, out_specs=out_specs, out_shape=out_shape, scratch_shapes=scratch,
        compiler_params=_cparams(("arbitrary" if with_router else "parallel",)),
        name="outproj_router" if with_router else "outproj",
    )(*args)


def _ffn_kernel(h_ref, x_ref, wg_hbm, wu_hbm, wd_hbm, o_ref, wg_buf, wu_buf, wd_buf, wgu_ref, wdb_ref, sem,
                *, sub, tf, nj):
    def copies(j, slot):
        cols = pl.ds(pl.multiple_of(j * tf, tf), tf)
        return (pltpu.make_async_copy(wg_hbm.at[:, cols], wg_buf.at[slot], sem.at[0, slot]),
                pltpu.make_async_copy(wu_hbm.at[:, cols], wu_buf.at[slot], sem.at[1, slot]),
                pltpu.make_async_copy(wd_hbm.at[cols, :], wd_buf.at[slot], sem.at[2, slot]))

    for cp in copies(0, 0):
        cp.start()
    o_ref[...] = x_ref[...]

    def body(j, carry):
        slot = j % 2

        @pl.when(j + 1 < nj)
        def _():
            for cp in copies(j + 1, 1 - slot):
                cp.start()

        for cp in copies(j, slot):
            cp.wait()
        wgu_ref[:, 0:tf] = wg_buf[slot].astype(BF16)
        wgu_ref[:, tf:2 * tf] = wu_buf[slot].astype(BF16)
        wdb_ref[...] = wd_buf[slot].astype(BF16)
        for part in range(o_ref.shape[0] // sub):
            rows = slice(part * sub, (part + 1) * sub)
            gu = jnp.dot(h_ref[rows, :], wgu_ref[...], preferred_element_type=F32)
            a = (_silu(gu[:, :tf]) * gu[:, tf:]).astype(BF16)
            o_ref[rows, :] += jnp.dot(a, wdb_ref[...], preferred_element_type=F32)
        return carry
    lax.fori_loop(0, nj, body, 0)


def ffn(h, x, wg, wu, wd, *, tm=1024, tf=256):
    t, d = x.shape
    f = wg.shape[1]
    hbm = pl.BlockSpec(memory_space=pl.ANY)
    return pl.pallas_call(
        functools.partial(_ffn_kernel, sub=min(512, tm), tf=tf, nj=f // tf),
        grid=(t // tm,),
        in_specs=[pl.BlockSpec((tm, d), lambda i: (i, 0)),
                  pl.BlockSpec((tm, d), lambda i: (i, 0), pipeline_mode=pl.Buffered(1)), hbm, hbm, hbm],
        out_specs=pl.BlockSpec((tm, d), lambda i: (i, 0)),
        out_shape=jax.ShapeDtypeStruct((t, d), F32),
        scratch_shapes=[pltpu.VMEM((2, d, tf), wg.dtype), pltpu.VMEM((2, d, tf), wu.dtype),
                        pltpu.VMEM((2, tf, d), wd.dtype),
                        pltpu.VMEM((d, 2 * tf), BF16), pltpu.VMEM((tf, d), BF16),
                        pltpu.SemaphoreType.DMA((3, 2))],
        compiler_params=_cparams(("parallel",)),
        name="ffn_swiglu",
    )(h, x, wg, wu, wd)


def _moe_ffn_kernel(te_ref, np_ref, nt_ref, h_ref, wg_ref, wu_ref, wd_ref, o_ref, hb_ref, wgu_ref, wdb_ref,
                    *, sub):
    i = pl.program_id(0)
    j = pl.program_id(1)
    parts = o_ref.shape[0] // sub
    tf = wg_ref.shape[1]
    n_valid = np_ref[i]

    @pl.when(j == 0)
    def _():
        o_ref[...] = jnp.zeros(o_ref.shape, F32)

    for part in range(parts):
        rows = slice(part * sub, (part + 1) * sub)

        @pl.when(jnp.logical_and(n_valid > part, j == 0))
        def _(rows=rows):
            hb_ref[rows, :] = h_ref[rows, :].astype(BF16)

        @pl.when(n_valid > part)
        def _(rows=rows, part=part):
            if part == 0:
                wgu_ref[:, 0:tf] = wg_ref[...].astype(BF16)
                wgu_ref[:, tf:2 * tf] = wu_ref[...].astype(BF16)
                wdb_ref[...] = wd_ref[...].astype(BF16)
            h = hb_ref[rows, :]
            gu = jnp.dot(h, wgu_ref[...], preferred_element_type=F32)
            a = (_silu(gu[:, :tf]) * gu[:, tf:]).astype(BF16)
            o_ref[rows, :] += jnp.dot(a, wdb_ref[...], preferred_element_type=F32)


def moe_ffn(hs, tile_expert, tile_parts, n_tiles, wg, wu, wd, *, tm, sub, tf=256):
    npad, d = hs.shape
    f = wg.shape[2]
    nj = f // tf
    col = lambda i, j, nt: jnp.where(i < nt[0], j, nj - 1)
    grid_spec = pltpu.PrefetchScalarGridSpec(
        num_scalar_prefetch=3,
        grid=(npad // tm, f // tf),
        in_specs=[
            pl.BlockSpec((tm, d), lambda i, j, te, tp, nt: (jnp.minimum(i, nt[0] - 1), 0)),
            pl.BlockSpec((None, d, tf), lambda i, j, te, tp, nt: (te[i], 0, col(i, j, nt))),
            pl.BlockSpec((None, d, tf), lambda i, j, te, tp, nt: (te[i], 0, col(i, j, nt))),
            pl.BlockSpec((None, tf, d), lambda i, j, te, tp, nt: (te[i], col(i, j, nt), 0)),
        ],
        out_specs=pl.BlockSpec((tm, d), lambda i, j, te, tp, nt: (i, 0)),
        scratch_shapes=[pltpu.VMEM((tm, d), BF16), pltpu.VMEM((d, 2 * tf), BF16), pltpu.VMEM((tf, d), BF16)],
    )
    return pl.pallas_call(
        functools.partial(_moe_ffn_kernel, sub=sub),
        grid_spec=grid_spec,
        out_shape=jax.ShapeDtypeStruct((npad, d), F32),
        compiler_params=_cparams(("parallel", "arbitrary")),
        name="moe_grouped_swiglu",
    )(tile_expert, tile_parts, n_tiles, hs, wg, wu, wd)


DMA_UNROLL = 8


def _dispatch_kernel(pos_ref, pad_ref, h_ref, xs_ref, zero_ref, sem, *, tg, n_tok):
    i = pl.program_id(0)
    base = i * tg

    def copy(k, r):
        return pltpu.make_async_copy(h_ref.at[pl.ds(r, 1), :],
                                     xs_ref.at[pl.ds(pos_ref[k * n_tok + base + r], 1), :], sem)

    def start(r, carry):
        for k in range(TOP_K):
            copy(k, r).start()
        return carry
    lax.fori_loop(0, tg, start, 0, unroll=DMA_UNROLL)

    def wait(r, carry):
        for k in range(TOP_K):
            copy(k, r).wait()
        return carry
    lax.fori_loop(0, tg, wait, 0, unroll=DMA_UNROLL)

    @pl.when(i == pl.num_programs(0) - 1)
    def _():
        zero_ref[...] = jnp.zeros(zero_ref.shape, F32)

        def zero_row(r):
            return pltpu.make_async_copy(zero_ref.at[pl.ds(0, 1), :], xs_ref.at[pl.ds(r, 1), :], sem)

        for e in range(N_EXPERTS):
            lo, hi = pad_ref[e], pad_ref[N_EXPERTS + e]

            def zstart(r, carry):
                zero_row(r).start()
                return carry
            lax.fori_loop(lo, hi, zstart, 0)

            def zwait(r, carry):
                zero_row(r).wait()
                return carry
            lax.fori_loop(lo, hi, zwait, 0)

        for e in range(N_EXPERTS + 1):
            lo = pad_ref[2 * N_EXPERTS + e]
            n_blk = (pad_ref[3 * N_EXPERTS + 1 + e] - lo) // tg

            def zero_tile(c, lo=lo):
                rows = pl.ds(pl.multiple_of(lo + c * tg, tg), tg)
                return pltpu.make_async_copy(zero_ref, xs_ref.at[rows, :], sem)

            def tstart(c, carry, zero_tile=zero_tile):
                zero_tile(c).start()
                return carry
            lax.fori_loop(0, n_blk, tstart, 0)

            def twait(c, carry, zero_tile=zero_tile):
                zero_tile(c).wait()
                return carry
            lax.fori_loop(0, n_blk, twait, 0)


def dispatch_rows(h, pos, pad_ranges, npad, *, tg=256):
    t, d = h.shape
    grid_spec = pltpu.PrefetchScalarGridSpec(
        num_scalar_prefetch=2,
        grid=(t // tg,),
        in_specs=[pl.BlockSpec((tg, d), lambda i, pos, pad: (i, 0))],
        out_specs=pl.BlockSpec(memory_space=pl.ANY),
        scratch_shapes=[pltpu.VMEM((tg, d), F32), pltpu.SemaphoreType.DMA(())],
    )
    return pl.pallas_call(
        functools.partial(_dispatch_kernel, tg=tg, n_tok=t),
        grid_spec=grid_spec,
        out_shape=jax.ShapeDtypeStruct((npad, d), F32),
        compiler_params=_cparams(("arbitrary",)),
        name="dispatch_scatter",
    )(pos, pad_ranges, h)


def _combine_kernel(pos_ref, x_ref, rw_ref, ys_ref, o_ref, buf_ref, sem, *, tc, n_tok):
    base = pl.program_id(0) * tc

    def copy(k, r):
        return pltpu.make_async_copy(ys_ref.at[pl.ds(pos_ref[k * n_tok + base + r], 1), :],
                                     buf_ref.at[k, pl.ds(r, 1), :], sem)

    def start(r, carry):
        for k in range(TOP_K):
            copy(k, r).start()
        return carry
    lax.fori_loop(0, tc, start, 0, unroll=DMA_UNROLL)

    def wait(r, carry):
        for k in range(TOP_K):
            copy(k, r).wait()
        return carry
    lax.fori_loop(0, tc, wait, 0, unroll=DMA_UNROLL)
    rw = rw_ref[...]
    o_ref[...] = x_ref[...] + rw[:, 0:1] * buf_ref[0] + rw[:, 1:2] * buf_ref[1]


def combine_rows(x, rw, ys, pos, *, tc=256):
    t, d = x.shape
    grid_spec = pltpu.PrefetchScalarGridSpec(
        num_scalar_prefetch=1,
        grid=(t // tc,),
        in_specs=[pl.BlockSpec((tc, d), lambda i, pos: (i, 0)), pl.BlockSpec((tc, LANES), lambda i, pos: (i, 0)),
                  pl.BlockSpec(memory_space=pl.ANY)],
        out_specs=pl.BlockSpec((tc, d), lambda i, pos: (i, 0)),
        scratch_shapes=[pltpu.VMEM((TOP_K, tc, d), F32), pltpu.SemaphoreType.DMA(())],
    )
    return pl.pallas_call(
        functools.partial(_combine_kernel, tc=tc, n_tok=t),
        grid_spec=grid_spec,
        out_shape=jax.ShapeDtypeStruct((t, d), F32),
        compiler_params=_cparams(("arbitrary",)),
        name="combine_gather",
    )(pos, x, rw, ys)


def _routing_plan(ridx, rank, counts, tm, sub, npad):
    alloc = ((counts + tm - 1) // tm) * tm
    used = ((counts + sub - 1) // sub) * sub
    ends = jnp.cumsum(alloc)
    starts = ends - alloc
    hit = ridx[:, :, None] == jnp.arange(N_EXPERTS, dtype=jnp.int32)[None, None, :]
    pos = rank + jnp.sum(jnp.where(hit, starts[None, None, :], 0), axis=-1)
    n_tiles = (ends[-1] // tm).astype(jnp.int32).reshape(1)
    tile_row = jnp.arange(npad // tm, dtype=jnp.int32) * tm
    tile_expert = jnp.sum(jnp.minimum(tile_row, ends[-1] - tm)[:, None] >= ends[None, :], axis=-1).astype(jnp.int32)
    used_end = (starts + used)[tile_expert]
    tile_parts = jnp.where(tile_row < ends[-1], jnp.clip((used_end - tile_row) // sub, 0, tm // sub), 0)
    total = jnp.full((1,), npad, jnp.int32)
    pad_ranges = jnp.concatenate([starts + counts, starts + used,
                                  starts + used, ends[-1:], ends, total])
    return (pos.T.reshape(-1).astype(jnp.int32), pad_ranges.astype(jnp.int32), tile_expert,
            tile_parts.astype(jnp.int32), n_tiles)


def moe(h, x, ridx, rw, rank, counts, wg, wu, wd, *, tm=1024, sub=512, tf=256, tg=256):
    t, d = x.shape
    assert tm % sub == 0 and sub % tg == 0 and t % tg == 0
    npad = (TOP_K * t + N_EXPERTS * (tm - 1)) // tm * tm
    pos, pad_ranges, tile_expert, tile_parts, n_tiles = _routing_plan(
        ridx[:, :TOP_K], rank[:, :TOP_K], counts[0, :N_EXPERTS], tm, sub, npad)
    xs = dispatch_rows(h, pos, pad_ranges, npad, tg=tg)
    ys = moe_ffn(xs, tile_expert, tile_parts, n_tiles, wg, wu, wd, tm=tm, sub=sub, tf=tf)
    return combine_rows(x, rw, ys, pos, tc=tg)


def _split_w_in(w):
    sizes = (ATTN_WIDTH, ATTN_WIDTH, ATTN_WIDTH, 3 * GDN_WIDTH, GDN_WIDTH, GDN_HEADS, GDN_HEADS,
             SSM_WIDTH, SSM_XBC, SSM_HEADS)
    pieces, start = [], 0
    for size in sizes:
        pieces.append(w[:, start:start + size])
        start += size
    aq, ak, av, gqkv, ggate, gbeta, ga, sz, sxbc, sdt = pieces
    main = jnp.concatenate([gqkv, ggate, sz, sxbc, aq, ak, av], axis=1).astype(BF16)
    small = jnp.concatenate([gbeta, ga, sdt], axis=1)
    small = jnp.pad(small, ((0, 0), (0, LANES - small.shape[1]))).astype(BF16)
    return main, small


def kernel(x, rel_bias, mix_norm, w_in, attn_q_gain, attn_k_gain, attn_lambda, attn_sub_gain, gdn_conv_w, gdn_A_log, gdn_dt_bias, gdn_o_gain, ssm_conv_w, ssm_conv_b, ssm_A_log, ssm_dt_bias, ssm_D, ssm_norm_gain, w_out, ffn_norm, ffn_w_gate, ffn_w_up, ffn_w_down, moe_router, moe_w_gate, moe_w_up, moe_w_down):
    b, s, d = x.shape
    t = b * s
    depth = w_in.shape[0]
    tiles = _tile_plan(t, s)
    bias_tiles = _bias_tiles(rel_bias, tiles["attn_tq"])
    xf = x.reshape(t, d).astype(F32)
    for li in range(depth):
        w_main, w_small = _split_w_in(w_in[li])
        proj, small = norm_inproj(xf, mix_norm[li].astype(F32), w_main, w_small, tm=tiles["proj_tm"])
        proj3 = proj.reshape(b, s, N_MAIN)
        small3 = small.reshape(b, s, LANES)
        attn_o = attention(proj3, attn_q_gain[li], attn_k_gain[li], attn_lambda[li], attn_sub_gain[li],
                           bias_tiles, li, tq=tiles["attn_tq"])
        gdn_o = gdn(proj3, small3, gdn_conv_w[li], gdn_A_log[li], gdn_dt_bias[li], gdn_o_gain[li],
                    tile=tiles["scan_tile"])
        ssm_o = ssd(proj3, small3, ssm_conv_w[li], ssm_conv_b[li], ssm_A_log[li], ssm_dt_bias[li],
                    ssm_D[li], ssm_norm_gain[li], tile=tiles["scan_tile"])
        wo = w_out[li].astype(BF16)
        mix = (attn_o.reshape(t, ATTN_WIDTH), gdn_o.reshape(t, GDN_WIDTH), ssm_o.reshape(t, SSM_WIDTH))
        j = li // 2
        if li % 2 == 0:
            xf, h = outproj(*mix, wo, xf, ffn_norm[li], tm=tiles["proj_tm"])
            xf = ffn(h, xf, ffn_w_gate[j], ffn_w_up[j], ffn_w_down[j], tm=tiles["ffn_tm"])
        else:
            w_r = jnp.pad(moe_router[j], ((0, 0), (0, LANES - N_EXPERTS))).astype(BF16)
            xf, h, ridx, rw, rank, counts = outproj(*mix, wo, xf, ffn_norm[li], w_r, tm=tiles["proj_tm"])
            xf = moe(h, xf, ridx, rw, rank, counts, moe_w_gate[j], moe_w_up[j], moe_w_down[j],
                     tm=tiles["moe_tm"], sub=tiles["moe_tm"] // 2, tg=tiles["row_dma_tile"])
    return xf.reshape(b, s, d).astype(x.dtype)
```
